```python
import jax
import jax.numpy as jnp
from jax import lax

D_MODEL = 1024
BATCH = 8
SEQ = 2048
DEPTH = 4
DEC_BATCH = 128
DEC_SEQ = 1
PAST_LEN = 2048
PAGE_SIZE = 128

N_POOL_LAYERS = DEPTH // 2
N_ATTN_LAYERS = DEPTH - N_POOL_LAYERS
POOL_WINDOWS = (2, 4, 8, 16)
POOL_GROUP_W = D_MODEL // len(POOL_WINDOWS)
POOL_HIST = max(POOL_WINDOWS) - 1
D_FF = 2816
CONV_W = 3
ATTN_GROUPS = ((128, 1), (512, 4), (2048, 16))
N_GROUPS = len(ATTN_GROUPS)
HEADS_PER_GROUP = 8
HEAD_DIM = 64
ROT_DIM = HEAD_DIM // 4
ROPE_THETA = 500000.0
ATTN_BLOCK = 128
NORM_EPS = 1e-6
NEG_INF = -1e30

kernel_name = "yoco_pool_dilated_decoder_step"


def rms_norm(x, g):
    xf = x.astype(jnp.float32)
    y = xf * lax.rsqrt(jnp.mean(xf * xf, axis=-1, keepdims=True) + NORM_EPS)
    return (y * g.astype(jnp.float32)).astype(x.dtype)


def rotary(x, pos):
    inv_freq = ROPE_THETA ** (-jnp.arange(0, ROT_DIM, 2, dtype=jnp.float32) / ROT_DIM)
    ang = pos.astype(jnp.float32)[:, None] * inv_freq[None, :]
    shape = (1, pos.shape[0]) + (1,) * (x.ndim - 3) + (ROT_DIM // 2,)
    cos = jnp.cos(ang).reshape(shape)
    sin = jnp.sin(ang).reshape(shape)
    xf = x.astype(jnp.float32)
    x1 = xf[..., : ROT_DIM // 2]
    x2 = xf[..., ROT_DIM // 2: ROT_DIM]
    out = jnp.concatenate([x1 * cos - x2 * sin, x2 * cos + x1 * sin, xf[..., ROT_DIM:]], axis=-1)
    return out.astype(x.dtype)


def pool_mixer(ext, n_hist, pos, w_pool, scale):
    B, L, D = ext.shape
    T = L - n_hist
    ef = ext.astype(jnp.float32)
    csum = jnp.concatenate([jnp.zeros((B, 1, D), jnp.float32), jnp.cumsum(ef, axis=1)], axis=1)
    hi = csum[:, n_hist + 1:]
    cur = ef[:, n_hist:]
    t_end = jnp.arange(T) + n_hist + 1
    parts = []
    for g, w in enumerate(POOL_WINDOWS):
        sl = slice(g * POOL_GROUP_W, (g + 1) * POOL_GROUP_W)
        lo = jnp.take(csum[..., sl], jnp.maximum(t_end - w, 0), axis=1)
        cnt = jnp.minimum(pos + 1, w).astype(jnp.float32)[None, :, None]
        parts.append((hi[..., sl] - lo) / cnt - cur[..., sl])
    d = jnp.stack(parts, axis=2)
    mixed = jnp.einsum('btgc,gce->btge', d, w_pool.astype(jnp.float32)).reshape(B, T, D)
    return (mixed * scale.astype(jnp.float32)).astype(ext.dtype)


def conv_ffn(x, hist, g, w_up, cw, cb, w_down):
    h = rms_norm(x, g)
    u = jnp.einsum('btd,df->btf', h, w_up)
    ext = jnp.concatenate([hist.astype(u.dtype), u], axis=1)
    T = u.shape[1]
    c = cb + sum(cw[j] * ext[:, j: j + T] for j in range(CONV_W))
    a, b = jnp.split(c, 2, axis=-1)
    out = jnp.einsum('btf,fd->btd', jax.nn.silu(a) * b, w_down)
    return out, ext[:, -(CONV_W - 1):]


def shared_kv(x, pos, g_kv, w_kv, k_gain):
    B, T, _ = x.shape
    kv = jnp.einsum('btd,de->bte', rms_norm(x, g_kv), w_kv)
    kv = kv.reshape(B, T, 2, N_GROUPS, HEADS_PER_GROUP, HEAD_DIM)
    k = rotary(rms_norm(kv[:, :, 0], k_gain), pos)
    return k, kv[:, :, 1]


def project_q(x, pos, g, w_q, q_gain):
    B, T, _ = x.shape
    q = jnp.einsum('btd,de->bte', rms_norm(x, g), w_q)
    q = q.reshape(B, T, N_GROUPS, HEADS_PER_GROUP, HEAD_DIM)
    return rotary(rms_norm(q, q_gain), pos) * (HEAD_DIM ** -0.5)


def dilated_prompt(q, k, v, dil, n_back):
    B, S, H, E = q.shape
    L = S // dil
    nb = -(-L // ATTN_BLOCK)
    Lp = nb * ATTN_BLOCK

    def blocks(a):
        a = a.reshape(B, L, dil, H, E)
        a = jnp.pad(a, ((0, 0), (0, Lp - L), (0, 0), (0, 0), (0, 0)))
        return a.reshape(B, nb, ATTN_BLOCK, dil, H, E)

    def with_prev(a):
        prev = jnp.concatenate([jnp.zeros_like(a[:, :1]), a[:, :-1]], axis=1)
        return jnp.concatenate([prev, a], axis=2)

    qb = blocks(q)
    kk = with_prev(blocks(k))
    vv = with_prev(blocks(v))
    s = jnp.einsum('bnqrhe,bnkrhe->bnrhqk', qb, kk).astype(jnp.float32)
    qi = jnp.arange(ATTN_BLOCK)[:, None] + ATTN_BLOCK
    ki = jnp.arange(2 * ATTN_BLOCK)[None, :]
    dist = qi - ki
    key_idx = jnp.arange(nb)[:, None, None] * ATTN_BLOCK - ATTN_BLOCK + ki[None]
    mask = ((dist >= 0) & (dist <= n_back))[None] & (key_idx >= 0)
    s = jnp.where(mask[None, :, None, None], s, NEG_INF)
    lse = jax.nn.logsumexp(s, axis=-1)
    p = jnp.exp(s - lse[..., None])
    o = jnp.einsum('bnrhqk,bnkrhe->bnqrhe', p, vv.astype(jnp.float32))
    o = o.reshape(B, Lp, dil, H, E)[:, :L].reshape(B, S, H, E)
    lse = jnp.transpose(lse, (0, 1, 4, 2, 3)).reshape(B, Lp, dil, H)[:, :L].reshape(B, S, H)
    return o, lse


def dilated_sample(q, k_buf, v_buf, k_new, v_new, dil, n_back):
    n_buf = k_buf.shape[1]
    T = q.shape[1]
    kext = jnp.concatenate([k_buf.astype(k_new.dtype), k_new], axis=1)
    vext = jnp.concatenate([v_buf.astype(v_new.dtype), v_new], axis=1)
    idx = n_buf + jnp.arange(T)[:, None] - dil * jnp.arange(n_back + 1)[None, :]
    valid = idx >= 0
    idx = jnp.maximum(idx, 0)
    kg = jnp.take(kext, idx, axis=1)
    vg = jnp.take(vext, idx, axis=1)
    s = jnp.einsum('bthe,btkhe->bthk', q, kg).astype(jnp.float32)
    s = jnp.where(valid[None, :, None, :], s, NEG_INF)
    lse = jax.nn.logsumexp(s, axis=-1)
    p = jnp.exp(s - lse[..., None])
    o = jnp.einsum('bthk,btkhe->bthe', p, vg.astype(jnp.float32))
    return o, lse


def dilated_mixer(q, k, v, kv_hist, w_o):
    B, T = q.shape[:2]
    outs, lses = [], []
    for g, (win, dil) in enumerate(ATTN_GROUPS):
        n_back = win // dil
        if kv_hist is None:
            o, lse = dilated_prompt(q[:, :, g], k[:, :, g], v[:, :, g], dil, n_back)
        else:
            buf = kv_hist[g]
            o, lse = dilated_sample(q[:, :, g], buf[:, :, 0], buf[:, :, 1], k[:, :, g], v[:, :, g], dil, n_back)
        outs.append(o)
        lses.append(lse)
    wts = jax.nn.softmax(jnp.stack(lses, axis=0), axis=0)
    o = jnp.sum(wts[..., None] * jnp.stack(outs, axis=0), axis=0)
    return jnp.einsum('bte,ed->btd', o.reshape(B, T, HEADS_PER_GROUP * HEAD_DIM).astype(q.dtype), w_o)


def trunk(x, pos, pool_hist, conv_hist, kv_hist, g_pool, w_pool, pool_scale, g_ffn, w_up, conv_w,
          conv_b, w_down, g_kv, w_kv, k_gain, g_attn, w_q, q_gain, w_o):
    B, T, _ = x.shape
    new_pool, new_conv = [], []
    k_sh = v_sh = None
    for layer in range(DEPTH):
        if layer < N_POOL_LAYERS:
            xn = rms_norm(x, g_pool[layer])
            if pool_hist is None:
                ext, n_hist = xn, 0
            else:
                ext = jnp.concatenate([pool_hist[layer].astype(xn.dtype), xn], axis=1)
                n_hist = pool_hist.shape[2]
            x = x + pool_mixer(ext, n_hist, pos, w_pool[layer], pool_scale[layer])
            new_pool.append(ext[:, -POOL_HIST:])
        else:
            if layer == N_POOL_LAYERS:
                k_sh, v_sh = shared_kv(x, pos, g_kv, w_kv, k_gain)
            j = layer - N_POOL_LAYERS
            q = project_q(x, pos, g_attn[j], w_q[j], q_gain[j])
            x = x + dilated_mixer(q, k_sh, v_sh, kv_hist, w_o[j])
        if conv_hist is None:
            hist = jnp.zeros((B, CONV_W - 1, 2 * D_FF), x.dtype)
        else:
            hist = conv_hist[layer]
        f_out, h_new = conv_ffn(x, hist, g_ffn[layer], w_up[layer], conv_w[layer], conv_b[layer], w_down[layer])
        x = x + f_out
        new_conv.append(h_new)
    kv_new = []
    for g, (win, _) in enumerate(ATTN_GROUPS):
        rows = jnp.stack([k_sh[:, :, g], v_sh[:, :, g]], axis=2)
        kv_new.append(rows[:, -min(win, T):] if kv_hist is None else rows)
    return x, jnp.stack(new_pool, axis=0), jnp.stack(new_conv, axis=0), kv_new


def setup_inputs(seed: int = 0) -> dict:
    key = jax.random.key(seed)
    ks = jax.random.split(key, 22)
    f32 = jnp.float32

    def nrm(k, shape, s):
        return jax.random.normal(k, shape, f32) * s

    qkv_w = N_GROUPS * HEADS_PER_GROUP * HEAD_DIM
    out_w = HEADS_PER_GROUP * HEAD_DIM
    kv_shape = lambda win: (DEC_BATCH, min(win, PAST_LEN), 2, HEADS_PER_GROUP, HEAD_DIM)
    return {
        'x_prompt': nrm(ks[0], (BATCH, SEQ, D_MODEL), 1.0),
        'x_sample': nrm(ks[1], (DEC_BATCH, DEC_SEQ, D_MODEL), 1.0),
        'state_pool': nrm(ks[2], (N_POOL_LAYERS, DEC_BATCH, POOL_HIST, D_MODEL), 1.0),
        'state_conv': nrm(ks[3], (DEPTH, DEC_BATCH, CONV_W - 1, 2 * D_FF), 1.0),
        'cache_kv_w128': nrm(ks[4], kv_shape(ATTN_GROUPS[0][0]), 1.0),
        'cache_kv_w512': nrm(ks[5], kv_shape(ATTN_GROUPS[1][0]), 1.0),
        'cache_kv_w2048': nrm(ks[6], kv_shape(ATTN_GROUPS[2][0]), 1.0),
        'g_pool': 1.0 + nrm(ks[7], (N_POOL_LAYERS, D_MODEL), 0.05),
        'w_pool': nrm(ks[8], (N_POOL_LAYERS, len(POOL_WINDOWS), POOL_GROUP_W, POOL_GROUP_W), POOL_GROUP_W ** -0.5),
        'pool_scale': 1.0 + nrm(ks[9], (N_POOL_LAYERS, D_MODEL), 0.05),
        'g_ffn': 1.0 + nrm(ks[10], (DEPTH, D_MODEL), 0.05),
        'w_up': nrm(ks[11], (DEPTH, D_MODEL, 2 * D_FF), D_MODEL ** -0.5),
        'conv_w': nrm(ks[12], (DEPTH, CONV_W, 2 * D_FF), CONV_W ** -0.5),
        'conv_b': nrm(ks[13], (DEPTH, 2 * D_FF), 0.01),
        'w_down': nrm(ks[14], (DEPTH, D_FF, D_MODEL), D_FF ** -0.5),
        'g_kv': 1.0 + nrm(ks[15], (D_MODEL,), 0.05),
        'w_kv': nrm(ks[16], (D_MODEL, 2 * qkv_w), D_MODEL ** -0.5),
        'k_gain': 1.0 + nrm(ks[17], (HEAD_DIM,), 0.05),
        'g_attn': 1.0 + nrm(ks[18], (N_ATTN_LAYERS, D_MODEL), 0.05),
        'w_q': nrm(ks[19], (N_ATTN_LAYERS, D_MODEL, qkv_w), D_MODEL ** -0.5),
        'q_gain': 1.0 + nrm(ks[20], (N_ATTN_LAYERS, HEAD_DIM), 0.05),
        'w_o': nrm(ks[21], (N_ATTN_LAYERS, out_w, D_MODEL), out_w ** -0.5),
    }


def reference(x_prompt, x_sample, state_pool, state_conv, cache_kv_w128, cache_kv_w512, cache_kv_w2048,
              g_pool, w_pool, pool_scale, g_ffn, w_up, conv_w, conv_b, w_down, g_kv, w_kv, k_gain,
              g_attn, w_q, q_gain, w_o):
    pos_p = jnp.arange(x_prompt.shape[1], dtype=jnp.int32)
    y_p, pool_p, conv_p, kv_p = trunk(x_prompt, pos_p, None, None, None, g_pool, w_pool, pool_scale,
                                      g_ffn, w_up, conv_w, conv_b, w_down, g_kv, w_kv, k_gain,
                                      g_attn, w_q, q_gain, w_o)
    pos_s = PAST_LEN + jnp.arange(x_sample.shape[1], dtype=jnp.int32)
    y_s, pool_s, conv_s, kv_s = trunk(x_sample, pos_s, state_pool, state_conv,
                                      (cache_kv_w128, cache_kv_w512, cache_kv_w2048),
                                      g_pool, w_pool, pool_scale, g_ffn, w_up, conv_w, conv_b, w_down,
                                      g_kv, w_kv, k_gain, g_attn, w_q, q_gain, w_o)
    return (y_p, y_s, pool_p, pool_s, conv_p, conv_s, kv_p[0], kv_s[0], kv_p[1], kv_s[1], kv_p[2], kv_s[2])
```

```python
import functools

import jax
import jax.numpy as jnp
from jax import lax
from jax.experimental import pallas as pl
from jax.experimental.pallas import tpu as pltpu

F32 = jnp.float32
BF16 = jnp.bfloat16

NORM_EPS = 1e-6
NEG_INF = -1e30
POOL_WINDOWS = (2, 4, 8, 16)
POOL_HIST = max(POOL_WINDOWS) - 1
CONV_W = 3
ATTN_GROUPS = ((128, 1), (512, 4), (2048, 16))
N_GROUPS = len(ATTN_GROUPS)
HEADS = 8
HEAD_DIM = 64
ROT_DIM = HEAD_DIM // 4
ROPE_THETA = 500000.0
ATTN_BLOCK = 128
PAST_LEN = 2048

LANES = 128
SUBLANES = 8
VMEM_LIMIT = 56 * 1024 * 1024


def _cparams(n_grid):
    return pltpu.CompilerParams(dimension_semantics=("arbitrary",) * n_grid,
                                vmem_limit_bytes=VMEM_LIMIT)


def _const_spec(shape):
    nd = len(shape)
    return pl.BlockSpec(shape, lambda *_: (0,) * nd, pipeline_mode=pl.Buffered(1))


def _rms(x, g):
    ms = jnp.mean(x * x, axis=-1, keepdims=True)
    return x * lax.rsqrt(ms + NORM_EPS) * g


def _silu(a):
    return a / (1.0 + jnp.exp(-a))


def _ffn_core(x1, gf_ref, wup_ref, cw_ref, cb_ref, wdown_ref, hbuf, act, fc, prev_rows):
    d_ff = wdown_ref.shape[0]
    hbuf[...] = _rms(x1, gf_ref[...]).astype(BF16)
    for c in range(d_ff // fc):
        halves = []
        for base in (0, d_ff):
            cols = slice(base + c * fc, base + (c + 1) * fc)
            u = jnp.dot(hbuf[...], wup_ref[:, cols], preferred_element_type=F32)
            u1, u2 = prev_rows(cols, u)
            conv = cb_ref[:, cols] + ((cw_ref[0:1, cols] * u2 + cw_ref[1:2, cols] * u1)
                                      + cw_ref[2:3, cols] * u)
            halves.append(conv)
        a, b = halves
        act[:, c * fc:(c + 1) * fc] = (_silu(a) * b).astype(BF16)
    return x1 + jnp.dot(act[...], wdown_ref[...], preferred_element_type=F32)


def _prompt_prev_rows(ubuf, ucarry, tm):
    def prev_rows(cols, u):
        ubuf[0:SUBLANES, :] = ucarry[:, cols]
        ubuf[SUBLANES:SUBLANES + tm, :] = u
        ucarry[:, cols] = ubuf[tm:tm + SUBLANES, :]
        return (ubuf[SUBLANES - 1:SUBLANES - 1 + tm, :], ubuf[SUBLANES - 2:SUBLANES - 2 + tm, :])
    return prev_rows


def _sample_prev_rows(hist_ref, newhist_ref):
    def prev_rows(cols, u):
        newhist_ref[0, :, cols] = hist_ref[1, :, cols]
        newhist_ref[1, :, cols] = u
        return hist_ref[1, :, cols], hist_ref[0, :, cols]
    return prev_rows


def _pool_mix(xn, shifted, cnt_of, wp_ref, ps_ref):
    gw = xn.shape[1] // len(POOL_WINDOWS)
    parts = []
    for g, w in enumerate(POOL_WINDOWS):
        sl = slice(g * gw, (g + 1) * gw)
        cur = xn[:, sl]
        s = cur
        for k in range(1, w):
            s = s + shifted(k, sl)
        d = s / cnt_of(w, cur.shape) - cur
        parts.append(jnp.dot(d.astype(BF16), wp_ref[g], preferred_element_type=F32))
    return jnp.concatenate(parts, axis=-1) * ps_ref[...]


def _prompt_layer_kernel(mode, tm, fc, *refs):
    if mode == "pool":
        (x_ref, gp_ref, wp_ref, ps_ref, gf_ref, wup_ref, cw_ref, cb_ref, wdown_ref,
         xo_ref, ptail_ref, ctail_ref, xnbuf, hbuf, act, ubuf, ucarry) = refs
    else:
        (x_ref, o_ref, wo_ref, gf_ref, wup_ref, cw_ref, cb_ref, wdown_ref,
         xo_ref, ctail_ref, hbuf, act, ubuf, ucarry) = refs
    t = pl.program_id(1)
    hist = 2 * SUBLANES

    @pl.when(t == 0)
    def _():
        ucarry[...] = jnp.zeros_like(ucarry)
        if mode == "pool":
            xnbuf[0:hist, :] = jnp.zeros((hist, xnbuf.shape[1]), F32)

    x = x_ref[...]
    if mode == "pool":
        xn = _rms(x, gp_ref[...])
        xnbuf[hist:hist + tm, :] = xn

        def shifted(k, sl):
            return xnbuf[hist - k:hist - k + tm, sl]

        def cnt_of(w, shape):
            pos = t * tm + lax.broadcasted_iota(jnp.int32, shape, 0)
            return jnp.minimum(pos + 1, w).astype(F32)

        x1 = x + _pool_mix(xn, shifted, cnt_of, wp_ref, ps_ref)
        tail = xnbuf[tm:tm + hist, :]
        ptail_ref[...] = tail
        xnbuf[0:hist, :] = tail
    else:
        x1 = x + jnp.dot(o_ref[...], wo_ref[...], preferred_element_type=F32)

    xo_ref[...] = _ffn_core(x1, gf_ref, wup_ref, cw_ref, cb_ref, wdown_ref, hbuf, act, fc,
                            _prompt_prev_rows(ubuf, ucarry, tm))
    ctail_ref[...] = ucarry[...]


def _sample_layer_kernel(mode, fc, *refs):
    if mode == "pool":
        (x_ref, st_ref, gp_ref, wp_ref, ps_ref, gf_ref, wup_ref, cw_ref, cb_ref, wdown_ref,
         hist_ref, xo_ref, nst_ref, nhist_ref, hbuf, act) = refs
    else:
        (x_ref, o_ref, wo_ref, gf_ref, wup_ref, cw_ref, cb_ref, wdown_ref,
         hist_ref, xo_ref, nhist_ref, hbuf, act) = refs
    x = x_ref[...]
    if mode == "pool":
        xn = _rms(x, gp_ref[...])
        n_hist = st_ref.shape[0]

        def shifted(k, sl):
            return st_ref[n_hist - k, :, sl]

        def cnt_of(w, shape):
            return float(min(PAST_LEN + 1, w))

        x1 = x + _pool_mix(xn, shifted, cnt_of, wp_ref, ps_ref)
        for i in range(n_hist - 1):
            nst_ref[i] = st_ref[i + 1]
        nst_ref[n_hist - 1] = xn
    else:
        x1 = x + jnp.dot(o_ref[...], wo_ref[...], preferred_element_type=F32)
    xo_ref[...] = _ffn_core(x1, gf_ref, wup_ref, cw_ref, cb_ref, wdown_ref, hbuf, act, fc,
                            _sample_prev_rows(hist_ref, nhist_ref))


def _ffn_weight_specs(d, f2, f):
    return [_const_spec((1, d)), _const_spec((d, f2)), _const_spec((CONV_W, f2)),
            _const_spec((1, f2)), _const_spec((f, d))]


def _pick_fc(f):
    for fc in (256, 128):
        if f % fc == 0:
            return fc
    return f


def _prompt_layer(mode, x, pre, ffn_w, tm):
    b, s, d = x.shape
    gf, wup, cw, cb, wdown = ffn_w
    f2, f = wup.shape[1], wdown.shape[0]
    fc = _pick_fc(f)
    row = pl.BlockSpec((None, tm, d), lambda i, j: (i, j, 0))
    hist = 2 * SUBLANES
    ctail_spec = pl.BlockSpec((None, SUBLANES, f2), lambda i, j: (i, 0, 0))
    ctail_shape = jax.ShapeDtypeStruct((b, SUBLANES, f2), F32)
    common_scratch = [pltpu.VMEM((tm, d), BF16), pltpu.VMEM((tm, f), BF16),
                      pltpu.VMEM((tm + SUBLANES, fc), F32), pltpu.VMEM((SUBLANES, f2), F32)]
    if mode == "pool":
        gp, wp, ps = pre
        in_specs = [row, _const_spec((1, d)), _const_spec(wp.shape), _const_spec((1, d))]
        args = [x, gp, wp, ps]
        out_specs = [row, pl.BlockSpec((None, hist, d), lambda i, j: (i, 0, 0)), ctail_spec]
        out_shape = [jax.ShapeDtypeStruct(x.shape, F32), jax.ShapeDtypeStruct((b, hist, d), F32),
                     ctail_shape]
        scratch = [pltpu.VMEM((tm + hist, d), F32)] + common_scratch
    else:
        o, wo = pre
        in_specs = [row, pl.BlockSpec((None, tm, o.shape[2]), lambda i, j: (i, j, 0)),
                    _const_spec(wo.shape)]
        args = [x, o, wo]
        out_specs = [row, ctail_spec]
        out_shape = [jax.ShapeDtypeStruct(x.shape, F32), ctail_shape]
        scratch = common_scratch
    return pl.pallas_call(
        functools.partial(_prompt_layer_kernel, mode, tm, fc),
        grid=(b, s // tm),
        in_specs=in_specs + _ffn_weight_specs(d, f2, f),
        out_specs=out_specs, out_shape=out_shape, scratch_shapes=scratch,
        compiler_params=_cparams(2), name=f"prompt_{mode}_layer",
    )(*args, gf, wup, cw, cb, wdown)


def _sample_layer(mode, x, pre, ffn_w, hist_t, bn):
    n, d = x.shape
    gf, wup, cw, cb, wdown = ffn_w
    f2, f = wup.shape[1], wdown.shape[0]
    fc = _pick_fc(f)
    row = lambda width: pl.BlockSpec((bn, width), lambda i: (i, 0))
    tmaj = lambda a: pl.BlockSpec((a.shape[0], bn, a.shape[2]), lambda i: (0, i, 0))
    if mode == "pool":
        st_t, gp, wp, ps = pre
        in_specs = [row(d), tmaj(st_t), _const_spec((1, d)), _const_spec(wp.shape),
                    _const_spec((1, d))]
        args = [x, st_t, gp, wp, ps]
        out_specs = [row(d), tmaj(st_t), tmaj(hist_t)]
        out_shape = [jax.ShapeDtypeStruct(x.shape, F32), jax.ShapeDtypeStruct(st_t.shape, F32),
                     jax.ShapeDtypeStruct(hist_t.shape, F32)]
    else:
        o, wo = pre
        in_specs = [row(d), row(o.shape[1]), _const_spec(wo.shape)]
        args = [x, o, wo]
        out_specs = [row(d), tmaj(hist_t)]
        out_shape = [jax.ShapeDtypeStruct(x.shape, F32), jax.ShapeDtypeStruct(hist_t.shape, F32)]
    return pl.pallas_call(
        functools.partial(_sample_layer_kernel, mode, fc),
        grid=(n // bn,),
        in_specs=in_specs + _ffn_weight_specs(d, f2, f) + [tmaj(hist_t)],
        out_specs=out_specs, out_shape=out_shape,
        scratch_shapes=[pltpu.VMEM((bn, d), BF16), pltpu.VMEM((bn, f), BF16)],
        compiler_params=_cparams(1), name=f"sample_{mode}_layer",
    )(*args, gf, wup, cw, cb, wdown, hist_t)


def _head_norm_rot(blk, gain, cosb, sina, sinb):
    lo = lax.broadcasted_iota(jnp.int32, blk.shape, 1) < HEAD_DIM
    sq = blk * blk
    s_lo = jnp.sum(jnp.where(lo, sq, 0.0), axis=-1, keepdims=True)
    s_hi = jnp.sum(jnp.where(lo, 0.0, sq), axis=-1, keepdims=True)
    ms = jnp.where(lo, s_lo, s_hi) * (1.0 / HEAD_DIM)
    y = blk * lax.rsqrt(ms + NORM_EPS) * gain
    return (y * cosb + pltpu.roll(y, LANES - ROT_DIM // 2, 1) * sina
            + pltpu.roll(y, ROT_DIM // 2, 1) * sinb)


def _proj_kernel(do_kv, *refs):
    if do_kv:
        (x_ref, cos_ref, sina_ref, sinb_ref, gkv_ref, wkv_ref, kg_ref, gq_ref, wq_ref, qg_ref,
         kv0_ref, kv1_ref, kv2_ref, q_ref) = refs
    else:
        (x_ref, cos_ref, sina_ref, sinb_ref, gq_ref, wq_ref, qg_ref, q_ref) = refs
    x = x_ref[...]
    cosb, sina, sinb = cos_ref[...], sina_ref[...], sinb_ref[...]
    gw = HEADS * HEAD_DIM
    if do_kv:
        h = _rms(x, gkv_ref[...]).astype(BF16)
        kv = jnp.dot(h, wkv_ref[...], preferred_element_type=F32)
        for g, out_ref in enumerate((kv0_ref, kv1_ref, kv2_ref)):
            for j in range(gw // LANES):
                c0 = g * gw + j * LANES
                out_ref[:, j * LANES:(j + 1) * LANES] = _head_norm_rot(
                    kv[:, c0:c0 + LANES], kg_ref[...], cosb, sina, sinb)
            v0 = N_GROUPS * gw + g * gw
            out_ref[:, gw:2 * gw] = kv[:, v0:v0 + gw]
    h = _rms(x, gq_ref[...]).astype(BF16)
    q = jnp.dot(h, wq_ref[...], preferred_element_type=F32)
    for j in range(q.shape[1] // LANES):
        q_ref[:, j * LANES:(j + 1) * LANES] = _head_norm_rot(
            q[:, j * LANES:(j + 1) * LANES], qg_ref[...], cosb, sina, sinb) * (HEAD_DIM ** -0.5)


def _proj(x2d, tabs, n_tab_blocks, kv_w, q_w, tm):
    r, d = x2d.shape
    gq, wq, qg = q_w
    row = pl.BlockSpec((tm, d), lambda i: (i, 0))
    tab = pl.BlockSpec((tm, LANES), lambda i: (i % n_tab_blocks, 0))
    gw = HEADS * HEAD_DIM
    in_specs = [row, tab, tab, tab]
    args = [x2d, *tabs]
    out_specs, out_shape = [], []
    if kv_w is not None:
        gkv, wkv, kg = kv_w
        in_specs += [_const_spec((1, d)), _const_spec(wkv.shape), _const_spec((1, LANES))]
        args += [gkv, wkv, kg]
        out_specs += [pl.BlockSpec((tm, 2 * gw), lambda i: (i, 0))] * N_GROUPS
        out_shape += [jax.ShapeDtypeStruct((r, 2 * gw), F32)] * N_GROUPS
    in_specs += [_const_spec((1, d)), _const_spec(wq.shape), _const_spec((1, LANES))]
    args += [gq, wq, qg]
    out_specs.append(pl.BlockSpec((tm, wq.shape[1]), lambda i: (i, 0)))
    out_shape.append(jax.ShapeDtypeStruct((r, wq.shape[1]), F32))
    return pl.pallas_call(
        functools.partial(_proj_kernel, kv_w is not None),
        grid=(r // tm,), in_specs=in_specs, out_specs=out_specs, out_shape=out_shape,
        compiler_params=_cparams(1), name="proj_kvq" if kv_w is not None else "proj_q",
    )(*args)


def _prompt_attn_kernel(seq, *refs):
    q_refs, k_refs, v_refs = refs[0:3], refs[3:6], refs[6:9]
    o_ref = refs[9]
    qb, kb, vb, accs, ms, ls = refs[10:16]
    blk = ATTN_BLOCK
    lane = lax.broadcasted_iota(jnp.int32, (blk, LANES), 1)
    lo = lane < HEAD_DIM
    qi = lax.broadcasted_iota(jnp.int32, (blk, 2 * blk), 0)
    kj = lax.broadcasted_iota(jnp.int32, (blk, 2 * blk), 1)
    band = (kj >= qi) & (kj <= qi + blk)

    for g, (win, dil) in enumerate(ATTN_GROUPS):
        sub = seq // dil
        nb = sub // blk
        pitch = sub + blk
        for r in range(dil):
            rows = pl.ds(r, sub, stride=dil) if dil > 1 else pl.ds(0, sub)
            qf = q_refs[g][rows, :]
            lo_q = lax.broadcasted_iota(jnp.int32, qf.shape, 1) < HEAD_DIM
            qb[0, r * sub:(r + 1) * sub, :] = jnp.where(lo_q, qf, 0.0).astype(BF16)
            qb[1, r * sub:(r + 1) * sub, :] = jnp.where(lo_q, 0.0, qf).astype(BF16)
            kb[r * pitch:r * pitch + blk, :] = jnp.zeros((blk, LANES), BF16)
            vb[r * pitch:r * pitch + blk, :] = jnp.zeros((blk, LANES), BF16)
            kb[r * pitch + blk:(r + 1) * pitch, :] = k_refs[g][rows, :].astype(BF16)
            vb[r * pitch + blk:(r + 1) * pitch, :] = v_refs[g][rows, :].astype(BF16)

        def body(i, carry, g=g, dil=dil, sub=sub, nb=nb, pitch=pitch):
            r = i // nb
            n = i % nb
            q0 = pl.multiple_of(r * sub + n * blk, blk)
            k0 = pl.multiple_of(r * pitch + n * blk, blk)
            kt = kb[pl.ds(k0, 2 * blk), :]
            vt = vb[pl.ds(k0, 2 * blk), :]
            mask = band & (kj >= jnp.where(n == 0, blk, 0))
            acc_t = jnp.zeros((blk, LANES), F32)
            m_t = jnp.zeros((blk, LANES), F32)
            l_t = jnp.zeros((blk, LANES), F32)
            for hd in range(2):
                hsel = lo if hd == 0 else jnp.logical_not(lo)
                qh = qb[hd, pl.ds(q0, blk), :]
                s = lax.dot_general(qh, kt, (((1,), (1,)), ((), ())), preferred_element_type=F32)
                s = jnp.where(mask, s, NEG_INF)
                m = jnp.max(s, axis=-1, keepdims=True)
                p = jnp.exp(s - m)
                l = jnp.sum(p, axis=-1, keepdims=True)
                pv = jnp.dot(p.astype(BF16), vt, preferred_element_type=F32)
                acc_t = jnp.where(hsel, pv, acc_t)
                m_t = jnp.where(hsel, m, m_t)
                l_t = jnp.where(hsel, l, l_t)
            if dil > 1:
                out_rows = pl.ds(r + n * blk * dil, blk, stride=dil)
            else:
                out_rows = pl.ds(pl.multiple_of(n * blk, blk), blk)
            accs[g, out_rows, :] = acc_t
            ms[g, out_rows, :] = m_t
            ls[g, out_rows, :] = l_t
            return carry

        lax.fori_loop(0, dil * nb, body, 0)

    def merge(i, carry):
        rows = pl.ds(pl.multiple_of(i * blk, blk), blk)
        m = jnp.maximum(jnp.maximum(ms[0, rows, :], ms[1, rows, :]), ms[2, rows, :])
        num = jnp.zeros((blk, LANES), F32)
        den = jnp.zeros((blk, LANES), F32)
        for g in range(N_GROUPS):
            w = jnp.exp(ms[g, rows, :] - m)
            num = num + w * accs[g, rows, :]
            den = den + w * ls[g, rows, :]
        o_ref[rows, :] = (num / den).astype(o_ref.dtype)
        return carry

    lax.fori_loop(0, seq // blk, merge, 0)


def _prompt_attn(q, kvs):
    b, s, _ = q.shape
    gw = HEADS * HEAD_DIM
    n_hp = gw // LANES
    blk_spec = lambda col: pl.BlockSpec((None, s, LANES), lambda i, j, col=col: (i, 0, col + j))
    in_specs = ([blk_spec(g * n_hp) for g in range(N_GROUPS)]
                + [blk_spec(0)] * N_GROUPS
                + [blk_spec(n_hp)] * N_GROUPS)
    args = [q] * N_GROUPS + list(kvs) + list(kvs)
    max_pitch_rows = s + ATTN_BLOCK * max(d for _, d in ATTN_GROUPS)
    return pl.pallas_call(
        functools.partial(_prompt_attn_kernel, s),
        grid=(b, n_hp),
        in_specs=in_specs,
        out_specs=pl.BlockSpec((None, s, LANES), lambda i, j: (i, 0, j)),
        out_shape=jax.ShapeDtypeStruct((b, s, gw), BF16),
        scratch_shapes=[pltpu.VMEM((2, s, LANES), BF16),
                        pltpu.VMEM((max_pitch_rows, LANES), BF16),
                        pltpu.VMEM((max_pitch_rows, LANES), BF16),
                        pltpu.VMEM((N_GROUPS, s, LANES), F32),
                        pltpu.VMEM((N_GROUPS, s, LANES), F32),
                        pltpu.VMEM((N_GROUPS, s, LANES), F32)],
        compiler_params=_cparams(2), name="prompt_attn",
    )(*args)


def _sample_attn_kernel(bb, *refs):
    q_ref, kvn_ref, c_refs, o_ref = refs[0], refs[1], refs[2:5], refs[5]

    def body(b, carry):
        m_run = l_run = acc_run = None
        for g in range(N_GROUPS):
            qv = q_ref[b, g]
            k_new = kvn_ref[b, 0, g]
            v_new = kvn_ref[b, 1, g]
            kk = c_refs[g][b, :, 0]
            vv = c_refs[g][b, :, 1]
            s = jnp.sum(kk * qv[None], axis=-1, keepdims=True)
            s_new = jnp.sum(k_new * qv, axis=-1, keepdims=True)
            m = jnp.maximum(jnp.max(s, axis=0), s_new)
            p = jnp.exp(s - m[None])
            p_new = jnp.exp(s_new - m)
            l = jnp.sum(p, axis=0) + p_new
            acc = jnp.sum(p * vv, axis=0) + p_new * v_new
            if g == 0:
                m_run, l_run, acc_run = m, l, acc
            else:
                m_new = jnp.maximum(m_run, m)
                a_old = jnp.exp(m_run - m_new)
                a_cur = jnp.exp(m - m_new)
                l_run = a_old * l_run + a_cur * l
                acc_run = a_old * acc_run + a_cur * acc
                m_run = m_new
        o_ref[b] = acc_run / l_run
        return carry

    lax.fori_loop(0, bb, body, 0)


def _sample_attn(q4, kvn, caches, bb):
    n = q4.shape[0]
    in_specs = [pl.BlockSpec((bb,) + q4.shape[1:], lambda i: (i, 0, 0, 0)),
                pl.BlockSpec((bb,) + kvn.shape[1:], lambda i: (i, 0, 0, 0, 0))]
    args = [q4, kvn]
    for c, (win, dil) in zip(caches, ATTN_GROUPS):
        n_buf = c.shape[1]
        n_back = win // dil
        assert n_buf == n_back * dil, "window buffer must hold exactly n_back dilated rows"
        c6 = c.reshape(n, n_back, dil, 2, HEADS, HEAD_DIM)
        in_specs.append(pl.BlockSpec((bb, n_back, None, 2, HEADS, HEAD_DIM),
                                     lambda i: (i, 0, 0, 0, 0, 0)))
        args.append(c6)
    return pl.pallas_call(
        functools.partial(_sample_attn_kernel, bb),
        grid=(n // bb,),
        in_specs=in_specs,
        out_specs=pl.BlockSpec((bb, HEADS, HEAD_DIM), lambda i: (i, 0, 0)),
        out_shape=jax.ShapeDtypeStruct((n, HEADS, HEAD_DIM), F32),
        compiler_params=_cparams(1), name="sample_attn",
    )(*args)


def _rot_tables(pos):
    half = ROT_DIM // 2
    inv_freq = ROPE_THETA ** (-jnp.arange(0, ROT_DIM, 2, dtype=F32) / ROT_DIM)
    ang = pos.astype(F32)[:, None] * inv_freq[None, :]
    cos, sin = jnp.cos(ang), jnp.sin(ang)
    t = pos.shape[0]
    pad = jnp.zeros((t, HEAD_DIM - ROT_DIM), F32)
    zer = jnp.zeros((t, half), F32)
    cosb = jnp.concatenate([cos, cos, pad + 1.0], axis=1)
    sina = jnp.concatenate([-sin, zer, pad], axis=1)
    sinb = jnp.concatenate([zer, sin, pad], axis=1)
    return tuple(jnp.tile(a, (1, LANES // HEAD_DIM)) for a in (cosb, sina, sinb))


def kernel(x_prompt, x_sample, state_pool, state_conv, cache_kv_w128, cache_kv_w512, cache_kv_w2048,
           g_pool, w_pool, pool_scale, g_ffn, w_up, conv_w, conv_b, w_down, g_kv, w_kv, k_gain,
           g_attn, w_q, q_gain, w_o):
    b, s, d = x_prompt.shape
    n = x_sample.shape[0]
    depth = w_up.shape[0]
    n_pool = g_pool.shape[0]
    gw = HEADS * HEAD_DIM
    row = lambda a: a.reshape(1, -1)
    lane2 = lambda a: jnp.tile(a, LANES // HEAD_DIM).reshape(1, LANES)
    tm = min(512, s)
    bn = min(64, n)
    bb = min(4, n)

    wup_b, wdown_b = w_up.astype(BF16), w_down.astype(BF16)
    wpool_b, wkv_b, wq_b, wo_b = (w.astype(BF16) for w in (w_pool, w_kv, w_q, w_o))
    ffn_w = [(row(g_ffn[l]), wup_b[l], conv_w[l], row(conv_b[l]), wdown_b[l]) for l in range(depth)]
    kv_w = (row(g_kv), wkv_b, lane2(k_gain))
    q_w = [(row(g_attn[j]), wq_b[j], lane2(q_gain[j])) for j in range(depth - n_pool)]

    tabs_p = _rot_tables(jnp.arange(s, dtype=jnp.int32))
    tabs_s = _rot_tables(jnp.full((n,), PAST_LEN, jnp.int32))
    caches = (cache_kv_w128, cache_kv_w512, cache_kv_w2048)

    xp, xs = x_prompt, x_sample.reshape(n, d)
    pool_p, pool_s, conv_p, conv_s = [], [], [], []
    hist_t = jnp.swapaxes(state_conv, 1, 2)
    st_t = jnp.swapaxes(state_pool, 1, 2)
    kvs_p = kvs_s = None
    for l in range(depth):
        if l < n_pool:
            pre = (row(g_pool[l]), wpool_b[l], row(pool_scale[l]))
            xp, ptail, ctail = _prompt_layer("pool", xp, pre, ffn_w[l], tm)
            pool_p.append(ptail[:, 2 * SUBLANES - POOL_HIST:])
            xs, nst, nhist = _sample_layer("pool", xs, (st_t[l],) + pre, ffn_w[l], hist_t[l], bn)
            pool_s.append(jnp.swapaxes(nst, 0, 1))
        else:
            j = l - n_pool
            if j == 0:
                *kvs_p, q_p = _proj(xp.reshape(b * s, d), tabs_p, s // tm, kv_w, q_w[j], tm)
                *kvs_s, q_s = _proj(xs, tabs_s, 1, kv_w, q_w[j], n)
                kvs_p = [a.reshape(b, s, 2 * gw) for a in kvs_p]
                kvn = jnp.stack(kvs_s, axis=1).reshape(n, N_GROUPS, 2, HEADS, HEAD_DIM).swapaxes(1, 2)
            else:
                (q_p,) = _proj(xp.reshape(b * s, d), tabs_p, s // tm, None, q_w[j], tm)
                (q_s,) = _proj(xs, tabs_s, 1, None, q_w[j], n)
            o_p = _prompt_attn(q_p.reshape(b, s, N_GROUPS * gw), kvs_p)
            o_s = _sample_attn(q_s.reshape(n, N_GROUPS, HEADS, HEAD_DIM), kvn, caches, bb)
            xp, ctail = _prompt_layer("attn", xp, (o_p, wo_b[j]), ffn_w[l], tm)
            xs, nhist = _sample_layer("attn", xs, (o_s.reshape(n, gw).astype(BF16), wo_b[j]),
                                      ffn_w[l], hist_t[l], bn)
        conv_p.append(ctail[:, SUBLANES - (CONV_W - 1):])
        conv_s.append(jnp.swapaxes(nhist, 0, 1))

    outs = [xp, xs.reshape(n, 1, d), jnp.stack(pool_p), jnp.stack(pool_s),
            jnp.stack(conv_p), jnp.stack(conv_s)]
    for g, (win, _) in enumerate(ATTN_GROUPS):
        rows_p = kvs_p[g][:, s - min(win, s):].reshape(b, min(win, s), 2, HEADS, HEAD_DIM)
        outs += [rows_p, kvs_s[g].reshape(n, 1, 2, HEADS, HEAD_DIM)]
    return tuple(outs)
```

```python
import functools

import jax
import jax.numpy as jnp
from jax import lax
from jax.experimental import pallas as pl
from jax.experimental.pallas import tpu as pltpu

F32 = jnp.float32
BF16 = jnp.bfloat16

NORM_EPS = 1e-6
NEG_INF = -1e30
POOL_WINDOWS = (2, 4, 8, 16)
POOL_HIST = max(POOL_WINDOWS) - 1
CONV_W = 3
ATTN_GROUPS = ((128, 1), (512, 4), (2048, 16))
N_GROUPS = len(ATTN_GROUPS)
HEADS = 8
HEAD_DIM = 64
ROT_DIM = HEAD_DIM // 4
ROPE_THETA = 500000.0
ATTN_BLOCK = 128
PAST_LEN = 2048

LANES = 128
SUBLANES = 8
VMEM_LIMIT = 56 * 1024 * 1024


def _cparams(n_grid):
    return pltpu.CompilerParams(dimension_semantics=("arbitrary",) * n_grid,
                                vmem_limit_bytes=VMEM_LIMIT)


def _const_spec(shape):
    nd = len(shape)
    return pl.BlockSpec(shape, lambda *_: (0,) * nd, pipeline_mode=pl.Buffered(1))


def _rms(x, g):
    ms = jnp.mean(x * x, axis=-1, keepdims=True)
    return x * lax.rsqrt(ms + NORM_EPS) * g


def _silu(a):
    return a / (1.0 + jnp.exp(-a))


def _ffn_core(x1, gf_ref, wup_ref, cw_ref, cb_ref, wdown_ref, hbuf, act, fc, prev_rows):
    d_ff = wdown_ref.shape[0]
    hbuf[...] = _rms(x1, gf_ref[...]).astype(BF16)
    for c in range(d_ff // fc):
        halves = []
        for base in (0, d_ff):
            cols = slice(base + c * fc, base + (c + 1) * fc)
            u = jnp.dot(hbuf[...], wup_ref[:, cols], preferred_element_type=F32)
            u1, u2 = prev_rows(cols, u)
            conv = cb_ref[:, cols] + ((cw_ref[0:1, cols] * u2 + cw_ref[1:2, cols] * u1)
                                      + cw_ref[2:3, cols] * u)
            halves.append(conv)
        a, b = halves
        act[:, c * fc:(c + 1) * fc] = (_silu(a) * b).astype(BF16)
    return x1 + jnp.dot(act[...], wdown_ref[...], preferred_element_type=F32)


def _prompt_prev_rows(ubuf, ucarry, tm):
    def prev_rows(cols, u):
        ubuf[0:SUBLANES, :] = ucarry[:, cols]
        ubuf[SUBLANES:SUBLANES + tm, :] = u
        ucarry[:, cols] = ubuf[tm:tm + SUBLANES, :]
        return (ubuf[SUBLANES - 1:SUBLANES - 1 + tm, :], ubuf[SUBLANES - 2:SUBLANES - 2 + tm, :])
    return prev_rows


def _sample_prev_rows(hist_ref, newhist_ref):
    def prev_rows(cols, u):
        newhist_ref[0, :, cols] = hist_ref[1, :, cols]
        newhist_ref[1, :, cols] = u
        return hist_ref[1, :, cols], hist_ref[0, :, cols]
    return prev_rows


def _pool_mix(xn, shifted, cnt_of, wp_ref, ps_ref):
    gw = xn.shape[1] // len(POOL_WINDOWS)
    parts = []
    for g, w in enumerate(POOL_WINDOWS):
        sl = slice(g * gw, (g + 1) * gw)
        cur = xn[:, sl]
        s = cur
        for k in range(1, w):
            s = s + shifted(k, sl)
        d = s / cnt_of(w, cur.shape) - cur
        parts.append(jnp.dot(d.astype(BF16), wp_ref[g], preferred_element_type=F32))
    return jnp.concatenate(parts, axis=-1) * ps_ref[...]


def _prompt_layer_kernel(mode, tm, fc, *refs):
    if mode == "pool":
        (x_ref, gp_ref, wp_ref, ps_ref, gf_ref, wup_ref, cw_ref, cb_ref, wdown_ref,
         xo_ref, ptail_ref, ctail_ref, xnbuf, hbuf, act, ubuf, ucarry) = refs
    else:
        (x_ref, o_ref, wo_ref, gf_ref, wup_ref, cw_ref, cb_ref, wdown_ref,
         xo_ref, ctail_ref, hbuf, act, ubuf, ucarry) = refs
    t = pl.program_id(1)
    hist = 2 * SUBLANES

    @pl.when(t == 0)
    def _():
        ucarry[...] = jnp.zeros_like(ucarry)
        if mode == "pool":
            xnbuf[0:hist, :] = jnp.zeros((hist, xnbuf.shape[1]), F32)

    x = x_ref[...]
    if mode == "pool":
        xn = _rms(x, gp_ref[...])
        xnbuf[hist:hist + tm, :] = xn

        def shifted(k, sl):
            return xnbuf[hist - k:hist - k + tm, sl]

        def cnt_of(w, shape):
            pos = t * tm + lax.broadcasted_iota(jnp.int32, shape, 0)
            return jnp.minimum(pos + 1, w).astype(F32)

        x1 = x + _pool_mix(xn, shifted, cnt_of, wp_ref, ps_ref)
        tail = xnbuf[tm:tm + hist, :]
        ptail_ref[...] = tail
        xnbuf[0:hist, :] = tail
    else:
        x1 = x + jnp.dot(o_ref[...].astype(BF16), wo_ref[...], preferred_element_type=F32)

    xo_ref[...] = _ffn_core(x1, gf_ref, wup_ref, cw_ref, cb_ref, wdown_ref, hbuf, act, fc,
                            _prompt_prev_rows(ubuf, ucarry, tm))
    ctail_ref[...] = ucarry[...]


def _sample_layer_kernel(mode, fc, *refs):
    if mode == "pool":
        (x_ref, st_ref, gp_ref, wp_ref, ps_ref, gf_ref, wup_ref, cw_ref, cb_ref, wdown_ref,
         hist_ref, xo_ref, nst_ref, nhist_ref, hbuf, act) = refs
    else:
        (x_ref, o_ref, wo_ref, gf_ref, wup_ref, cw_ref, cb_ref, wdown_ref,
         hist_ref, xo_ref, nhist_ref, hbuf, act) = refs
    x = x_ref[...]
    if mode == "pool":
        xn = _rms(x, gp_ref[...])
        n_hist = st_ref.shape[0]

        def shifted(k, sl):
            return st_ref[n_hist - k, :, sl]

        def cnt_of(w, shape):
            return float(min(PAST_LEN + 1, w))

        x1 = x + _pool_mix(xn, shifted, cnt_of, wp_ref, ps_ref)
        for i in range(n_hist - 1):
            nst_ref[i] = st_ref[i + 1]
        nst_ref[n_hist - 1] = xn
    else:
        x1 = x + jnp.dot(o_ref[...].astype(BF16), wo_ref[...], preferred_element_type=F32)
    xo_ref[...] = _ffn_core(x1, gf_ref, wup_ref, cw_ref, cb_ref, wdown_ref, hbuf, act, fc,
                            _sample_prev_rows(hist_ref, nhist_ref))


def _ffn_weight_specs(d, f2, f):
    return [_const_spec((1, d)), _const_spec((d, f2)), _const_spec((CONV_W, f2)),
            _const_spec((1, f2)), _const_spec((f, d))]


def _pick_fc(f):
    for fc in (256, 128):
        if f % fc == 0:
            return fc
    return f


def _prompt_layer(mode, x, pre, ffn_w, tm):
    b, s, d = x.shape
    gf, wup, cw, cb, wdown = ffn_w
    f2, f = wup.shape[1], wdown.shape[0]
    fc = _pick_fc(f)
    row = pl.BlockSpec((None, tm, d), lambda i, j: (i, j, 0))
    hist = 2 * SUBLANES
    ctail_spec = pl.BlockSpec((None, SUBLANES, f2), lambda i, j: (i, 0, 0))
    ctail_shape = jax.ShapeDtypeStruct((b, SUBLANES, f2), F32)
    common_scratch = [pltpu.VMEM((tm, d), BF16), pltpu.VMEM((tm, f), BF16),
                      pltpu.VMEM((tm + SUBLANES, fc), F32), pltpu.VMEM((SUBLANES, f2), F32)]
    if mode == "pool":
        gp, wp, ps = pre
        in_specs = [row, _const_spec((1, d)), _const_spec(wp.shape), _const_spec((1, d))]
        args = [x, gp, wp, ps]
        out_specs = [row, pl.BlockSpec((None, hist, d), lambda i, j: (i, 0, 0)), ctail_spec]
        out_shape = [jax.ShapeDtypeStruct(x.shape, F32), jax.ShapeDtypeStruct((b, hist, d), F32),
                     ctail_shape]
        scratch = [pltpu.VMEM((tm + hist, d), F32)] + common_scratch
    else:
        o, wo = pre
        in_specs = [row, pl.BlockSpec((None, tm, o.shape[2]), lambda i, j: (i, j, 0)),
                    _const_spec(wo.shape)]
        args = [x, o, wo]
        out_specs = [row, ctail_spec]
        out_shape = [jax.ShapeDtypeStruct(x.shape, F32), ctail_shape]
        scratch = common_scratch
    return pl.pallas_call(
        functools.partial(_prompt_layer_kernel, mode, tm, fc),
        grid=(b, s // tm),
        in_specs=in_specs + _ffn_weight_specs(d, f2, f),
        out_specs=out_specs, out_shape=out_shape, scratch_shapes=scratch,
        compiler_params=_cparams(2), name=f"prompt_{mode}_layer",
    )(*args, gf, wup, cw, cb, wdown)


def _sample_layer(mode, x, pre, ffn_w, hist_t, bn):
    n, d = x.shape
    gf, wup, cw, cb, wdown = ffn_w
    f2, f = wup.shape[1], wdown.shape[0]
    fc = _pick_fc(f)
    row = lambda width: pl.BlockSpec((bn, width), lambda i: (i, 0))
    tmaj = lambda a: pl.BlockSpec((a.shape[0], bn, a.shape[2]), lambda i: (0, i, 0))
    if mode == "pool":
        st_t, gp, wp, ps = pre
        in_specs = [row(d), tmaj(st_t), _const_spec((1, d)), _const_spec(wp.shape),
                    _const_spec((1, d))]
        args = [x, st_t, gp, wp, ps]
        out_specs = [row(d), tmaj(st_t), tmaj(hist_t)]
        out_shape = [jax.ShapeDtypeStruct(x.shape, F32), jax.ShapeDtypeStruct(st_t.shape, F32),
                     jax.ShapeDtypeStruct(hist_t.shape, F32)]
    else:
        o, wo = pre
        in_specs = [row(d), row(o.shape[1]), _const_spec(wo.shape)]
        args = [x, o, wo]
        out_specs = [row(d), tmaj(hist_t)]
        out_shape = [jax.ShapeDtypeStruct(x.shape, F32), jax.ShapeDtypeStruct(hist_t.shape, F32)]
    return pl.pallas_call(
        functools.partial(_sample_layer_kernel, mode, fc),
        grid=(n // bn,),
        in_specs=in_specs + _ffn_weight_specs(d, f2, f) + [tmaj(hist_t)],
        out_specs=out_specs, out_shape=out_shape,
        scratch_shapes=[pltpu.VMEM((bn, d), BF16), pltpu.VMEM((bn, f), BF16)],
        compiler_params=_cparams(1), name=f"sample_{mode}_layer",
    )(*args, gf, wup, cw, cb, wdown, hist_t)


def _head_norm_rot(blk, gain, cosb, sina, sinb):
    lo = lax.broadcasted_iota(jnp.int32, blk.shape, 1) < HEAD_DIM
    sq = blk * blk
    s_lo = jnp.sum(jnp.where(lo, sq, 0.0), axis=-1, keepdims=True)
    s_hi = jnp.sum(jnp.where(lo, 0.0, sq), axis=-1, keepdims=True)
    ms = jnp.where(lo, s_lo, s_hi) * (1.0 / HEAD_DIM)
    y = blk * lax.rsqrt(ms + NORM_EPS) * gain
    return (y * cosb + pltpu.roll(y, LANES - ROT_DIM // 2, 1) * sina
            + pltpu.roll(y, ROT_DIM // 2, 1) * sinb)


def _proj_kernel(do_kv, *refs):
    if do_kv:
        (x_ref, cos_ref, sina_ref, sinb_ref, gkv_ref, wkv_ref, kg_ref, gq_ref, wq_ref, qg_ref,
         kv0_ref, kv1_ref, kv2_ref, q_ref) = refs
    else:
        (x_ref, cos_ref, sina_ref, sinb_ref, gq_ref, wq_ref, qg_ref, q_ref) = refs
    x = x_ref[...]
    cosb, sina, sinb = cos_ref[...], sina_ref[...], sinb_ref[...]
    gw = HEADS * HEAD_DIM
    if do_kv:
        h = _rms(x, gkv_ref[...]).astype(BF16)
        kv = jnp.dot(h, wkv_ref[...], preferred_element_type=F32)
        for g, out_ref in enumerate((kv0_ref, kv1_ref, kv2_ref)):
            for j in range(gw // LANES):
                c0 = g * gw + j * LANES
                out_ref[:, j * LANES:(j + 1) * LANES] = _head_norm_rot(
                    kv[:, c0:c0 + LANES], kg_ref[...], cosb, sina, sinb)
            v0 = N_GROUPS * gw + g * gw
            out_ref[:, gw:2 * gw] = kv[:, v0:v0 + gw]
    h = _rms(x, gq_ref[...]).astype(BF16)
    q = jnp.dot(h, wq_ref[...], preferred_element_type=F32)
    for j in range(q.shape[1] // LANES):
        q_ref[:, j * LANES:(j + 1) * LANES] = _head_norm_rot(
            q[:, j * LANES:(j + 1) * LANES], qg_ref[...], cosb, sina, sinb) * (HEAD_DIM ** -0.5)


def _proj(x2d, tabs, n_tab_blocks, kv_w, q_w, tm):
    r, d = x2d.shape
    gq, wq, qg = q_w
    row = pl.BlockSpec((tm, d), lambda i: (i, 0))
    tab = pl.BlockSpec((tm, LANES), lambda i: (i % n_tab_blocks, 0))
    gw = HEADS * HEAD_DIM
    in_specs = [row, tab, tab, tab]
    args = [x2d, *tabs]
    out_specs, out_shape = [], []
    if kv_w is not None:
        gkv, wkv, kg = kv_w
        in_specs += [_const_spec((1, d)), _const_spec(wkv.shape), _const_spec((1, LANES))]
        args += [gkv, wkv, kg]
        out_specs += [pl.BlockSpec((tm, 2 * gw), lambda i: (i, 0))] * N_GROUPS
        out_shape += [jax.ShapeDtypeStruct((r, 2 * gw), F32)] * N_GROUPS
    in_specs += [_const_spec((1, d)), _const_spec(wq.shape), _const_spec((1, LANES))]
    args += [gq, wq, qg]
    out_specs.append(pl.BlockSpec((tm, wq.shape[1]), lambda i: (i, 0)))
    out_shape.append(jax.ShapeDtypeStruct((r, wq.shape[1]), F32))
    return pl.pallas_call(
        functools.partial(_proj_kernel, kv_w is not None),
        grid=(r // tm,), in_specs=in_specs, out_specs=out_specs, out_shape=out_shape,
        compiler_params=_cparams(1), name="proj_kvq" if kv_w is not None else "proj_q",
    )(*args)


def _prompt_attn_kernel(seq, *refs):
    q_refs, k_refs, v_refs = refs[0:3], refs[3:6], refs[6:9]
    o_ref = refs[9]
    qb, kb, vb, tmpf, accs, ms, ls = refs[10:17]
    blk = ATTN_BLOCK
    lo = lax.broadcasted_iota(jnp.int32, (blk, LANES), 1) < HEAD_DIM
    quarter = seq // 4

    def valid_mask(nk):
        qi = lax.broadcasted_iota(jnp.int32, (2 * blk, nk), 0) & (blk - 1)
        kj = lax.broadcasted_iota(jnp.int32, (2 * blk, nk), 1)
        if nk == blk:
            return kj <= qi
        return lax.bitcast_convert_type(kj - qi, jnp.uint32) <= jnp.uint32(blk)

    for g, (win, dil) in enumerate(ATTN_GROUPS):
        assert win // dil == blk and dil in (1, 4, 16) and seq % (blk * dil) == 0
        sub = seq // dil
        nb = sub // blk

        def stage(src, dil=dil):
            if dil == 16:
                for r1 in range(4):
                    tmpf[r1 * quarter:(r1 + 1) * quarter, :] = src[pl.ds(r1, quarter, stride=4), :]

        def rows_of(src, r, n0, cnt, dil=dil):
            if dil == 1:
                return src[pl.ds(n0, cnt), :]
            if dil == 4:
                return src[pl.ds(r + 4 * n0, cnt, stride=4), :]
            return tmpf[pl.ds((r % 4) * quarter + r // 4 + 4 * n0, cnt, stride=4), :]

        stage(q_refs[g])
        for r in range(dil):
            for n in range(nb):
                qf = rows_of(q_refs[g], r, n * blk, blk)
                base = (r * nb + n) * 2 * blk
                qb[base:base + blk, :] = jnp.where(lo, qf, 0.0).astype(BF16)
                qb[base + blk:base + 2 * blk, :] = jnp.where(lo, 0.0, qf).astype(BF16)
        for src, dst in ((k_refs[g], kb), (v_refs[g], vb)):
            stage(src)
            for r in range(dil):
                dst[r * sub:(r + 1) * sub, :] = rows_of(src, r, 0, sub).astype(BF16)

        def block(r, n, first, g=g, dil=dil, sub=sub, nb=nb):
            q2 = qb[pl.ds(pl.multiple_of((r * nb + n) * 2 * blk, 2 * blk), 2 * blk), :]
            nk = blk if first else 2 * blk
            k0 = pl.multiple_of(r * sub + (n if first else n - 1) * blk, blk)
            kt = kb[pl.ds(k0, nk), :]
            vt = vb[pl.ds(k0, nk), :]
            s = lax.dot_general(q2, kt, (((1,), (1,)), ((), ())), preferred_element_type=F32)
            s = jnp.where(valid_mask(nk), s, NEG_INF)
            m = jnp.max(s, axis=-1, keepdims=True)
            p = jnp.exp(s - m)
            l = jnp.sum(p, axis=-1, keepdims=True)
            pv = jnp.dot(p.astype(BF16), vt, preferred_element_type=F32)
            if dil > 1:
                out_rows = pl.ds(r + n * blk * dil, blk, stride=dil)
            else:
                out_rows = pl.ds(pl.multiple_of(n * blk, blk), blk)
            accs[g, out_rows, :] = jnp.where(lo, pv[:blk], pv[blk:])
            ms[g, out_rows, :] = jnp.where(lo, m[:blk], m[blk:])
            ls[g, out_rows, :] = jnp.where(lo, l[:blk], l[blk:])

        def residue(r, carry, nb=nb, block=block):
            block(r, 0, True)
            for n in range(1, min(nb, 4)):
                block(r, n, False)
            return carry

        def later(n, carry, block=block):
            block(0, n, False)
            return carry

        if dil == 1:
            residue(0, 0)
            if nb > 4:
                lax.fori_loop(4, nb, later, 0, unroll=3)
        else:
            assert nb <= 4
            lax.fori_loop(0, dil, residue, 0, unroll=4 if nb == 1 else 1)

    def merge(i, carry):
        rows = pl.ds(pl.multiple_of(i * blk, blk), blk)
        m = jnp.maximum(jnp.maximum(ms[0, rows, :], ms[1, rows, :]), ms[2, rows, :])
        num = jnp.zeros((blk, LANES), F32)
        den = jnp.zeros((blk, LANES), F32)
        for g in range(N_GROUPS):
            w = jnp.exp(ms[g, rows, :] - m)
            num = num + w * accs[g, rows, :]
            den = den + w * ls[g, rows, :]
        o_ref[rows, :] = (num / den).astype(o_ref.dtype)
        return carry

    lax.fori_loop(0, seq // blk, merge, 0)


def _prompt_attn(q, kvs):
    b, s, _ = q.shape
    gw = HEADS * HEAD_DIM
    n_hp = gw // LANES
    blk_spec = lambda col: pl.BlockSpec((None, s, LANES), lambda i, j, col=col: (i, 0, col + j))
    in_specs = ([blk_spec(g * n_hp) for g in range(N_GROUPS)]
                + [blk_spec(0)] * N_GROUPS
                + [blk_spec(n_hp)] * N_GROUPS)
    args = [q] * N_GROUPS + list(kvs) + list(kvs)
    return pl.pallas_call(
        functools.partial(_prompt_attn_kernel, s),
        grid=(b, n_hp),
        in_specs=in_specs,
        out_specs=pl.BlockSpec((None, s, LANES), lambda i, j: (i, 0, j)),
        out_shape=jax.ShapeDtypeStruct((b, s, gw), BF16),
        scratch_shapes=[pltpu.VMEM((2 * s, LANES), BF16),
                        pltpu.VMEM((s, LANES), BF16),
                        pltpu.VMEM((s, LANES), BF16),
                        pltpu.VMEM((s, LANES), F32),
                        pltpu.VMEM((N_GROUPS, s, LANES), F32),
                        pltpu.VMEM((N_GROUPS, s, LANES), F32),
                        pltpu.VMEM((N_GROUPS, s, LANES), F32)],
        compiler_params=_cparams(2), name="prompt_attn",
    )(*args)


def _compact_kernel(bb, c0_ref, c1_ref, c2_ref, sel1_ref, sel2_ref, o_ref):
    for b in range(bb):
        for two in range(2):
            o_ref[b, 0, two] = c0_ref[b, two].astype(BF16)
            for g, (c_ref, sel_ref) in enumerate(((c1_ref, sel1_ref), (c2_ref, sel2_ref)), start=1):
                o_ref[b, g, two] = jnp.dot(c_ref[b, two].astype(BF16), sel_ref[...],
                                           preferred_element_type=F32).astype(BF16)


def _compact_caches(caches, bb):
    n = caches[0].shape[0]
    gw = HEADS * HEAD_DIM
    args, in_specs, sels = [], [], []
    for c, (win, dil) in zip(caches, ATTN_GROUPS):
        n_buf, n_back = c.shape[1], win // dil
        assert n_buf == n_back * dil, "window buffer must hold exactly n_back dilated rows"
        args.append(jnp.transpose(c, (0, 2, 3, 4, 1)).reshape(n, 2, gw, n_buf))
        in_specs.append(pl.BlockSpec((bb, 2, gw, n_buf), lambda i: (i, 0, 0, 0)))
        if dil > 1:
            t = lax.broadcasted_iota(jnp.int32, (n_buf, n_back), 0)
            m = lax.broadcasted_iota(jnp.int32, (n_buf, n_back), 1)
            sels.append((t == dil * m).astype(BF16))
    n_back = ATTN_GROUPS[0][0] // ATTN_GROUPS[0][1]
    return pl.pallas_call(
        functools.partial(_compact_kernel, bb),
        grid=(n // bb,),
        in_specs=in_specs + [_const_spec(s_.shape) for s_ in sels],
        out_specs=pl.BlockSpec((bb, N_GROUPS, 2, gw, n_back), lambda i: (i, 0, 0, 0, 0)),
        out_shape=jax.ShapeDtypeStruct((n, N_GROUPS, 2, gw, n_back), BF16),
        compiler_params=_cparams(1), name="compact_caches",
    )(*args, *sels)


def _sample_attn_kernel(bb, q_ref, kv0_ref, kv1_ref, kv2_ref, kc_ref, o_ref):
    gw = HEADS * HEAD_DIM
    head_of_row = lax.broadcasted_iota(jnp.int32, (HEADS, gw), 0)
    head_of_col = lax.broadcasted_iota(jnp.int32, (HEADS, gw), 1) // HEAD_DIM
    diag = head_of_row == head_of_col

    def body(b, carry):
        row = pl.ds(b, 1)
        m_run = l_run = acc_run = None
        for g, kvn_ref in enumerate((kv0_ref, kv1_ref, kv2_ref)):
            q_bd = jnp.where(diag, q_ref[row, g * gw:(g + 1) * gw], 0.0)
            k_new = kvn_ref[row, 0:gw]
            v_new = kvn_ref[row, gw:2 * gw]
            s = jnp.dot(q_bd.astype(BF16), kc_ref[b, g, 0], preferred_element_type=F32)
            s_new = jnp.sum(q_bd * k_new, axis=-1, keepdims=True)
            m = jnp.maximum(jnp.max(s, axis=-1, keepdims=True), s_new)
            p = jnp.exp(s - m)
            p_new = jnp.exp(s_new - m)
            l = jnp.sum(p, axis=-1, keepdims=True) + p_new
            acc = lax.dot_general(p.astype(BF16), kc_ref[b, g, 1], (((1,), (1,)), ((), ())),
                                  preferred_element_type=F32) + p_new * v_new
            if g == 0:
                m_run, l_run, acc_run = m, l, acc
            else:
                m_new = jnp.maximum(m_run, m)
                a_old = jnp.exp(m_run - m_new)
                a_cur = jnp.exp(m - m_new)
                l_run = a_old * l_run + a_cur * l
                acc_run = a_old * acc_run + a_cur * acc
                m_run = m_new
        o_ref[row, :] = jnp.sum(jnp.where(diag, acc_run / l_run, 0.0), axis=0, keepdims=True)
        return carry

    lax.fori_loop(0, bb, body, 0, unroll=2)


def _sample_attn(q, kvn, kc, bb):
    n = q.shape[0]
    gw = HEADS * HEAD_DIM
    rows = lambda width: pl.BlockSpec((bb, width), lambda i: (i, 0))
    return pl.pallas_call(
        functools.partial(_sample_attn_kernel, bb),
        grid=(n // bb,),
        in_specs=[rows(q.shape[1])] + [rows(2 * gw)] * N_GROUPS
                 + [pl.BlockSpec((bb,) + kc.shape[1:], lambda i: (i, 0, 0, 0, 0))],
        out_specs=rows(gw),
        out_shape=jax.ShapeDtypeStruct((n, gw), F32),
        compiler_params=_cparams(1), name="sample_attn",
    )(q, *kvn, kc)


def _rot_tables(pos):
    half = ROT_DIM // 2
    inv_freq = ROPE_THETA ** (-jnp.arange(0, ROT_DIM, 2, dtype=F32) / ROT_DIM)
    ang = pos.astype(F32)[:, None] * inv_freq[None, :]
    cos, sin = jnp.cos(ang), jnp.sin(ang)
    t = pos.shape[0]
    pad = jnp.zeros((t, HEAD_DIM - ROT_DIM), F32)
    zer = jnp.zeros((t, half), F32)
    cosb = jnp.concatenate([cos, cos, pad + 1.0], axis=1)
    sina = jnp.concatenate([-sin, zer, pad], axis=1)
    sinb = jnp.concatenate([zer, sin, pad], axis=1)
    return tuple(jnp.tile(a, (1, LANES // HEAD_DIM)) for a in (cosb, sina, sinb))


def kernel(x_prompt, x_sample, state_pool, state_conv, cache_kv_w128, cache_kv_w512, cache_kv_w2048,
           g_pool, w_pool, pool_scale, g_ffn, w_up, conv_w, conv_b, w_down, g_kv, w_kv, k_gain,
           g_attn, w_q, q_gain, w_o):
    b, s, d = x_prompt.shape
    n = x_sample.shape[0]
    depth = w_up.shape[0]
    n_pool = g_pool.shape[0]
    gw = HEADS * HEAD_DIM
    row = lambda a: a.reshape(1, -1)
    lane2 = lambda a: jnp.tile(a, LANES // HEAD_DIM).reshape(1, LANES)
    tm = min(512, s)
    bn = min(64, n)
    bb = min(8, n)

    wup_b, wdown_b = w_up.astype(BF16), w_down.astype(BF16)
    wpool_b, wkv_b, wq_b, wo_b = (w.astype(BF16) for w in (w_pool, w_kv, w_q, w_o))
    ffn_w = [(row(g_ffn[l]), wup_b[l], conv_w[l], row(conv_b[l]), wdown_b[l]) for l in range(depth)]
    kv_w = (row(g_kv), wkv_b, lane2(k_gain))
    q_w = [(row(g_attn[j]), wq_b[j], lane2(q_gain[j])) for j in range(depth - n_pool)]

    tabs_p = _rot_tables(jnp.arange(s, dtype=jnp.int32))
    tabs_s = _rot_tables(jnp.full((n,), PAST_LEN, jnp.int32))
    kc = _compact_caches((cache_kv_w128, cache_kv_w512, cache_kv_w2048), 1)

    xp, xs = x_prompt, x_sample.reshape(n, d)
    pool_p, pool_s, conv_p, conv_s = [], [], [], []
    hist_t = jnp.swapaxes(state_conv, 1, 2)
    st_t = jnp.swapaxes(state_pool, 1, 2)
    kvs_p = kvs_s = None
    for l in range(depth):
        if l < n_pool:
            pre = (row(g_pool[l]), wpool_b[l], row(pool_scale[l]))
            xp, ptail, ctail = _prompt_layer("pool", xp, pre, ffn_w[l], tm)
            pool_p.append(ptail[:, 2 * SUBLANES - POOL_HIST:])
            xs, nst, nhist = _sample_layer("pool", xs, (st_t[l],) + pre, ffn_w[l], hist_t[l], bn)
            pool_s.append(jnp.swapaxes(nst, 0, 1))
        else:
            j = l - n_pool
            if j == 0:
                *kvs_p, q_p = _proj(xp.reshape(b * s, d), tabs_p, s // tm, kv_w, q_w[j], tm)
                *kvs_s, q_s = _proj(xs, tabs_s, 1, kv_w, q_w[j], n)
                kvs_p = [a.reshape(b, s, 2 * gw) for a in kvs_p]
            else:
                (q_p,) = _proj(xp.reshape(b * s, d), tabs_p, s // tm, None, q_w[j], tm)
                (q_s,) = _proj(xs, tabs_s, 1, None, q_w[j], n)
            o_p = _prompt_attn(q_p.reshape(b, s, N_GROUPS * gw), kvs_p)
            o_s = _sample_attn(q_s, kvs_s, kc, bb)
            xp, ctail = _prompt_layer("attn", xp, (o_p, wo_b[j]), ffn_w[l], tm)
            xs, nhist = _sample_layer("attn", xs, (o_s, wo_b[j]), ffn_w[l], hist_t[l], bn)
        conv_p.append(ctail[:, SUBLANES - (CONV_W - 1):])
        conv_s.append(jnp.swapaxes(nhist, 0, 1))

    outs = [xp, xs.reshape(n, 1, d), jnp.stack(pool_p), jnp.stack(pool_s),
            jnp.stack(conv_p), jnp.stack(conv_s)]
    for g, (win, _) in enumerate(ATTN_GROUPS):
        rows_p = kvs_p[g][:, s - min(win, s):].reshape(b, min(win, s), 2, HEADS, HEAD_DIM)
        outs += [rows_p, kvs_s[g].reshape(n, 1, 2, HEADS, HEAD_DIM)]
    return tuple(outs)
```

```python
import functools

import jax
import jax.numpy as jnp
from jax import lax
from jax.experimental import pallas as pl
from jax.experimental.pallas import tpu as pltpu

F32 = jnp.float32
BF16 = jnp.bfloat16

NORM_EPS = 1e-6
NEG_INF = -1e30
POOL_WINDOWS = (2, 4, 8, 16)
POOL_HIST = max(POOL_WINDOWS) - 1
CONV_W = 3
ATTN_GROUPS = ((128, 1), (512, 4), (2048, 16))
N_GROUPS = len(ATTN_GROUPS)
HEADS = 8
HEAD_DIM = 64
ROT_DIM = HEAD_DIM // 4
ROPE_THETA = 500000.0
ATTN_BLOCK = 128
PAST_LEN = 2048

LANES = 128
SUBLANES = 8
VMEM_LIMIT = 56 * 1024 * 1024


def _cparams(n_grid):
    return pltpu.CompilerParams(dimension_semantics=("arbitrary",) * n_grid,
                                vmem_limit_bytes=VMEM_LIMIT)


def _const_spec(shape):
    nd = len(shape)
    return pl.BlockSpec(shape, lambda *_: (0,) * nd, pipeline_mode=pl.Buffered(1))


def _rms(x, g):
    ms = jnp.mean(x * x, axis=-1, keepdims=True)
    return x * lax.rsqrt(ms + NORM_EPS) * g


def _silu(a):
    return a / (1.0 + jnp.exp(-a))


def _ffn_core(x1, gf_ref, wup_ref, cw_ref, cb_ref, wdown_ref, hbuf, act, fc, rows):
    n_blocks, to_rows, prev_rows, from_rows = rows
    d_ff = wdown_ref.shape[0]
    q = hbuf.shape[0] // n_blocks
    blocks = [slice(j * q, (j + 1) * q) for j in range(n_blocks)]
    h = _rms(x1, gf_ref[...])
    for j, rb in enumerate(blocks):
        hbuf[rb, :] = to_rows(h, j).astype(BF16)
    for c in range(d_ff // fc):
        halves = []
        for base in (0, d_ff):
            cols = slice(base + c * fc, base + (c + 1) * fc)
            u = jnp.dot(hbuf[...], wup_ref[:, cols], preferred_element_type=F32)
            ub = [u[rb] for rb in blocks]
            prev = prev_rows(cols, ub)
            halves.append([cb_ref[:, cols] + ((cw_ref[0:1, cols] * u2 + cw_ref[1:2, cols] * u1)
                                              + cw_ref[2:3, cols] * u0)
                           for u0, (u1, u2) in zip(ub, prev)])
        for rb, a, b in zip(blocks, *halves):
            act[rb, c * fc:(c + 1) * fc] = (_silu(a) * b).astype(BF16)
    return x1 + from_rows(jnp.dot(act[...], wdown_ref[...], preferred_element_type=F32))


PROMPT_ROW_BLOCKS = 4


def _prompt_rows(slab, sbuf, ucarry, tm):
    nb = PROMPT_ROW_BLOCKS
    q = tm // nb
    n_slab = slab.shape[0]

    def to_rows(h, j):
        if j == 0:
            for k in range(n_slab):
                slab[k] = h[:, k * LANES:(k + 1) * LANES]
        return jnp.concatenate([slab[k, pl.ds(j, q, stride=nb), :] for k in range(n_slab)], axis=1)

    def shift_down(slot, blk, carry_row):
        sbuf[slot, SUBLANES - 1:SUBLANES, :] = carry_row
        sbuf[slot, SUBLANES:SUBLANES + q, :] = blk
        return sbuf[slot, SUBLANES - 1:SUBLANES - 1 + q, :]

    def prev_rows(cols, ub):
        s3 = shift_down(0, ub[3], ucarry[SUBLANES - 1:SUBLANES, cols])
        s2 = shift_down(1, ub[2], ucarry[SUBLANES - 2:SUBLANES - 1, cols])
        ucarry[SUBLANES - 1:SUBLANES, cols] = ub[3][q - 1:q]
        ucarry[SUBLANES - 2:SUBLANES - 1, cols] = ub[2][q - 1:q]
        return [(s3, s2), (ub[0], s3), (ub[1], ub[0]), (ub[2], ub[1])]

    def from_rows(y):
        for j in range(nb):
            for k in range(n_slab):
                slab[k, pl.ds(j, q, stride=nb), :] = y[j * q:(j + 1) * q, k * LANES:(k + 1) * LANES]
        return jnp.concatenate([slab[k] for k in range(n_slab)], axis=1)

    return nb, to_rows, prev_rows, from_rows


def _sample_rows(hist_ref, newhist_ref):
    def prev_rows(cols, ub):
        newhist_ref[0, :, cols] = hist_ref[1, :, cols]
        newhist_ref[1, :, cols] = ub[0]
        return [(hist_ref[1, :, cols], hist_ref[0, :, cols])]
    return 1, (lambda h, j: h), prev_rows, (lambda y: y)


def _pool_mix(xn, shifted, cnt_of, wp_ref, ps_ref):
    gw = xn.shape[1] // len(POOL_WINDOWS)
    parts = []
    for g, w in enumerate(POOL_WINDOWS):
        sl = slice(g * gw, (g + 1) * gw)
        cur = xn[:, sl]
        s = cur
        for k in range(1, w):
            s = s + shifted(k, sl)
        d = s / cnt_of(w, cur.shape) - cur
        parts.append(jnp.dot(d.astype(BF16), wp_ref[g], preferred_element_type=F32))
    return jnp.concatenate(parts, axis=-1) * ps_ref[...]


def _prompt_layer_kernel(mode, tm, fc, *refs):
    if mode == "pool":
        (x_ref, gp_ref, wp_ref, ps_ref, gf_ref, wup_ref, cw_ref, cb_ref, wdown_ref,
         xo_ref, ptail_ref, ctail_ref, xnbuf, hbuf, act, slab, sbuf, ucarry) = refs
    else:
        (x_ref, o_ref, wo_ref, gf_ref, wup_ref, cw_ref, cb_ref, wdown_ref,
         xo_ref, ctail_ref, hbuf, act, slab, sbuf, ucarry) = refs
    t = pl.program_id(1)
    hist = 2 * SUBLANES

    @pl.when(t == 0)
    def _():
        ucarry[...] = jnp.zeros_like(ucarry)
        if mode == "pool":
            xnbuf[0:hist, :] = jnp.zeros((hist, xnbuf.shape[1]), F32)

    x = x_ref[...]
    if mode == "pool":
        xn = _rms(x, gp_ref[...])
        xnbuf[hist:hist + tm, :] = xn

        def shifted(k, sl):
            return xnbuf[hist - k:hist - k + tm, sl]

        def cnt_of(w, shape):
            pos = t * tm + lax.broadcasted_iota(jnp.int32, shape, 0)
            return jnp.minimum(pos + 1, w).astype(F32)

        x1 = x + _pool_mix(xn, shifted, cnt_of, wp_ref, ps_ref)
        tail = xnbuf[tm:tm + hist, :]
        ptail_ref[...] = tail
        xnbuf[0:hist, :] = tail
    else:
        x1 = x + jnp.dot(o_ref[...].astype(BF16), wo_ref[...], preferred_element_type=F32)

    xo_ref[...] = _ffn_core(x1, gf_ref, wup_ref, cw_ref, cb_ref, wdown_ref, hbuf, act, fc,
                            _prompt_rows(slab, sbuf, ucarry, tm))
    ctail_ref[...] = ucarry[...]


def _sample_layer_kernel(mode, fc, *refs):
    if mode == "pool":
        (x_ref, st_ref, gp_ref, wp_ref, ps_ref, gf_ref, wup_ref, cw_ref, cb_ref, wdown_ref,
         hist_ref, xo_ref, nst_ref, nhist_ref, hbuf, act) = refs
    else:
        (x_ref, o_ref, wo_ref, gf_ref, wup_ref, cw_ref, cb_ref, wdown_ref,
         hist_ref, xo_ref, nhist_ref, hbuf, act) = refs
    x = x_ref[...]
    if mode == "pool":
        xn = _rms(x, gp_ref[...])
        n_hist = st_ref.shape[0]

        def shifted(k, sl):
            return st_ref[n_hist - k, :, sl]

        def cnt_of(w, shape):
            return float(min(PAST_LEN + 1, w))

        x1 = x + _pool_mix(xn, shifted, cnt_of, wp_ref, ps_ref)
        for i in range(n_hist - 1):
            nst_ref[i] = st_ref[i + 1]
        nst_ref[n_hist - 1] = xn
    else:
        x1 = x + jnp.dot(o_ref[...].astype(BF16), wo_ref[...], preferred_element_type=F32)
    xo_ref[...] = _ffn_core(x1, gf_ref, wup_ref, cw_ref, cb_ref, wdown_ref, hbuf, act, fc,
                            _sample_rows(hist_ref, nhist_ref))


def _ffn_weight_specs(d, f2, f):
    return [_const_spec((1, d)), _const_spec((d, f2)), _const_spec((CONV_W, f2)),
            _const_spec((1, f2)), _const_spec((f, d))]


def _pick_fc(f):
    for fc in (256, 128):
        if f % fc == 0:
            return fc
    return f


def _prompt_layer(mode, x, pre, ffn_w, tm):
    b, s, d = x.shape
    gf, wup, cw, cb, wdown = ffn_w
    f2, f = wup.shape[1], wdown.shape[0]
    fc = _pick_fc(f)
    row = pl.BlockSpec((None, tm, d), lambda i, j: (i, j, 0))
    hist = 2 * SUBLANES
    ctail_spec = pl.BlockSpec((None, SUBLANES, f2), lambda i, j: (i, 0, 0))
    ctail_shape = jax.ShapeDtypeStruct((b, SUBLANES, f2), F32)
    common_scratch = [pltpu.VMEM((tm, d), BF16), pltpu.VMEM((tm, f), BF16),
                      pltpu.VMEM((d // LANES, tm, LANES), F32),
                      pltpu.VMEM((2, tm // PROMPT_ROW_BLOCKS + SUBLANES, fc), F32),
                      pltpu.VMEM((SUBLANES, f2), F32)]
    if mode == "pool":
        gp, wp, ps = pre
        in_specs = [row, _const_spec((1, d)), _const_spec(wp.shape), _const_spec((1, d))]
        args = [x, gp, wp, ps]
        out_specs = [row, pl.BlockSpec((None, hist, d), lambda i, j: (i, 0, 0)), ctail_spec]
        out_shape = [jax.ShapeDtypeStruct(x.shape, F32), jax.ShapeDtypeStruct((b, hist, d), F32),
                     ctail_shape]
        scratch = [pltpu.VMEM((tm + hist, d), F32)] + common_scratch
    else:
        o, wo = pre
        in_specs = [row, pl.BlockSpec((None, tm, o.shape[2]), lambda i, j: (i, j, 0)),
                    _const_spec(wo.shape)]
        args = [x, o, wo]
        out_specs = [row, ctail_spec]
        out_shape = [jax.ShapeDtypeStruct(x.shape, F32), ctail_shape]
        scratch = common_scratch
    return pl.pallas_call(
        functools.partial(_prompt_layer_kernel, mode, tm, fc),
        grid=(b, s // tm),
        in_specs=in_specs + _ffn_weight_specs(d, f2, f),
        out_specs=out_specs, out_shape=out_shape, scratch_shapes=scratch,
        compiler_params=_cparams(2), name=f"prompt_{mode}_layer",
    )(*args, gf, wup, cw, cb, wdown)


def _sample_layer(mode, x, pre, ffn_w, hist_t, bn):
    n, d = x.shape
    gf, wup, cw, cb, wdown = ffn_w
    f2, f = wup.shape[1], wdown.shape[0]
    fc = _pick_fc(f)
    row = lambda width: pl.BlockSpec((bn, width), lambda i: (i, 0))
    tmaj = lambda a: pl.BlockSpec((a.shape[0], bn, a.shape[2]), lambda i: (0, i, 0))
    if mode == "pool":
        st_t, gp, wp, ps = pre
        in_specs = [row(d), tmaj(st_t), _const_spec((1, d)), _const_spec(wp.shape),
                    _const_spec((1, d))]
        args = [x, st_t, gp, wp, ps]
        out_specs = [row(d), tmaj(st_t), tmaj(hist_t)]
        out_shape = [jax.ShapeDtypeStruct(x.shape, F32), jax.ShapeDtypeStruct(st_t.shape, F32),
                     jax.ShapeDtypeStruct(hist_t.shape, F32)]
    else:
        o, wo = pre
        in_specs = [row(d), row(o.shape[1]), _const_spec(wo.shape)]
        args = [x, o, wo]
        out_specs = [row(d), tmaj(hist_t)]
        out_shape = [jax.ShapeDtypeStruct(x.shape, F32), jax.ShapeDtypeStruct(hist_t.shape, F32)]
    return pl.pallas_call(
        functools.partial(_sample_layer_kernel, mode, fc),
        grid=(n // bn,),
        in_specs=in_specs + _ffn_weight_specs(d, f2, f) + [tmaj(hist_t)],
        out_specs=out_specs, out_shape=out_shape,
        scratch_shapes=[pltpu.VMEM((bn, d), BF16), pltpu.VMEM((bn, f), BF16)],
        compiler_params=_cparams(1), name=f"sample_{mode}_layer",
    )(*args, gf, wup, cw, cb, wdown, hist_t)


HEAD_TILE = 2 * LANES


def _head_norm_rot(x, bd_ref, gain, tabs):
    gcos, sina, sinb = tabs
    ms = jnp.dot((x * x).astype(BF16), bd_ref[...], preferred_element_type=F32)
    outs = []
    for j in range(x.shape[1] // LANES):
        xs = x[:, j * LANES:(j + 1) * LANES]
        xg = xs * gain
        outs.append(xs * gcos + pltpu.roll(xg, LANES - ROT_DIM // 2, 1) * sina
                    + pltpu.roll(xg, ROT_DIM // 2, 1) * sinb)
    return jnp.concatenate(outs, axis=1) * lax.rsqrt(ms + NORM_EPS)


def _proj_kernel(do_kv, *refs):
    x_ref, bd_ref = refs[0:2]
    if do_kv:
        ktabs, (gkv_ref, wkv_ref, kg_ref) = refs[2:5], refs[5:8]
        qtabs, (gq_ref, wq_ref, qg_ref) = refs[8:11], refs[11:14]
        kv_refs, q_ref = refs[14:17], refs[17]
    else:
        qtabs, (gq_ref, wq_ref, qg_ref) = refs[2:5], refs[5:8]
        q_ref = refs[8]
    x = x_ref[...]
    gw = HEADS * HEAD_DIM
    if do_kv:
        h = _rms(x, gkv_ref[...]).astype(BF16)
        kv = jnp.dot(h, wkv_ref[...], preferred_element_type=F32)
        tabs = [t[...] for t in ktabs]
        for g, out_ref in enumerate(kv_refs):
            for j in range(gw // HEAD_TILE):
                c0 = g * gw + j * HEAD_TILE
                out_ref[:, j * HEAD_TILE:(j + 1) * HEAD_TILE] = _head_norm_rot(
                    kv[:, c0:c0 + HEAD_TILE], bd_ref, kg_ref[...], tabs)
            v0 = N_GROUPS * gw + g * gw
            out_ref[:, gw:2 * gw] = kv[:, v0:v0 + gw]
    h = _rms(x, gq_ref[...]).astype(BF16)
    q = jnp.dot(h, wq_ref[...], preferred_element_type=F32)
    tabs = [t[...] for t in qtabs]
    for j in range(q.shape[1] // HEAD_TILE):
        cols = slice(j * HEAD_TILE, (j + 1) * HEAD_TILE)
        q_ref[:, cols] = _head_norm_rot(q[:, cols], bd_ref, qg_ref[...], tabs)


def _proj(x2d, n_tab_blocks, kv_w, q_w, tm):
    r, d = x2d.shape
    row = pl.BlockSpec((tm, d), lambda i: (i, 0))
    tab = pl.BlockSpec((tm, LANES), lambda i: (i % n_tab_blocks, 0))
    gw = HEADS * HEAD_DIM
    head = lax.broadcasted_iota(jnp.int32, (HEAD_TILE, HEAD_TILE), 0) // HEAD_DIM
    bd = jnp.where(head == head.T, 1.0 / HEAD_DIM, 0.0).astype(BF16)
    in_specs, args = [row, _const_spec(bd.shape)], [x2d, bd]
    out_specs, out_shape = [], []
    for w in (kv_w, q_w):
        if w is None:
            continue
        tabs, g, wmat, hg = w
        in_specs += [tab, tab, tab, _const_spec((1, d)), _const_spec(wmat.shape), _const_spec((1, LANES))]
        args += [*tabs, g, wmat, hg]
    if kv_w is not None:
        out_specs += [pl.BlockSpec((tm, 2 * gw), lambda i: (i, 0))] * N_GROUPS
        out_shape += [jax.ShapeDtypeStruct((r, 2 * gw), F32)] * N_GROUPS
    n_q = q_w[2].shape[1]
    out_specs.append(pl.BlockSpec((tm, n_q), lambda i: (i, 0)))
    out_shape.append(jax.ShapeDtypeStruct((r, n_q), F32))
    return pl.pallas_call(
        functools.partial(_proj_kernel, kv_w is not None),
        grid=(r // tm,), in_specs=in_specs, out_specs=out_specs, out_shape=out_shape,
        compiler_params=_cparams(1), name="proj_kvq" if kv_w is not None else "proj_q",
    )(*args)


def _prompt_attn_kernel(seq, *refs):
    q_refs, k_refs, v_refs = refs[0:3], refs[3:6], refs[6:9]
    o_ref = refs[9]
    qb, kb, vb, tmpf, accs, ms, ls = refs[10:17]
    blk = ATTN_BLOCK
    lo = lax.broadcasted_iota(jnp.int32, (blk, LANES), 1) < HEAD_DIM
    quarter = seq // 4

    def valid_mask(nk):
        qi = lax.broadcasted_iota(jnp.int32, (2 * blk, nk), 0) & (blk - 1)
        kj = lax.broadcasted_iota(jnp.int32, (2 * blk, nk), 1)
        if nk == blk:
            return kj <= qi
        return lax.bitcast_convert_type(kj - qi, jnp.uint32) <= jnp.uint32(blk)

    for g, (win, dil) in enumerate(ATTN_GROUPS):
        assert win // dil == blk and dil in (1, 4, 16) and seq % (blk * dil) == 0
        sub = seq // dil
        nb = sub // blk

        def stage(src, dil=dil):
            if dil == 16:
                for r1 in range(4):
                    tmpf[r1 * quarter:(r1 + 1) * quarter, :] = src[pl.ds(r1, quarter, stride=4), :]

        def rows_of(src, r, n0, cnt, dil=dil):
            if dil == 1:
                return src[pl.ds(n0, cnt), :]
            if dil == 4:
                return src[pl.ds(r + 4 * n0, cnt, stride=4), :]
            return tmpf[pl.ds((r % 4) * quarter + r // 4 + 4 * n0, cnt, stride=4), :]

        stage(q_refs[g])
        for r in range(dil):
            for n in range(nb):
                qf = rows_of(q_refs[g], r, n * blk, blk)
                base = (r * nb + n) * 2 * blk
                qb[base:base + blk, :] = jnp.where(lo, qf, 0.0).astype(BF16)
                qb[base + blk:base + 2 * blk, :] = jnp.where(lo, 0.0, qf).astype(BF16)
        for src, dst in ((k_refs[g], kb), (v_refs[g], vb)):
            stage(src)
            for r in range(dil):
                dst[r * sub:(r + 1) * sub, :] = rows_of(src, r, 0, sub).astype(BF16)

        def block(r, n, first, g=g, dil=dil, sub=sub, nb=nb):
            q2 = qb[pl.ds((r * nb + n) * 2 * blk, 2 * blk), :]
            nk = blk if first else 2 * blk
            k0 = r * sub + (n if first else n - 1) * blk
            kt = kb[pl.ds(k0, nk), :]
            vt = vb[pl.ds(k0, nk), :]
            s = lax.dot_general(q2, kt, (((1,), (1,)), ((), ())), preferred_element_type=F32)
            s = jnp.where(valid_mask(nk), s, NEG_INF)
            m = jnp.max(s, axis=-1, keepdims=True)
            p = jnp.exp(s - m)
            l = jnp.sum(p, axis=-1, keepdims=True)
            pv = jnp.dot(p.astype(BF16), vt, preferred_element_type=F32)
            if dil > 1:
                out_rows = pl.ds(r + n * blk * dil, blk, stride=dil)
            else:
                out_rows = pl.ds(n * blk, blk)
            accs[g, out_rows, :] = jnp.where(lo, pv[:blk], pv[blk:])
            ms[g, out_rows, :] = jnp.where(lo, m[:blk], m[blk:])
            ls[g, out_rows, :] = jnp.where(lo, l[:blk], l[blk:])

        for r in range(dil):
            for n in range(nb):
                block(r, n, n == 0)

    def merge(i, carry):
        rows = pl.ds(pl.multiple_of(i * blk, blk), blk)
        m = jnp.maximum(jnp.maximum(ms[0, rows, :], ms[1, rows, :]), ms[2, rows, :])
        num = jnp.zeros((blk, LANES), F32)
        den = jnp.zeros((blk, LANES), F32)
        for g in range(N_GROUPS):
            w = jnp.exp(ms[g, rows, :] - m)
            num = num + w * accs[g, rows, :]
            den = den + w * ls[g, rows, :]
        o_ref[rows, :] = (num / den).astype(o_ref.dtype)
        return carry

    lax.fori_loop(0, seq // blk, merge, 0)


def _prompt_attn(q, kvs):
    b, s, _ = q.shape
    gw = HEADS * HEAD_DIM
    n_hp = gw // LANES
    blk_spec = lambda col: pl.BlockSpec((None, s, LANES), lambda i, j, col=col: (i, 0, col + j))
    in_specs = ([blk_spec(g * n_hp) for g in range(N_GROUPS)]
                + [blk_spec(0)] * N_GROUPS
                + [blk_spec(n_hp)] * N_GROUPS)
    args = [q] * N_GROUPS + list(kvs) + list(kvs)
    return pl.pallas_call(
        functools.partial(_prompt_attn_kernel, s),
        grid=(b, n_hp),
        in_specs=in_specs,
        out_specs=pl.BlockSpec((None, s, LANES), lambda i, j: (i, 0, j)),
        out_shape=jax.ShapeDtypeStruct((b, s, gw), BF16),
        scratch_shapes=[pltpu.VMEM((2 * s, LANES), BF16),
                        pltpu.VMEM((s, LANES), BF16),
                        pltpu.VMEM((s, LANES), BF16),
                        pltpu.VMEM((s, LANES), F32),
                        pltpu.VMEM((N_GROUPS, s, LANES), F32),
                        pltpu.VMEM((N_GROUPS, s, LANES), F32),
                        pltpu.VMEM((N_GROUPS, s, LANES), F32)],
        compiler_params=_cparams(2), name="prompt_attn",
    )(*args)


def _compact_kernel(bb, c0_ref, c1_ref, c2_ref, sel1_ref, sel2_ref, o_ref):
    nk = c0_ref.shape[-1]
    for b in range(bb):
        for two in range(2):
            o_ref[b, two, :, 0:nk] = c0_ref[b, two].astype(BF16)
            for g, (c_ref, sel_ref) in enumerate(((c1_ref, sel1_ref), (c2_ref, sel2_ref)), start=1):
                o_ref[b, two, :, g * nk:(g + 1) * nk] = jnp.dot(
                    c_ref[b, two].astype(BF16), sel_ref[...], preferred_element_type=F32).astype(BF16)


def _compact_caches(caches, bb):
    n = caches[0].shape[0]
    gw = HEADS * HEAD_DIM
    args, in_specs, sels = [], [], []
    for c, (win, dil) in zip(caches, ATTN_GROUPS):
        n_buf, n_back = c.shape[1], win // dil
        assert n_buf == n_back * dil, "window buffer must hold exactly n_back dilated rows"
        args.append(jnp.transpose(c, (0, 2, 3, 4, 1)).reshape(n, 2, gw, n_buf))
        in_specs.append(pl.BlockSpec((bb, 2, gw, n_buf), lambda i: (i, 0, 0, 0)))
        if dil > 1:
            t = lax.broadcasted_iota(jnp.int32, (n_buf, n_back), 0)
            m = lax.broadcasted_iota(jnp.int32, (n_buf, n_back), 1)
            sels.append((t == dil * m).astype(BF16))
    n_back = ATTN_GROUPS[0][0] // ATTN_GROUPS[0][1]
    return pl.pallas_call(
        functools.partial(_compact_kernel, bb),
        grid=(n // bb,),
        in_specs=in_specs + [_const_spec(s_.shape) for s_ in sels],
        out_specs=pl.BlockSpec((bb, 2, gw, N_GROUPS * n_back), lambda i: (i, 0, 0, 0)),
        out_shape=jax.ShapeDtypeStruct((n, 2, gw, N_GROUPS * n_back), BF16),
        compiler_params=_cparams(1), name="compact_caches",
    )(*args, *sels)


def _sample_attn_kernel(bb, q_ref, kv0_ref, kv1_ref, kv2_ref, kc_ref, o_ref):
    gw = HEADS * HEAD_DIM
    head_of_row = lax.broadcasted_iota(jnp.int32, (HEADS, gw), 0)
    head_of_col = lax.broadcasted_iota(jnp.int32, (HEADS, gw), 1) // HEAD_DIM
    diag = head_of_row == head_of_col

    nk = kc_ref.shape[-1] // N_GROUPS
    kvn_refs = (kv0_ref, kv1_ref, kv2_ref)

    def body(b, carry):
        row = pl.ds(b, 1)
        q_bd = [jnp.where(diag, q_ref[row, g * gw:(g + 1) * gw], 0.0) for g in range(N_GROUPS)]
        s_all = jnp.dot(jnp.concatenate(q_bd, axis=0).astype(BF16), kc_ref[b, 0],
                        preferred_element_type=F32)
        s = [s_all[g * HEADS:(g + 1) * HEADS, g * nk:(g + 1) * nk] for g in range(N_GROUPS)]
        s_new = [jnp.sum(q_bd[g] * kvn_refs[g][row, 0:gw], axis=-1, keepdims=True)
                 for g in range(N_GROUPS)]
        m = functools.reduce(jnp.maximum, [jnp.max(a, axis=-1, keepdims=True) for a in s] + s_new)
        p = [jnp.exp(a - m) for a in s]
        p_new = [jnp.exp(a - m) for a in s_new]
        l = sum(jnp.sum(a, axis=-1, keepdims=True) for a in p) + sum(p_new)
        acc = lax.dot_general(jnp.concatenate(p, axis=1).astype(BF16), kc_ref[b, 1],
                              (((1,), (1,)), ((), ())), preferred_element_type=F32)
        for g in range(N_GROUPS):
            acc = acc + p_new[g] * kvn_refs[g][row, gw:2 * gw]
        o_ref[row, :] = jnp.sum(jnp.where(diag, acc / l, 0.0), axis=0, keepdims=True)
        return carry

    lax.fori_loop(0, bb, body, 0, unroll=4)


def _sample_attn(q, kvn, kc, bb):
    n = q.shape[0]
    gw = HEADS * HEAD_DIM
    rows = lambda width: pl.BlockSpec((bb, width), lambda i: (i, 0))
    return pl.pallas_call(
        functools.partial(_sample_attn_kernel, bb),
        grid=(n // bb,),
        in_specs=[rows(q.shape[1])] + [rows(2 * gw)] * N_GROUPS
                 + [pl.BlockSpec((bb,) + kc.shape[1:], lambda i: (i, 0, 0, 0))],
        out_specs=rows(gw),
        out_shape=jax.ShapeDtypeStruct((n, gw), F32),
        compiler_params=_cparams(1), name="sample_attn",
    )(q, *kvn, kc)


def _rot_tables(pos):
    half = ROT_DIM // 2
    inv_freq = ROPE_THETA ** (-jnp.arange(0, ROT_DIM, 2, dtype=F32) / ROT_DIM)
    ang = pos.astype(F32)[:, None] * inv_freq[None, :]
    cos, sin = jnp.cos(ang), jnp.sin(ang)
    t = pos.shape[0]
    pad = jnp.zeros((t, HEAD_DIM - ROT_DIM), F32)
    zer = jnp.zeros((t, half), F32)
    cosb = jnp.concatenate([cos, cos, pad + 1.0], axis=1)
    sina = jnp.concatenate([-sin, zer, pad], axis=1)
    sinb = jnp.concatenate([zer, sin, pad], axis=1)
    return tuple(jnp.tile(a, (1, LANES // HEAD_DIM)) for a in (cosb, sina, sinb))


def kernel(x_prompt, x_sample, state_pool, state_conv, cache_kv_w128, cache_kv_w512, cache_kv_w2048,
           g_pool, w_pool, pool_scale, g_ffn, w_up, conv_w, conv_b, w_down, g_kv, w_kv, k_gain,
           g_attn, w_q, q_gain, w_o):
    b, s, d = x_prompt.shape
    n = x_sample.shape[0]
    depth = w_up.shape[0]
    n_pool = g_pool.shape[0]
    gw = HEADS * HEAD_DIM
    row = lambda a: a.reshape(1, -1)
    lane2 = lambda a: jnp.tile(a, LANES // HEAD_DIM).reshape(1, LANES)
    tm = min(512, s)
    bn = min(64, n)
    bb = min(8, n)

    wup_b, wdown_b = w_up.astype(BF16), w_down.astype(BF16)
    wpool_b, wkv_b, wq_b, wo_b = (w.astype(BF16) for w in (w_pool, w_kv, w_q, w_o))
    ffn_w = [(row(g_ffn[l]), wup_b[l], conv_w[l], row(conv_b[l]), wdown_b[l]) for l in range(depth)]
    q_scale = HEAD_DIM ** -0.5

    def proj_w(pos):
        cosb, sina, sinb = _rot_tables(pos)
        kg = lane2(k_gain)
        kv_w = ((cosb * kg, sina, sinb), row(g_kv), wkv_b, kg)
        q_w = []
        for j in range(depth - n_pool):
            qg = lane2(q_gain[j])
            q_w.append(((cosb * qg * q_scale, sina * q_scale, sinb * q_scale), row(g_attn[j]), wq_b[j], qg))
        return kv_w, q_w

    kv_w_p, q_w_p = proj_w(jnp.arange(s, dtype=jnp.int32))
    kv_w_s, q_w_s = proj_w(jnp.full((n,), PAST_LEN, jnp.int32))
    kc = _compact_caches((cache_kv_w128, cache_kv_w512, cache_kv_w2048), 1)

    xp, xs = x_prompt, x_sample.reshape(n, d)
    pool_p, pool_s, conv_p, conv_s = [], [], [], []
    hist_t = jnp.swapaxes(state_conv, 1, 2)
    st_t = jnp.swapaxes(state_pool, 1, 2)
    kvs_p = kvs_s = None
    for l in range(depth):
        if l < n_pool:
            pre = (row(g_pool[l]), wpool_b[l], row(pool_scale[l]))
            xp, ptail, ctail = _prompt_layer("pool", xp, pre, ffn_w[l], tm)
            pool_p.append(ptail[:, 2 * SUBLANES - POOL_HIST:])
            xs, nst, nhist = _sample_layer("pool", xs, (st_t[l],) + pre, ffn_w[l], hist_t[l], bn)
            pool_s.append(jnp.swapaxes(nst, 0, 1))
        else:
            j = l - n_pool
            if j == 0:
                *kvs_p, q_p = _proj(xp.reshape(b * s, d), s // tm, kv_w_p, q_w_p[j], tm)
                *kvs_s, q_s = _proj(xs, 1, kv_w_s, q_w_s[j], n)
                kvs_p = [a.reshape(b, s, 2 * gw) for a in kvs_p]
            else:
                (q_p,) = _proj(xp.reshape(b * s, d), s // tm, None, q_w_p[j], tm)
                (q_s,) = _proj(xs, 1, None, q_w_s[j], n)
            o_p = _prompt_attn(q_p.reshape(b, s, N_GROUPS * gw), kvs_p)
            o_s = _sample_attn(q_s, kvs_s, kc, bb)
            xp, ctail = _prompt_layer("attn", xp, (o_p, wo_b[j]), ffn_w[l], tm)
            xs, nhist = _sample_layer("attn", xs, (o_s, wo_b[j]), ffn_w[l], hist_t[l], bn)
        conv_p.append(ctail[:, SUBLANES - (CONV_W - 1):])
        conv_s.append(jnp.swapaxes(nhist, 0, 1))

    outs = [xp, xs.reshape(n, 1, d), jnp.stack(pool_p), jnp.stack(pool_s),
            jnp.stack(conv_p), jnp.stack(conv_s)]
    for g, (win, _) in enumerate(ATTN_GROUPS):
        rows_p = kvs_p[g][:, s - min(win, s):].reshape(b, min(win, s), 2, HEADS, HEAD_DIM)
        outs += [rows_p, kvs_s[g].reshape(n, 1, 2, HEADS, HEAD_DIM)]
    return tuple(outs)
```

```python
import functools

import jax
import jax.numpy as jnp
from jax import lax
from jax.experimental import pallas as pl
from jax.experimental.pallas import tpu as pltpu

F32 = jnp.float32
BF16 = jnp.bfloat16

NORM_EPS = 1e-6
NEG_INF = -1e30
POOL_WINDOWS = (2, 4, 8, 16)
POOL_HIST = max(POOL_WINDOWS) - 1
CONV_W = 3
ATTN_GROUPS = ((128, 1), (512, 4), (2048, 16))
N_GROUPS = len(ATTN_GROUPS)
HEADS = 8
HEAD_DIM = 64
ROT_DIM = HEAD_DIM // 4
ROPE_THETA = 500000.0
ATTN_BLOCK = 128
PAST_LEN = 2048

LANES = 128
SUBLANES = 8
VMEM_LIMIT = 56 * 1024 * 1024


def _cparams(n_grid):
    return pltpu.CompilerParams(dimension_semantics=("arbitrary",) * n_grid,
                                vmem_limit_bytes=VMEM_LIMIT)


def _const_spec(shape):
    nd = len(shape)
    return pl.BlockSpec(shape, lambda *_: (0,) * nd, pipeline_mode=pl.Buffered(1))


def _rms(x, g):
    ms = jnp.mean(x * x, axis=-1, keepdims=True)
    return x * lax.rsqrt(ms + NORM_EPS) * g


def _silu(a):
    return a / (1.0 + jnp.exp(-a))


def _ffn_core(x1, gf_ref, wup_ref, cw_ref, cb_ref, wdown_ref, hbuf, act, fc, rows):
    n_blocks, to_rows, prev_rows, from_rows = rows
    d_ff = wdown_ref.shape[0]
    q = hbuf.shape[0] // n_blocks
    blocks = [slice(j * q, (j + 1) * q) for j in range(n_blocks)]
    h = _rms(x1, gf_ref[...])
    for j, rb in enumerate(blocks):
        hbuf[rb, :] = to_rows(h, j).astype(BF16)
    for c in range(d_ff // fc):
        halves = []
        for base in (0, d_ff):
            cols = slice(base + c * fc, base + (c + 1) * fc)
            u = jnp.dot(hbuf[...], wup_ref[:, cols], preferred_element_type=F32)
            ub = [u[rb] for rb in blocks]
            prev = prev_rows(cols, ub)
            halves.append([cb_ref[:, cols] + ((cw_ref[0:1, cols] * u2 + cw_ref[1:2, cols] * u1)
                                              + cw_ref[2:3, cols] * u0)
                           for u0, (u1, u2) in zip(ub, prev)])
        for rb, a, b in zip(blocks, *halves):
            act[rb, c * fc:(c + 1) * fc] = (_silu(a) * b).astype(BF16)
    return x1 + from_rows(jnp.dot(act[...], wdown_ref[...], preferred_element_type=F32))


PROMPT_ROW_BLOCKS = 4


def _prompt_rows(slab, sbuf, ucarry, tm):
    nb = PROMPT_ROW_BLOCKS
    q = tm // nb
    n_slab = slab.shape[0]

    def to_rows(h, j):
        if j == 0:
            for k in range(n_slab):
                slab[k] = h[:, k * LANES:(k + 1) * LANES]
        return jnp.concatenate([slab[k, pl.ds(j, q, stride=nb), :] for k in range(n_slab)], axis=1)

    def shift_down(slot, blk, carry_row):
        sbuf[slot, SUBLANES - 1:SUBLANES, :] = carry_row
        sbuf[slot, SUBLANES:SUBLANES + q, :] = blk
        return sbuf[slot, SUBLANES - 1:SUBLANES - 1 + q, :]

    def prev_rows(cols, ub):
        s3 = shift_down(0, ub[3], ucarry[SUBLANES - 1:SUBLANES, cols])
        s2 = shift_down(1, ub[2], ucarry[SUBLANES - 2:SUBLANES - 1, cols])
        ucarry[SUBLANES - 1:SUBLANES, cols] = ub[3][q - 1:q]
        ucarry[SUBLANES - 2:SUBLANES - 1, cols] = ub[2][q - 1:q]
        return [(s3, s2), (ub[0], s3), (ub[1], ub[0]), (ub[2], ub[1])]

    def from_rows(y):
        for j in range(nb):
            for k in range(n_slab):
                slab[k, pl.ds(j, q, stride=nb), :] = y[j * q:(j + 1) * q, k * LANES:(k + 1) * LANES]
        return jnp.concatenate([slab[k] for k in range(n_slab)], axis=1)

    return nb, to_rows, prev_rows, from_rows


def _sample_rows(hist_ref, newhist_ref):
    def prev_rows(cols, ub):
        newhist_ref[0, :, cols] = hist_ref[1, :, cols]
        newhist_ref[1, :, cols] = ub[0]
        return [(hist_ref[1, :, cols], hist_ref[0, :, cols])]
    return 1, (lambda h, j: h), prev_rows, (lambda y: y)


def _pool_mix(xn, shifted, cnt_of, wp_ref, ps_ref):
    gw = xn.shape[1] // len(POOL_WINDOWS)
    parts = []
    for g, w in enumerate(POOL_WINDOWS):
        sl = slice(g * gw, (g + 1) * gw)
        cur = xn[:, sl]
        s = cur
        for k in range(1, w):
            s = s + shifted(k, sl)
        d = s / cnt_of(w, cur.shape) - cur
        parts.append(jnp.dot(d.astype(BF16), wp_ref[g], preferred_element_type=F32))
    return jnp.concatenate(parts, axis=-1) * ps_ref[...]


def _prompt_layer_kernel(mode, tm, fc, *refs):
    if mode == "pool":
        (x_ref, gp_ref, wp_ref, ps_ref, gf_ref, wup_ref, cw_ref, cb_ref, wdown_ref,
         xo_ref, ptail_ref, ctail_ref, xnbuf, hbuf, act, slab, sbuf, ucarry) = refs
    else:
        (x_ref, o_ref, wo_ref, gf_ref, wup_ref, cw_ref, cb_ref, wdown_ref,
         xo_ref, ctail_ref, hbuf, act, slab, sbuf, ucarry) = refs
    t = pl.program_id(1)
    hist = 2 * SUBLANES

    @pl.when(t == 0)
    def _():
        ucarry[...] = jnp.zeros_like(ucarry)
        if mode == "pool":
            xnbuf[0:hist, :] = jnp.zeros((hist, xnbuf.shape[1]), F32)

    x = x_ref[...]
    if mode == "pool":
        xn = _rms(x, gp_ref[...])
        xnbuf[hist:hist + tm, :] = xn

        def shifted(k, sl):
            return xnbuf[hist - k:hist - k + tm, sl]

        def cnt_of(w, shape):
            pos = t * tm + lax.broadcasted_iota(jnp.int32, shape, 0)
            return jnp.minimum(pos + 1, w).astype(F32)

        x1 = x + _pool_mix(xn, shifted, cnt_of, wp_ref, ps_ref)
        tail = xnbuf[tm:tm + hist, :]
        ptail_ref[...] = tail
        xnbuf[0:hist, :] = tail
    else:
        x1 = x + jnp.dot(o_ref[...].astype(BF16), wo_ref[...], preferred_element_type=F32)

    xo_ref[...] = _ffn_core(x1, gf_ref, wup_ref, cw_ref, cb_ref, wdown_ref, hbuf, act, fc,
                            _prompt_rows(slab, sbuf, ucarry, tm))
    ctail_ref[...] = ucarry[...]


def _sample_layer_kernel(mode, fc, *refs):
    if mode == "pool":
        (x_ref, st_ref, gp_ref, wp_ref, ps_ref, gf_ref, wup_ref, cw_ref, cb_ref, wdown_ref,
         hist_ref, xo_ref, nst_ref, nhist_ref, hbuf, act) = refs
    else:
        (x_ref, o_ref, wo_ref, gf_ref, wup_ref, cw_ref, cb_ref, wdown_ref,
         hist_ref, xo_ref, nhist_ref, hbuf, act) = refs
    x = x_ref[...]
    if mode == "pool":
        xn = _rms(x, gp_ref[...])
        n_hist = st_ref.shape[0]

        def shifted(k, sl):
            return st_ref[n_hist - k, :, sl]

        def cnt_of(w, shape):
            return float(min(PAST_LEN + 1, w))

        x1 = x + _pool_mix(xn, shifted, cnt_of, wp_ref, ps_ref)
        for i in range(n_hist - 1):
            nst_ref[i] = st_ref[i + 1]
        nst_ref[n_hist - 1] = xn
    else:
        x1 = x + jnp.dot(o_ref[...].astype(BF16), wo_ref[...], preferred_element_type=F32)
    xo_ref[...] = _ffn_core(x1, gf_ref, wup_ref, cw_ref, cb_ref, wdown_ref, hbuf, act, fc,
                            _sample_rows(hist_ref, nhist_ref))


def _layer_spec(stacked, l):
    nd = stacked.ndim - 1
    return pl.BlockSpec((None,) + stacked.shape[1:], lambda *_: (l,) + (0,) * nd,
                        pipeline_mode=pl.Buffered(1))


def _pick_fc(f):
    for fc in (256, 128):
        if f % fc == 0:
            return fc
    return f


def _prompt_layer(mode, x, pre, pre_l, ffn_w, l, tm):
    b, s, d = x.shape
    f2, f = ffn_w[1].shape[2], ffn_w[4].shape[1]
    fc = _pick_fc(f)
    row = pl.BlockSpec((None, tm, d), lambda i, j: (i, j, 0))
    hist = 2 * SUBLANES
    ctail_spec = pl.BlockSpec((None, SUBLANES, f2), lambda i, j: (i, 0, 0))
    ctail_shape = jax.ShapeDtypeStruct((b, SUBLANES, f2), F32)
    common_scratch = [pltpu.VMEM((tm, d), BF16), pltpu.VMEM((tm, f), BF16),
                      pltpu.VMEM((d // LANES, tm, LANES), F32),
                      pltpu.VMEM((2, tm // PROMPT_ROW_BLOCKS + SUBLANES, fc), F32),
                      pltpu.VMEM((SUBLANES, f2), F32)]
    if mode == "pool":
        in_specs = [row] + [_layer_spec(a, pre_l) for a in pre]
        args = [x, *pre]
        out_specs = [row, pl.BlockSpec((None, hist, d), lambda i, j: (i, 0, 0)), ctail_spec]
        out_shape = [jax.ShapeDtypeStruct(x.shape, F32), jax.ShapeDtypeStruct((b, hist, d), F32),
                     ctail_shape]
        scratch = [pltpu.VMEM((tm + hist, d), F32)] + common_scratch
    else:
        o, wo = pre
        in_specs = [row, pl.BlockSpec((None, tm, o.shape[2]), lambda i, j: (i, j, 0)),
                    _layer_spec(wo, pre_l)]
        args = [x, o, wo]
        out_specs = [row, ctail_spec]
        out_shape = [jax.ShapeDtypeStruct(x.shape, F32), ctail_shape]
        scratch = common_scratch
    return pl.pallas_call(
        functools.partial(_prompt_layer_kernel, mode, tm, fc),
        grid=(b, s // tm),
        in_specs=in_specs + [_layer_spec(a, l) for a in ffn_w],
        out_specs=out_specs, out_shape=out_shape, scratch_shapes=scratch,
        compiler_params=_cparams(2), name=f"prompt_{mode}_layer",
    )(*args, *ffn_w)


def _sample_layer(mode, x, pre, pre_l, ffn_w, l, hist_t, bn):
    n, d = x.shape
    f2, f = ffn_w[1].shape[2], ffn_w[4].shape[1]
    fc = _pick_fc(f)
    row = lambda width: pl.BlockSpec((bn, width), lambda i: (i, 0))
    tmaj = lambda a, k: pl.BlockSpec((None, a.shape[1], bn, a.shape[3]), lambda i: (k, 0, i, 0))
    tmaj_out = lambda a: pl.BlockSpec((a.shape[1], bn, a.shape[3]), lambda i: (0, i, 0))
    hist_shape = jax.ShapeDtypeStruct(hist_t.shape[1:], F32)
    if mode == "pool":
        st_t = pre[0]
        in_specs = [row(d), tmaj(st_t, pre_l)] + [_layer_spec(a, pre_l) for a in pre[1:]]
        args = [x, *pre]
        out_specs = [row(d), tmaj_out(st_t), tmaj_out(hist_t)]
        out_shape = [jax.ShapeDtypeStruct(x.shape, F32), jax.ShapeDtypeStruct(st_t.shape[1:], F32),
                     hist_shape]
    else:
        o, wo = pre
        in_specs = [row(d), row(o.shape[1]), _layer_spec(wo, pre_l)]
        args = [x, o, wo]
        out_specs = [row(d), tmaj_out(hist_t)]
        out_shape = [jax.ShapeDtypeStruct(x.shape, F32), hist_shape]
    return pl.pallas_call(
        functools.partial(_sample_layer_kernel, mode, fc),
        grid=(n // bn,),
        in_specs=in_specs + [_layer_spec(a, l) for a in ffn_w] + [tmaj(hist_t, l)],
        out_specs=out_specs, out_shape=out_shape,
        scratch_shapes=[pltpu.VMEM((bn, d), BF16), pltpu.VMEM((bn, f), BF16)],
        compiler_params=_cparams(1), name=f"sample_{mode}_layer",
    )(*args, *ffn_w, hist_t)


HEAD_TILE = 2 * LANES


def _head_norm_rot(x, bd_ref, gain, tabs):
    gcos, sina, sinb = tabs
    ms = jnp.dot((x * x).astype(BF16), bd_ref[...], preferred_element_type=F32)
    outs = []
    for j in range(x.shape[1] // LANES):
        xs = x[:, j * LANES:(j + 1) * LANES]
        xg = xs * gain
        outs.append(xs * gcos + pltpu.roll(xg, LANES - ROT_DIM // 2, 1) * sina
                    + pltpu.roll(xg, ROT_DIM // 2, 1) * sinb)
    return jnp.concatenate(outs, axis=1) * lax.rsqrt(ms + NORM_EPS)


def _proj_kernel(do_kv, kvt_groups, *refs):
    x_ref, bd_ref = refs[0:2]
    if do_kv:
        ktabs, (gkv_ref, wkv_ref, kg_ref) = refs[2:5], refs[5:8]
        qtabs, (gq_ref, wq_ref, qg_ref) = refs[8:11], refs[11:14]
        kv_refs, q_ref, kvt_refs = refs[14:17], refs[17], refs[18:]
    else:
        qtabs, (gq_ref, wq_ref, qg_ref) = refs[2:5], refs[5:8]
        q_ref = refs[8]
    x = x_ref[...]
    gw = HEADS * HEAD_DIM
    if do_kv:
        h = _rms(x, gkv_ref[...]).astype(BF16)
        kv = jnp.dot(h, wkv_ref[...], preferred_element_type=F32)
        tabs = [t[...] for t in ktabs]
        for g, out_ref in enumerate(kv_refs):
            for j in range(gw // HEAD_TILE):
                c0 = g * gw + j * HEAD_TILE
                out_ref[:, j * HEAD_TILE:(j + 1) * HEAD_TILE] = _head_norm_rot(
                    kv[:, c0:c0 + HEAD_TILE], bd_ref, kg_ref[...], tabs)
            v0 = N_GROUPS * gw + g * gw
            out_ref[:, gw:2 * gw] = kv[:, v0:v0 + gw]
        for g, kvt_ref in zip(kvt_groups, kvt_refs):
            kvt_ref[...] = kv_refs[g][...].T
    h = _rms(x, gq_ref[...]).astype(BF16)
    q = jnp.dot(h, wq_ref[...], preferred_element_type=F32)
    tabs = [t[...] for t in qtabs]
    for j in range(q.shape[1] // HEAD_TILE):
        cols = slice(j * HEAD_TILE, (j + 1) * HEAD_TILE)
        q_ref[:, cols] = _head_norm_rot(q[:, cols], bd_ref, qg_ref[...], tabs)


def _proj(x2d, n_tab_blocks, kv_w, q_w, tm, kvt_groups=()):
    r, d = x2d.shape
    row = pl.BlockSpec((tm, d), lambda i: (i, 0))
    tab = pl.BlockSpec((tm, LANES), lambda i: (i % n_tab_blocks, 0))
    gw = HEADS * HEAD_DIM
    head = lax.broadcasted_iota(jnp.int32, (HEAD_TILE, HEAD_TILE), 0) // HEAD_DIM
    bd = jnp.where(head == head.T, 1.0 / HEAD_DIM, 0.0).astype(BF16)
    in_specs, args = [row, _const_spec(bd.shape)], [x2d, bd]
    out_specs, out_shape = [], []
    for w in (kv_w, q_w):
        if w is None:
            continue
        tabs, g, wmat, hg, layer = w
        wspec = lambda a, layer=layer: _const_spec(a.shape) if layer is None else _layer_spec(a, layer)
        in_specs += [tab, tab, tab, wspec(g), wspec(wmat), _const_spec((1, LANES))]
        args += [*tabs, g, wmat, hg]
    if kv_w is not None:
        out_specs += [pl.BlockSpec((tm, 2 * gw), lambda i: (i, 0))] * N_GROUPS
        out_shape += [jax.ShapeDtypeStruct((r, 2 * gw), F32)] * N_GROUPS
    n_q = q_w[2].shape[-1]
    out_specs.append(pl.BlockSpec((tm, n_q), lambda i: (i, 0)))
    out_shape.append(jax.ShapeDtypeStruct((r, n_q), F32))
    seq = n_tab_blocks * tm
    for _ in kvt_groups:
        out_specs.append(pl.BlockSpec((None, 2 * gw, tm), lambda i: (i // n_tab_blocks, 0, i % n_tab_blocks)))
        out_shape.append(jax.ShapeDtypeStruct((r // seq, 2 * gw, seq), F32))
    return pl.pallas_call(
        functools.partial(_proj_kernel, kv_w is not None, tuple(kvt_groups)),
        grid=(r // tm,), in_specs=in_specs, out_specs=out_specs, out_shape=out_shape,
        compiler_params=_cparams(1), name="proj_kvq" if kv_w is not None else "proj_q",
    )(*args)


def _prompt_attn_kernel(seq, *refs):
    q_refs, k_refs, v_refs = refs[0:3], refs[3:6], refs[6:9]
    o_ref = refs[9]
    qb, kb, vb, tmpf, accs, ms, lsw = refs[10:17]
    blk = ATTN_BLOCK
    lo = lax.broadcasted_iota(jnp.int32, (blk, LANES), 1) < HEAD_DIM
    quarter = seq // 4

    def valid_mask(nk):
        qi = lax.broadcasted_iota(jnp.int32, (2 * blk, nk), 0) & (blk - 1)
        kj = lax.broadcasted_iota(jnp.int32, (2 * blk, nk), 1)
        if nk == blk:
            return kj <= qi
        return lax.bitcast_convert_type(kj - qi, jnp.uint32) <= jnp.uint32(blk)

    for g, (win, dil) in enumerate(ATTN_GROUPS):
        assert win // dil == blk and dil in (1, 4, 16) and seq % (blk * dil) == 0
        sub = seq // dil
        nb = sub // blk

        def stage(src, dil=dil):
            if dil == 16:
                for r1 in range(4):
                    tmpf[r1 * quarter:(r1 + 1) * quarter, :] = src[pl.ds(r1, quarter, stride=4), :]

        def rows_of(src, r, n0, cnt, dil=dil):
            if dil == 1:
                return src[pl.ds(n0, cnt), :]
            if dil == 4:
                return src[pl.ds(r + 4 * n0, cnt, stride=4), :]
            return tmpf[pl.ds((r % 4) * quarter + r // 4 + 4 * n0, cnt, stride=4), :]

        stage(q_refs[g])
        for r in range(dil):
            for n in range(nb):
                qf = rows_of(q_refs[g], r, n * blk, blk)
                base = (r * nb + n) * 2 * blk
                qb[base:base + blk, :] = jnp.where(lo, qf, 0.0).astype(BF16)
                qb[base + blk:base + 2 * blk, :] = jnp.where(lo, 0.0, qf).astype(BF16)
        stage(k_refs[g])
        for r in range(dil):
            kb[r * sub:(r + 1) * sub, :] = rows_of(k_refs[g], r, 0, sub).astype(BF16)
        stage(v_refs[g])
        for r in range(dil):
            vf = rows_of(v_refs[g], r, 0, sub)
            lo_v = lax.broadcasted_iota(jnp.int32, vf.shape, 1) < HEAD_DIM
            vb[0, r * sub:(r + 1) * sub, :] = jnp.where(lo_v, vf, 1.0).astype(BF16)
            vb[1, r * sub:(r + 1) * sub, :] = jnp.where(lo_v, 1.0, vf).astype(BF16)

        def geom(r, n, sub=sub, nb=nb):
            first = n == 0
            nk = blk if first else 2 * blk
            return (r * nb + n) * 2 * blk, r * sub + (n if first else n - 1) * blk, nk

        def scores(r, n):
            q0, k0, nk = geom(r, n)
            s = lax.dot_general(qb[pl.ds(q0, 2 * blk), :], kb[pl.ds(k0, nk), :],
                                (((1,), (1,)), ((), ())), preferred_element_type=F32)
            return jnp.where(valid_mask(nk), s, NEG_INF)

        def softmax(s):
            m = jnp.max(s, axis=-1, keepdims=True)
            return jnp.exp(s - m).astype(BF16), m

        def values(r, n, pm, g=g, dil=dil):
            p, m = pm
            _, k0, nk = geom(r, n)
            pv0 = jnp.dot(p[:blk], vb[0, pl.ds(k0, nk), :], preferred_element_type=F32)
            pv1 = jnp.dot(p[blk:], vb[1, pl.ds(k0, nk), :], preferred_element_type=F32)
            if dil > 1:
                out_rows = pl.ds(r + n * blk * dil, blk, stride=dil)
            else:
                out_rows = pl.ds(n * blk, blk)
            accs[g, out_rows, :] = jnp.where(lo, pv0, pv1)
            ms[g, out_rows, :] = jnp.where(lo, m[:blk], m[blk:])
            lsw[g, out_rows, :] = jnp.where(lo, pv1, pv0)

        todo = [(r, n) for r in range(dil) for n in range(nb)]
        s_q, p_q = [], []
        for i in range(len(todo) + 2):
            if i < len(todo):
                s_q.append(scores(*todo[i]))
            if 1 <= i <= len(todo):
                p_q.append(softmax(s_q.pop(0)))
            if i >= 2:
                values(*todo[i - 2], p_q.pop(0))

    def merge(i, carry):
        rows = pl.ds(pl.multiple_of(i * blk, blk), blk)
        m = jnp.maximum(jnp.maximum(ms[0, rows, :], ms[1, rows, :]), ms[2, rows, :])
        num = jnp.zeros((blk, LANES), F32)
        den = jnp.zeros((blk, LANES), F32)
        for g in range(N_GROUPS):
            w = jnp.exp(ms[g, rows, :] - m)
            num = num + w * accs[g, rows, :]
            den = den + w * pltpu.roll(lsw[g, rows, :], HEAD_DIM, 1)
        o_ref[rows, :] = (num / den).astype(o_ref.dtype)
        return carry

    lax.fori_loop(0, seq // blk, merge, 0, unroll=4)


def _prompt_attn(q, kvs):
    b, s, _ = q.shape
    gw = HEADS * HEAD_DIM
    n_hp = gw // LANES
    blk_spec = lambda col: pl.BlockSpec((None, s, LANES), lambda i, j, col=col: (i, 0, col + j))
    in_specs = ([blk_spec(g * n_hp) for g in range(N_GROUPS)]
                + [blk_spec(0)] * N_GROUPS
                + [blk_spec(n_hp)] * N_GROUPS)
    args = [q] * N_GROUPS + list(kvs) + list(kvs)
    return pl.pallas_call(
        functools.partial(_prompt_attn_kernel, s),
        grid=(b, n_hp),
        in_specs=in_specs,
        out_specs=pl.BlockSpec((None, s, LANES), lambda i, j: (i, 0, j)),
        out_shape=jax.ShapeDtypeStruct((b, s, gw), BF16),
        scratch_shapes=[pltpu.VMEM((2 * s, LANES), BF16),
                        pltpu.VMEM((s, LANES), BF16),
                        pltpu.VMEM((2, s, LANES), BF16),
                        pltpu.VMEM((s, LANES), F32)]
                       + [pltpu.VMEM((N_GROUPS, s, LANES), F32)] * 3,
        compiler_params=_cparams(2), name="prompt_attn",
    )(*args)


def _compact_kernel(bb, c0_ref, c1_ref, c2_ref, sel1_ref, sel2_ref, o_ref):
    nk = c0_ref.shape[-1]
    for b in range(bb):
        for two in range(2):
            o_ref[b, two, :, 0:nk] = c0_ref[b, two].astype(BF16)
            for g, (c_ref, sel_ref) in enumerate(((c1_ref, sel1_ref), (c2_ref, sel2_ref)), start=1):
                o_ref[b, two, :, g * nk:(g + 1) * nk] = jnp.dot(
                    c_ref[b, two].astype(BF16), sel_ref[...], preferred_element_type=F32).astype(BF16)


def _compact_caches(caches, bb):
    n = caches[0].shape[0]
    gw = HEADS * HEAD_DIM
    args, in_specs, sels = [], [], []
    for c, (win, dil) in zip(caches, ATTN_GROUPS):
        n_buf, n_back = c.shape[1], win // dil
        assert n_buf == n_back * dil, "window buffer must hold exactly n_back dilated rows"
        args.append(jnp.transpose(c, (0, 2, 3, 4, 1)).reshape(n, 2, gw, n_buf))
        in_specs.append(pl.BlockSpec((bb, 2, gw, n_buf), lambda i: (i, 0, 0, 0)))
        if dil > 1:
            t = lax.broadcasted_iota(jnp.int32, (n_buf, n_back), 0)
            m = lax.broadcasted_iota(jnp.int32, (n_buf, n_back), 1)
            sels.append((t == dil * m).astype(BF16))
    n_back = ATTN_GROUPS[0][0] // ATTN_GROUPS[0][1]
    return pl.pallas_call(
        functools.partial(_compact_kernel, bb),
        grid=(n // bb,),
        in_specs=in_specs + [_const_spec(s_.shape) for s_ in sels],
        out_specs=pl.BlockSpec((bb, 2, gw, N_GROUPS * n_back), lambda i: (i, 0, 0, 0)),
        out_shape=jax.ShapeDtypeStruct((n, 2, gw, N_GROUPS * n_back), BF16),
        compiler_params=_cparams(1), name="compact_caches",
    )(*args, *sels)


def _sample_attn_kernel(bb, q_ref, kv0_ref, kv1_ref, kv2_ref, kc_ref, o_ref):
    gw = HEADS * HEAD_DIM
    head_of_row = lax.broadcasted_iota(jnp.int32, (HEADS, gw), 0)
    head_of_col = lax.broadcasted_iota(jnp.int32, (HEADS, gw), 1) // HEAD_DIM
    diag = head_of_row == head_of_col

    nk = kc_ref.shape[-1] // N_GROUPS
    kvn_refs = (kv0_ref, kv1_ref, kv2_ref)

    def body(b, carry):
        row = pl.ds(b, 1)
        q_bd = [jnp.where(diag, q_ref[row, g * gw:(g + 1) * gw], 0.0) for g in range(N_GROUPS)]
        s_all = jnp.dot(jnp.concatenate(q_bd, axis=0).astype(BF16), kc_ref[b, 0],
                        preferred_element_type=F32)
        s = [s_all[g * HEADS:(g + 1) * HEADS, g * nk:(g + 1) * nk] for g in range(N_GROUPS)]
        s_new = [jnp.sum(q_bd[g] * kvn_refs[g][row, 0:gw], axis=-1, keepdims=True)
                 for g in range(N_GROUPS)]
        m = functools.reduce(jnp.maximum, [jnp.max(a, axis=-1, keepdims=True) for a in s] + s_new)
        p = [jnp.exp(a - m) for a in s]
        p_new = [jnp.exp(a - m) for a in s_new]
        l = sum(jnp.sum(a, axis=-1, keepdims=True) for a in p) + sum(p_new)
        acc = lax.dot_general(jnp.concatenate(p, axis=1).astype(BF16), kc_ref[b, 1],
                              (((1,), (1,)), ((), ())), preferred_element_type=F32)
        for g in range(N_GROUPS):
            acc = acc + p_new[g] * kvn_refs[g][row, gw:2 * gw]
        o_ref[row, :] = jnp.sum(jnp.where(diag, acc / l, 0.0), axis=0, keepdims=True)
        return carry

    lax.fori_loop(0, bb, body, 0, unroll=4)


def _sample_attn(q, kvn, kc, bb):
    n = q.shape[0]
    gw = HEADS * HEAD_DIM
    rows = lambda width: pl.BlockSpec((bb, width), lambda i: (i, 0))
    return pl.pallas_call(
        functools.partial(_sample_attn_kernel, bb),
        grid=(n // bb,),
        in_specs=[rows(q.shape[1])] + [rows(2 * gw)] * N_GROUPS
                 + [pl.BlockSpec((bb,) + kc.shape[1:], lambda i: (i, 0, 0, 0))],
        out_specs=rows(gw),
        out_shape=jax.ShapeDtypeStruct((n, gw), F32),
        compiler_params=_cparams(1), name="sample_attn",
    )(q, *kvn, kc)


def _rot_tables(pos):
    half = ROT_DIM // 2
    lane = lax.broadcasted_iota(jnp.int32, (pos.shape[0], LANES), 1) % HEAD_DIM
    inv_freq = ROPE_THETA ** (-(2 * (lane % half)).astype(F32) / ROT_DIM)
    ang = pos.astype(F32)[:, None] * inv_freq
    cos, sin = jnp.cos(ang), jnp.sin(ang)
    cosb = jnp.where(lane < ROT_DIM, cos, 1.0)
    sina = jnp.where(lane < half, -sin, 0.0)
    sinb = jnp.where((lane >= half) & (lane < ROT_DIM), sin, 0.0)
    return cosb, sina, sinb


def kernel(x_prompt, x_sample, state_pool, state_conv, cache_kv_w128, cache_kv_w512, cache_kv_w2048,
           g_pool, w_pool, pool_scale, g_ffn, w_up, conv_w, conv_b, w_down, g_kv, w_kv, k_gain,
           g_attn, w_q, q_gain, w_o):
    b, s, d = x_prompt.shape
    n = x_sample.shape[0]
    depth = w_up.shape[0]
    n_pool = g_pool.shape[0]
    gw = HEADS * HEAD_DIM
    rows3 = lambda a: a.reshape(a.shape[0], 1, a.shape[1])
    lane2 = lambda a: jnp.tile(a, LANES // HEAD_DIM).reshape(1, LANES)
    tm = min(512, s)
    bn = min(64, n)
    bb = min(8, n)

    ffn_w = (rows3(g_ffn), w_up.astype(BF16), conv_w, rows3(conv_b), w_down.astype(BF16))
    pool_w = (rows3(g_pool), w_pool.astype(BF16), rows3(pool_scale))
    wkv_b, wq_b, wo_b = w_kv.astype(BF16), w_q.astype(BF16), w_o.astype(BF16)
    q_scale = HEAD_DIM ** -0.5

    def proj_w(pos):
        cosb, sina, sinb = _rot_tables(pos)
        kg = lane2(k_gain)
        kv_w = ((cosb * kg, sina, sinb), g_kv.reshape(1, d), wkv_b, kg, None)
        q_w = []
        for j in range(depth - n_pool):
            qg = lane2(q_gain[j])
            q_w.append(((cosb * (qg * q_scale), sina * q_scale, sinb * q_scale), rows3(g_attn), wq_b, qg, j))
        return kv_w, q_w

    kv_w_p, q_w_p = proj_w(jnp.arange(s, dtype=jnp.int32))
    kv_w_s, q_w_s = proj_w(jnp.full((n,), PAST_LEN, jnp.int32))
    kc = _compact_caches((cache_kv_w128, cache_kv_w512, cache_kv_w2048), 1)

    xp, xs = x_prompt, x_sample.reshape(n, d)
    pool_p, pool_s, conv_p, conv_s = [], [], [], []
    hist_t = jnp.swapaxes(state_conv, 1, 2)
    st_t = jnp.swapaxes(state_pool, 1, 2)
    big = max(range(N_GROUPS), key=lambda g: ATTN_GROUPS[g][0])
    kvs_p = kvs_s = kvt_p = None
    for l in range(depth):
        if l < n_pool:
            xp, ptail, ctail = _prompt_layer("pool", xp, pool_w, l, ffn_w, l, tm)
            pool_p.append(ptail[:, 2 * SUBLANES - POOL_HIST:])
            xs, nst, nhist = _sample_layer("pool", xs, (st_t,) + pool_w, l, ffn_w, l, hist_t, bn)
            pool_s.append(jnp.swapaxes(nst, 0, 1))
        else:
            j = l - n_pool
            if j == 0:
                *kvs_p, q_p, kvt_p = _proj(xp.reshape(b * s, d), s // tm, kv_w_p, q_w_p[j], tm, (big,))
                *kvs_s, q_s = _proj(xs, 1, kv_w_s, q_w_s[j], n)
                kvs_p = [a.reshape(b, s, 2 * gw) for a in kvs_p]
            else:
                (q_p,) = _proj(xp.reshape(b * s, d), s // tm, None, q_w_p[j], tm)
                (q_s,) = _proj(xs, 1, None, q_w_s[j], n)
            o_p = _prompt_attn(q_p.reshape(b, s, N_GROUPS * gw), kvs_p)
            o_s = _sample_attn(q_s, kvs_s, kc, bb)
            xp, ctail = _prompt_layer("attn", xp, (o_p, wo_b), j, ffn_w, l, tm)
            xs, nhist = _sample_layer("attn", xs, (o_s, wo_b), j, ffn_w, l, hist_t, bn)
        conv_p.append(ctail[:, SUBLANES - (CONV_W - 1):])
        conv_s.append(jnp.swapaxes(nhist, 0, 1))

    outs = [xp, xs.reshape(n, 1, d), jnp.stack(pool_p), jnp.stack(pool_s),
            jnp.stack(conv_p), jnp.stack(conv_s)]
    for g, (win, _) in enumerate(ATTN_GROUPS):
        if g == big and win >= s:
            rows_p = jnp.transpose(kvt_p.reshape(b, 2, HEADS, HEAD_DIM, s), (0, 4, 1, 2, 3))
        else:
            rows_p = kvs_p[g][:, s - min(win, s):].reshape(b, min(win, s), 2, HEADS, HEAD_DIM)
        outs += [rows_p, kvs_s[g].reshape(n, 1, 2, HEADS, HEAD_DIM)]
    return tuple(outs)
```

```python
import collections
import functools

import jax
import jax.numpy as jnp
from jax import lax
from jax.experimental import pallas as pl
from jax.experimental.pallas import tpu as pltpu

F32 = jnp.float32
BF16 = jnp.bfloat16

NORM_EPS = 1e-6
NEG_INF = -1e30
POOL_WINDOWS = (2, 4, 8, 16)
POOL_HIST = max(POOL_WINDOWS) - 1
CONV_W = 3
ATTN_GROUPS = ((128, 1), (512, 4), (2048, 16))
N_GROUPS = len(ATTN_GROUPS)
HEADS = 8
HEAD_DIM = 64
ROT_DIM = HEAD_DIM // 4
ROPE_THETA = 500000.0
ATTN_BLOCK = 128
PAST_LEN = 2048

LANES = 128
SUBLANES = 8
VMEM_LIMIT = 62 * 1024 * 1024


def _cparams(n_grid):
    return pltpu.CompilerParams(dimension_semantics=("arbitrary",) * n_grid,
                                vmem_limit_bytes=VMEM_LIMIT)


def _const_spec(shape):
    nd = len(shape)
    return pl.BlockSpec(shape, lambda *_: (0,) * nd, pipeline_mode=pl.Buffered(1))


def _layer_spec(stacked, l):
    nd = stacked.ndim - 1
    return pl.BlockSpec((None,) + stacked.shape[1:], lambda *_: (l,) + (0,) * nd,
                        pipeline_mode=pl.Buffered(1))


def _rms(x, g):
    ms = jnp.mean(x * x, axis=-1, keepdims=True)
    return x * lax.rsqrt(ms + NORM_EPS) * g


def _silu(a):
    return a / (1.0 + jnp.exp(-a))


_Side = collections.namedtuple("_Side", "c_args sels base kc per_step use_mxu")

COMPACT_ROWS = 64


def _compact_tasks(c_refs, sel_refs, o_ref, use_mxu):
    nk = c_refs[0].shape[-1]
    n_rows = c_refs[0].shape[1]
    assert nk == LANES and n_rows % COMPACT_ROWS == 0
    tasks = []

    def cast_task(u):
        def run():
            o_ref[u, :, 0:nk] = c_refs[0][u].astype(BF16)
        return run

    def chunk_task(u, g, r0):
        rows, out_cols = slice(r0, r0 + COMPACT_ROWS), slice(g * nk, (g + 1) * nk)
        dil = c_refs[g].shape[-1] // nk

        def run():
            if use_mxu:
                o_ref[u, rows, out_cols] = jnp.dot(c_refs[g][u, rows, :].astype(BF16), sel_refs[g - 1][...],
                                                   preferred_element_type=F32).astype(BF16)
                return
            keep = lax.broadcasted_iota(jnp.int32, (COMPACT_ROWS, LANES), 1) % dil == 0
            acc = jnp.where(keep, c_refs[g][u, rows, 0:LANES], 0.0)
            for j in range(1, dil):
                tile = jnp.where(keep, c_refs[g][u, rows, j * LANES:(j + 1) * LANES], 0.0)
                acc = acc + pltpu.roll(tile, j, 1)
            o_ref[u, rows, out_cols] = acc.astype(BF16)
        return run

    for u in range(o_ref.shape[0]):
        tasks.append(cast_task(u))
        for g in range(1, len(c_refs)):
            tasks += [chunk_task(u, g, r0) for r0 in range(0, n_rows, COMPACT_ROWS)]
    return tasks


def _drain(tasks, n_slots, i):
    for task in tasks[len(tasks) * i // n_slots:len(tasks) * (i + 1) // n_slots]:
        task()


def _side_plan(caches):
    n = caches[0].shape[0]
    gw = HEADS * HEAD_DIM
    c_args, sels = [], []
    for c, (win, dil) in zip(caches, ATTN_GROUPS):
        n_buf, n_back = c.shape[1], win // dil
        assert n_buf == n_back * dil, "window buffer must hold exactly n_back dilated rows"
        c_args.append(jnp.transpose(c, (0, 2, 3, 4, 1)).reshape(n, 2, gw, n_buf))
        if dil > 1:
            t = lax.broadcasted_iota(jnp.int32, (n_buf, n_back), 0)
            m = lax.broadcasted_iota(jnp.int32, (n_buf, n_back), 1)
            sels.append((t == dil * m).astype(BF16))
    return c_args, sels


def _side_io(side, step_of):
    gw = HEADS * HEAD_DIM
    n = side.c_args[0].shape[0]
    n_keys = N_GROUPS * (ATTN_GROUPS[0][0] // ATTN_GROUPS[0][1])
    assert side.per_step in (1, 2) and side.base % side.per_step == 0
    unit = lambda *idx: side.base + side.per_step * step_of(*idx)
    blk_idx = lambda *idx: (unit(*idx) // 2, (unit(*idx) % 2) // side.per_step, 0, 0)
    c_specs = [pl.BlockSpec((None, side.per_step, gw, c.shape[3]), blk_idx) for c in side.c_args]
    sels = side.sels if side.use_mxu else []
    in_specs = c_specs + [_const_spec(s_.shape) for s_ in sels]
    args = list(side.c_args) + list(sels)
    if side.kc is not None:
        in_specs.append(pl.BlockSpec(memory_space=pl.ANY))
        args.append(side.kc)
    out_spec = pl.BlockSpec((None, side.per_step, gw, n_keys), blk_idx)
    out_shape = jax.ShapeDtypeStruct((n, 2, gw, n_keys), BF16)
    return in_specs, args, out_spec, out_shape


def _split_side(refs, n_in, n_out, side):
    ins, rest = refs[:n_in], refs[n_in:]
    side_in = ()
    if side is not None:
        k = N_GROUPS + (len(side.sels) if side.use_mxu else 0) + (side.kc is not None)
        side_in, rest = rest[:k], rest[k:]
    outs, rest = rest[:n_out], rest[n_out:]
    side_out = None
    if side is not None:
        side_out, rest = rest[0], rest[1:]
    return ins, side_in, outs, side_out, rest


def _side_tasks(side, side_in, side_out):
    if side is None:
        return []
    n_sel = len(side.sels) if side.use_mxu else 0
    return _compact_tasks(side_in[:N_GROUPS], side_in[N_GROUPS:N_GROUPS + n_sel], side_out, side.use_mxu)


def _host_call(kernel, side, step_of, n_in, n_out, *, grid, in_specs, out_specs, out_shape, args,
               scratch_shapes, name):
    aliases = {}
    if side is not None:
        s_in, s_args, s_out, s_shape = _side_io(side, step_of)
        if side.kc is not None:
            aliases = {len(in_specs) + len(s_in) - 1: len(out_specs)}
        in_specs, args = in_specs + s_in, args + s_args
        out_specs, out_shape = out_specs + [s_out], out_shape + [s_shape]
    assert len(in_specs) == n_in + (0 if side is None else len(s_in))
    res = pl.pallas_call(
        functools.partial(kernel, side), grid=grid, in_specs=in_specs, out_specs=out_specs,
        out_shape=out_shape, scratch_shapes=scratch_shapes, input_output_aliases=aliases,
        compiler_params=_cparams(len(grid)), name=name,
    )(*args)
    return (res[:n_out], res[n_out]) if side is not None else (res, None)


def _ffn_core(x1, gf_ref, wup_ref, cw_ref, cb_ref, wdown_ref, hbuf, act, fc, rows, side_tasks=()):
    n_blocks, to_rows, prev_rows, from_rows = rows
    d_ff = wdown_ref.shape[0]
    q = hbuf.shape[0] // n_blocks
    blocks = [slice(j * q, (j + 1) * q) for j in range(n_blocks)]
    h = _rms(x1, gf_ref[...])
    for j, rb in enumerate(blocks):
        hbuf[rb, :] = to_rows(h, j).astype(BF16)
    for c in range(d_ff // fc):
        halves = []
        for base in (0, d_ff):
            cols = slice(base + c * fc, base + (c + 1) * fc)
            u = jnp.dot(hbuf[...], wup_ref[:, cols], preferred_element_type=F32)
            ub = [u[rb] for rb in blocks]
            prev = prev_rows(cols, ub)
            halves.append([cb_ref[:, cols] + ((cw_ref[0:1, cols] * u2 + cw_ref[1:2, cols] * u1)
                                              + cw_ref[2:3, cols] * u0)
                           for u0, (u1, u2) in zip(ub, prev)])
        for rb, a, b in zip(blocks, *halves):
            act[rb, c * fc:(c + 1) * fc] = (_silu(a) * b).astype(BF16)
        _drain(side_tasks, d_ff // fc, c)
    return x1 + from_rows(jnp.dot(act[...], wdown_ref[...], preferred_element_type=F32))


PROMPT_ROW_BLOCKS = 4


def _prompt_rows(slab, sbuf, ucarry, tm):
    nb = PROMPT_ROW_BLOCKS
    q = tm // nb
    n_slab = slab.shape[0]

    def to_rows(h, j):
        if j == 0:
            for k in range(n_slab):
                slab[k] = h[:, k * LANES:(k + 1) * LANES]
        return jnp.concatenate([slab[k, pl.ds(j, q, stride=nb), :] for k in range(n_slab)], axis=1)

    def shift_down(slot, blk, carry_row):
        sbuf[slot, SUBLANES - 1:SUBLANES, :] = carry_row
        sbuf[slot, SUBLANES:SUBLANES + q, :] = blk
        return sbuf[slot, SUBLANES - 1:SUBLANES - 1 + q, :]

    def prev_rows(cols, ub):
        s3 = shift_down(0, ub[3], ucarry[SUBLANES - 1:SUBLANES, cols])
        s2 = shift_down(1, ub[2], ucarry[SUBLANES - 2:SUBLANES - 1, cols])
        ucarry[SUBLANES - 1:SUBLANES, cols] = ub[3][q - 1:q]
        ucarry[SUBLANES - 2:SUBLANES - 1, cols] = ub[2][q - 1:q]
        return [(s3, s2), (ub[0], s3), (ub[1], ub[0]), (ub[2], ub[1])]

    def from_rows(y):
        for j in range(nb):
            for k in range(n_slab):
                slab[k, pl.ds(j, q, stride=nb), :] = y[j * q:(j + 1) * q, k * LANES:(k + 1) * LANES]
        return jnp.concatenate([slab[k] for k in range(n_slab)], axis=1)

    return nb, to_rows, prev_rows, from_rows


def _sample_rows(hist_ref, newhist_ref):
    def prev_rows(cols, ub):
        newhist_ref[0, :, cols] = hist_ref[1, :, cols]
        newhist_ref[1, :, cols] = ub[0]
        return [(hist_ref[1, :, cols], hist_ref[0, :, cols])]
    return 1, (lambda h, j: h), prev_rows, (lambda y: y)


def _pool_mix(xn, window_sum, cnt_of, wp_ref, ps_ref):
    gw = xn.shape[1] // len(POOL_WINDOWS)
    parts = []
    for g, w in enumerate(POOL_WINDOWS):
        sl = slice(g * gw, (g + 1) * gw)
        cur = xn[:, sl]
        d = window_sum(w, sl, cur) / cnt_of(w, cur.shape) - cur
        parts.append(jnp.dot(d.astype(BF16), wp_ref[g], preferred_element_type=F32))
    return jnp.concatenate(parts, axis=-1) * ps_ref[...]


POOL_PAD = 2 * SUBLANES


def _prompt_window_sum(xnbuf, lv, tm):
    hist = 2 * SUBLANES
    span = hist + tm

    def window_sum(w, sl, cur):
        s = xnbuf[POOL_PAD:POOL_PAD + span, sl] + xnbuf[POOL_PAD - 1:POOL_PAD - 1 + span, sl]
        shift, slot = 2, 0
        while shift < w:
            lv[slot, POOL_PAD:POOL_PAD + span, :] = s
            s = lv[slot, POOL_PAD:POOL_PAD + span, :] + lv[slot, POOL_PAD - shift:POOL_PAD - shift + span, :]
            shift, slot = 2 * shift, 1 - slot
        return s[hist:]
    return window_sum


def _prompt_layer_kernel(mode, tm, fc, side, *refs):
    ins, side_in, outs, side_out, scratch = _split_side(refs, 9 if mode == "pool" else 8,
                                                        3 if mode == "pool" else 2, side)
    if mode == "pool":
        x_ref, gp_ref, wp_ref, ps_ref, gf_ref, wup_ref, cw_ref, cb_ref, wdown_ref = ins
        xo_ref, ptail_ref, ctail_ref = outs
        xnbuf, lv, hbuf, act, slab, sbuf, ucarry = scratch
    else:
        x_ref, o_ref, wo_ref, gf_ref, wup_ref, cw_ref, cb_ref, wdown_ref = ins
        xo_ref, ctail_ref = outs
        hbuf, act, slab, sbuf, ucarry = scratch
    t = pl.program_id(1)
    hist = 2 * SUBLANES

    @pl.when(t == 0)
    def _():
        ucarry[...] = jnp.zeros_like(ucarry)
        if mode == "pool":
            xnbuf[0:POOL_PAD + hist, :] = jnp.zeros((POOL_PAD + hist, xnbuf.shape[1]), F32)
            lv[:, 0:POOL_PAD, :] = jnp.zeros((lv.shape[0], POOL_PAD, lv.shape[2]), F32)

    x = x_ref[...]
    if mode == "pool":
        xn = _rms(x, gp_ref[...])
        xnbuf[POOL_PAD + hist:POOL_PAD + hist + tm, :] = xn

        def cnt_of(w, shape):
            pos = t * tm + lax.broadcasted_iota(jnp.int32, shape, 0)
            return jnp.minimum(pos + 1, w).astype(F32)

        x1 = x + _pool_mix(xn, _prompt_window_sum(xnbuf, lv, tm), cnt_of, wp_ref, ps_ref)
        tail = xnbuf[POOL_PAD + tm:POOL_PAD + tm + hist, :]
        ptail_ref[...] = tail
        xnbuf[POOL_PAD:POOL_PAD + hist, :] = tail
    else:
        x1 = x + jnp.dot(o_ref[...].astype(BF16), wo_ref[...], preferred_element_type=F32)

    xo_ref[...] = _ffn_core(x1, gf_ref, wup_ref, cw_ref, cb_ref, wdown_ref, hbuf, act, fc,
                            _prompt_rows(slab, sbuf, ucarry, tm), _side_tasks(side, side_in, side_out))
    ctail_ref[...] = ucarry[...]


def _sample_layer_kernel(mode, fc, *refs):
    if mode == "pool":
        (x_ref, st_ref, gp_ref, wp_ref, ps_ref, gf_ref, wup_ref, cw_ref, cb_ref, wdown_ref,
         hist_ref, xo_ref, nst_ref, nhist_ref, hbuf, act) = refs
    else:
        (x_ref, o_ref, wo_ref, gf_ref, wup_ref, cw_ref, cb_ref, wdown_ref,
         hist_ref, xo_ref, nhist_ref, hbuf, act) = refs
    x = x_ref[...]
    if mode == "pool":
        xn = _rms(x, gp_ref[...])
        n_hist = st_ref.shape[0]

        def window_sum(w, sl, cur):
            s = cur
            for k in range(1, w):
                s = s + st_ref[n_hist - k, :, sl]
            return s

        def cnt_of(w, shape):
            return float(min(PAST_LEN + 1, w))

        x1 = x + _pool_mix(xn, window_sum, cnt_of, wp_ref, ps_ref)
        for i in range(n_hist - 1):
            nst_ref[i] = st_ref[i + 1]
        nst_ref[n_hist - 1] = xn
    else:
        x1 = x + jnp.dot(o_ref[...].astype(BF16), wo_ref[...], preferred_element_type=F32)
    xo_ref[...] = _ffn_core(x1, gf_ref, wup_ref, cw_ref, cb_ref, wdown_ref, hbuf, act, fc,
                            _sample_rows(hist_ref, nhist_ref))


def _pick_fc(f):
    for fc in (256, 128):
        if f % fc == 0:
            return fc
    return f


def _prompt_layer(mode, x, pre, pre_l, ffn_w, l, tm, side):
    b, s, d = x.shape
    f2, f = ffn_w[1].shape[2], ffn_w[4].shape[1]
    fc = _pick_fc(f)
    row = pl.BlockSpec((None, tm, d), lambda i, j: (i, j, 0))
    hist = 2 * SUBLANES
    ctail_spec = pl.BlockSpec((None, SUBLANES, f2), lambda i, j: (i, 0, 0))
    ctail_shape = jax.ShapeDtypeStruct((b, SUBLANES, f2), F32)
    common_scratch = [pltpu.VMEM((tm, d), BF16), pltpu.VMEM((tm, f), BF16),
                      pltpu.VMEM((d // LANES, tm, LANES), F32),
                      pltpu.VMEM((2, tm // PROMPT_ROW_BLOCKS + SUBLANES, fc), F32),
                      pltpu.VMEM((SUBLANES, f2), F32)]
    if mode == "pool":
        in_specs = [row] + [_layer_spec(a, pre_l) for a in pre]
        args = [x, *pre]
        out_specs = [row, pl.BlockSpec((None, hist, d), lambda i, j: (i, 0, 0)), ctail_spec]
        out_shape = [jax.ShapeDtypeStruct(x.shape, F32), jax.ShapeDtypeStruct((b, hist, d), F32),
                     ctail_shape]
        ext = POOL_PAD + hist + tm
        scratch = [pltpu.VMEM((ext, d), F32),
                   pltpu.VMEM((2, ext, d // len(POOL_WINDOWS)), F32)] + common_scratch
    else:
        o, wo = pre
        in_specs = [row, pl.BlockSpec((None, tm, o.shape[2]), lambda i, j: (i, j, 0)),
                    _layer_spec(wo, pre_l)]
        args = [x, o, wo]
        out_specs = [row, ctail_spec]
        out_shape = [jax.ShapeDtypeStruct(x.shape, F32), ctail_shape]
        scratch = common_scratch
    nj = s // tm
    return _host_call(
        functools.partial(_prompt_layer_kernel, mode, tm, fc), side, lambda i, j: i * nj + j,
        len(in_specs) + len(ffn_w), len(out_specs),
        grid=(b, nj), in_specs=in_specs + [_layer_spec(a, l) for a in ffn_w],
        out_specs=out_specs, out_shape=out_shape, args=args + list(ffn_w),
        scratch_shapes=scratch, name=f"prompt_{mode}_layer")


def _sample_layer(mode, x, pre, pre_l, ffn_w, l, hist_t, bn):
    n, d = x.shape
    f2, f = ffn_w[1].shape[2], ffn_w[4].shape[1]
    fc = _pick_fc(f)
    row = lambda width: pl.BlockSpec((bn, width), lambda i: (i, 0))
    tmaj = lambda a, k: pl.BlockSpec((None, a.shape[1], bn, a.shape[3]), lambda i: (k, 0, i, 0))
    tmaj_out = lambda a: pl.BlockSpec((a.shape[1], bn, a.shape[3]), lambda i: (0, i, 0))
    hist_shape = jax.ShapeDtypeStruct(hist_t.shape[1:], F32)
    if mode == "pool":
        st_t = pre[0]
        in_specs = [row(d), tmaj(st_t, pre_l)] + [_layer_spec(a, pre_l) for a in pre[1:]]
        args = [x, *pre]
        out_specs = [row(d), tmaj_out(st_t), tmaj_out(hist_t)]
        out_shape = [jax.ShapeDtypeStruct(x.shape, F32), jax.ShapeDtypeStruct(st_t.shape[1:], F32),
                     hist_shape]
    else:
        o, wo = pre
        in_specs = [row(d), row(o.shape[1]), _layer_spec(wo, pre_l)]
        args = [x, o, wo]
        out_specs = [row(d), tmaj_out(hist_t)]
        out_shape = [jax.ShapeDtypeStruct(x.shape, F32), hist_shape]
    return pl.pallas_call(
        functools.partial(_sample_layer_kernel, mode, fc),
        grid=(n // bn,),
        in_specs=in_specs + [_layer_spec(a, l) for a in ffn_w] + [tmaj(hist_t, l)],
        out_specs=out_specs, out_shape=out_shape,
        scratch_shapes=[pltpu.VMEM((bn, d), BF16), pltpu.VMEM((bn, f), BF16)],
        compiler_params=_cparams(1), name=f"sample_{mode}_layer",
    )(*args, *ffn_w, hist_t)


HEAD_TILE = 2 * LANES


def _head_norm_rot(x, bd_ref, gain, tabs):
    gcos, sina, sinb = tabs
    ms = jnp.dot((x * x).astype(BF16), bd_ref[...], preferred_element_type=F32)
    outs = []
    for j in range(x.shape[1] // LANES):
        xs = x[:, j * LANES:(j + 1) * LANES]
        xg = xs * gain
        outs.append(xs * gcos + pltpu.roll(xg, LANES - ROT_DIM // 2, 1) * sina
                    + pltpu.roll(xg, ROT_DIM // 2, 1) * sinb)
    return jnp.concatenate(outs, axis=1) * lax.rsqrt(ms + NORM_EPS)


def _proj_kernel(do_kv, kvt_groups, side, *refs):
    n_out = (N_GROUPS + 1 + len(kvt_groups)) if do_kv else 1
    ins, side_in, outs, side_out, _ = _split_side(refs, 14 if do_kv else 8, n_out, side)
    x_ref, bd_ref = ins[0:2]
    if do_kv:
        ktabs, (gkv_ref, wkv_ref, kg_ref) = ins[2:5], ins[5:8]
        qtabs, (gq_ref, wq_ref, qg_ref) = ins[8:11], ins[11:14]
        kv_refs, q_ref, kvt_refs = outs[0:N_GROUPS], outs[N_GROUPS], outs[N_GROUPS + 1:]
    else:
        qtabs, (gq_ref, wq_ref, qg_ref) = ins[2:5], ins[5:8]
        q_ref = outs[0]
    side_tasks = _side_tasks(side, side_in, side_out)
    x = x_ref[...]
    gw = HEADS * HEAD_DIM
    if do_kv:
        h = _rms(x, gkv_ref[...]).astype(BF16)
        kv = jnp.dot(h, wkv_ref[...], preferred_element_type=F32)
        tabs = [t[...] for t in ktabs]
        for g, out_ref in enumerate(kv_refs):
            for j in range(gw // HEAD_TILE):
                c0 = g * gw + j * HEAD_TILE
                out_ref[:, j * HEAD_TILE:(j + 1) * HEAD_TILE] = _head_norm_rot(
                    kv[:, c0:c0 + HEAD_TILE], bd_ref, kg_ref[...], tabs)
            v0 = N_GROUPS * gw + g * gw
            out_ref[:, gw:2 * gw] = kv[:, v0:v0 + gw]
        for g, kvt_ref in zip(kvt_groups, kvt_refs):
            kvt_ref[...] = kv_refs[g][...].T
    h = _rms(x, gq_ref[...]).astype(BF16)
    q = jnp.dot(h, wq_ref[...], preferred_element_type=F32)
    tabs = [t[...] for t in qtabs]
    n_tiles = q.shape[1] // HEAD_TILE
    for j in range(n_tiles):
        cols = slice(j * HEAD_TILE, (j + 1) * HEAD_TILE)
        q_ref[:, cols] = _head_norm_rot(q[:, cols], bd_ref, qg_ref[...], tabs)
        _drain(side_tasks, n_tiles, j)


def _proj(x2d, n_tab_blocks, kv_w, q_w, tm, kvt_groups=(), side=None):
    r, d = x2d.shape
    row = pl.BlockSpec((tm, d), lambda i: (i, 0))
    tab = pl.BlockSpec((tm, LANES), lambda i: (i % n_tab_blocks, 0))
    gw = HEADS * HEAD_DIM
    head = lax.broadcasted_iota(jnp.int32, (HEAD_TILE, HEAD_TILE), 0) // HEAD_DIM
    bd = jnp.where(head == head.T, 1.0 / HEAD_DIM, 0.0).astype(BF16)
    in_specs, args = [row, _const_spec(bd.shape)], [x2d, bd]
    out_specs, out_shape = [], []
    for w in (kv_w, q_w):
        if w is None:
            continue
        tabs, g, wmat, hg, layer = w
        wspec = lambda a, layer=layer: _const_spec(a.shape) if layer is None else _layer_spec(a, layer)
        in_specs += [tab, tab, tab, wspec(g), wspec(wmat), _const_spec((1, LANES))]
        args += [*tabs, g, wmat, hg]
    if kv_w is not None:
        out_specs += [pl.BlockSpec((tm, 2 * gw), lambda i: (i, 0))] * N_GROUPS
        out_shape += [jax.ShapeDtypeStruct((r, 2 * gw), F32)] * N_GROUPS
    n_q = q_w[2].shape[-1]
    out_specs.append(pl.BlockSpec((tm, n_q), lambda i: (i, 0)))
    out_shape.append(jax.ShapeDtypeStruct((r, n_q), F32))
    seq = n_tab_blocks * tm
    if kv_w is not None:
        for _ in kvt_groups:
            out_specs.append(pl.BlockSpec((None, 2 * gw, tm), lambda i: (i // n_tab_blocks, 0, i % n_tab_blocks)))
            out_shape.append(jax.ShapeDtypeStruct((r // seq, 2 * gw, seq), F32))
    return _host_call(
        functools.partial(_proj_kernel, kv_w is not None, tuple(kvt_groups)), side, lambda i: i,
        len(in_specs), len(out_specs),
        grid=(r // tm,), in_specs=in_specs, out_specs=out_specs, out_shape=out_shape, args=args,
        scratch_shapes=[], name="proj_kvq" if kv_w is not None else "proj_q")


def _prompt_attn_kernel(seq, side, *refs):
    ins, side_in, outs, side_out, scratch = _split_side(refs, 3 * N_GROUPS, 1, side)
    q_refs, k_refs, v_refs = ins[0:3], ins[3:6], ins[6:9]
    o_ref = outs[0]
    qb, kb, vb, tmpf, accs, ms, ls = scratch
    blk = ATTN_BLOCK
    lo = lax.broadcasted_iota(jnp.int32, (blk, LANES), 1) < HEAD_DIM
    quarter = seq // 4
    side_tasks = _side_tasks(side, side_in, side_out)
    n_blocks_total = sum(seq // blk for _ in ATTN_GROUPS)
    done_blocks = 0

    def valid_mask(nk):
        qi = lax.broadcasted_iota(jnp.int32, (2 * blk, nk), 0) & (blk - 1)
        kj = lax.broadcasted_iota(jnp.int32, (2 * blk, nk), 1)
        if nk == blk:
            return kj <= qi
        return lax.bitcast_convert_type(kj - qi, jnp.uint32) <= jnp.uint32(blk)

    for g, (win, dil) in enumerate(ATTN_GROUPS):
        assert win // dil == blk and dil in (1, 4, 16) and seq % (blk * dil) == 0
        sub = seq // dil
        nb = sub // blk

        def stage(src, dil=dil):
            if dil == 16:
                for r1 in range(4):
                    tmpf[r1 * quarter:(r1 + 1) * quarter, :] = src[pl.ds(r1, quarter, stride=4), :]

        def rows_of(src, r, n0, cnt, dil=dil):
            if dil == 1:
                return src[pl.ds(n0, cnt), :]
            if dil == 4:
                return src[pl.ds(r + 4 * n0, cnt, stride=4), :]
            return tmpf[pl.ds((r % 4) * quarter + r // 4 + 4 * n0, cnt, stride=4), :]

        stage(q_refs[g])
        for r in range(dil):
            for n in range(nb):
                qf = rows_of(q_refs[g], r, n * blk, blk)
                base = (r * nb + n) * 2 * blk
                qb[base:base + blk, :] = jnp.where(lo, qf, 0.0).astype(BF16)
                qb[base + blk:base + 2 * blk, :] = jnp.where(lo, 0.0, qf).astype(BF16)
        for src, dst in ((k_refs[g], kb), (v_refs[g], vb)):
            stage(src)
            for r in range(dil):
                dst[r * sub:(r + 1) * sub, :] = rows_of(src, r, 0, sub).astype(BF16)

        def block(r, n, first, g=g, dil=dil, sub=sub, nb=nb):
            q2 = qb[pl.ds((r * nb + n) * 2 * blk, 2 * blk), :]
            nk = blk if first else 2 * blk
            k0 = r * sub + (n if first else n - 1) * blk
            kt = kb[pl.ds(k0, nk), :]
            vt = vb[pl.ds(k0, nk), :]
            s = lax.dot_general(q2, kt, (((1,), (1,)), ((), ())), preferred_element_type=F32)
            s = jnp.where(valid_mask(nk), s, NEG_INF)
            m = jnp.max(s, axis=-1, keepdims=True)
            p = jnp.exp(s - m)
            l = jnp.sum(p, axis=-1, keepdims=True)
            pv = jnp.dot(p.astype(BF16), vt, preferred_element_type=F32)
            if dil > 1:
                out_rows = pl.ds(r + n * blk * dil, blk, stride=dil)
            else:
                out_rows = pl.ds(n * blk, blk)
            accs[g, out_rows, :] = jnp.where(lo, pv[:blk], pv[blk:])
            ms[g, out_rows, :] = jnp.where(lo, m[:blk], m[blk:])
            ls[g, out_rows, :] = jnp.where(lo, l[:blk], l[blk:])

        for r in range(dil):
            for n in range(nb):
                block(r, n, n == 0)
                _drain(side_tasks, n_blocks_total, done_blocks)
                done_blocks += 1

    def merge(i, carry):
        rows = pl.ds(pl.multiple_of(i * blk, blk), blk)
        m = jnp.maximum(jnp.maximum(ms[0, rows, :], ms[1, rows, :]), ms[2, rows, :])
        num = jnp.zeros((blk, LANES), F32)
        den = jnp.zeros((blk, LANES), F32)
        for g in range(N_GROUPS):
            w = jnp.exp(ms[g, rows, :] - m)
            num = num + w * accs[g, rows, :]
            den = den + w * ls[g, rows, :]
        o_ref[rows, :] = (num / den).astype(o_ref.dtype)
        return carry

    lax.fori_loop(0, seq // blk, merge, 0)


def _prompt_attn(q, kvs, side):
    b, s, _ = q.shape
    gw = HEADS * HEAD_DIM
    n_hp = gw // LANES
    blk_spec = lambda col: pl.BlockSpec((None, s, LANES), lambda i, j, col=col: (i, 0, col + j))
    in_specs = ([blk_spec(g * n_hp) for g in range(N_GROUPS)]
                + [blk_spec(0)] * N_GROUPS
                + [blk_spec(n_hp)] * N_GROUPS)
    args = [q] * N_GROUPS + list(kvs) + list(kvs)
    (o,), kc = _host_call(
        functools.partial(_prompt_attn_kernel, s), side, lambda i, j: i * n_hp + j, len(in_specs), 1,
        grid=(b, n_hp), in_specs=in_specs,
        out_specs=[pl.BlockSpec((None, s, LANES), lambda i, j: (i, 0, j))],
        out_shape=[jax.ShapeDtypeStruct((b, s, gw), BF16)], args=args,
        scratch_shapes=[pltpu.VMEM((2 * s, LANES), BF16),
                        pltpu.VMEM((s, LANES), BF16),
                        pltpu.VMEM((s, LANES), BF16),
                        pltpu.VMEM((s, LANES), F32)]
                       + [pltpu.VMEM((N_GROUPS, s, LANES), F32)] * 3,
        name="prompt_attn")
    return o, kc


def _compact_kernel(side, *refs):
    _, side_in, _, side_out, _ = _split_side(refs, 0, 0, side)
    for task in _side_tasks(side, side_in, side_out):
        task()


def _compact_rest(side, n_units):
    _, kc = _host_call(_compact_kernel, side, lambda i: i, 0, 0, grid=(n_units,), in_specs=[],
                       out_specs=[], out_shape=[], args=[], scratch_shapes=[], name="compact_caches")
    return kc


def _sample_attn_kernel(bb, q_ref, kv0_ref, kv1_ref, kv2_ref, kc_ref, o_ref):
    gw = HEADS * HEAD_DIM
    head_of_row = lax.broadcasted_iota(jnp.int32, (HEADS, gw), 0)
    head_of_col = lax.broadcasted_iota(jnp.int32, (HEADS, gw), 1) // HEAD_DIM
    diag = head_of_row == head_of_col
    nk = kc_ref.shape[-1] // N_GROUPS
    kvn_refs = (kv0_ref, kv1_ref, kv2_ref)

    def body(b, carry):
        row = pl.ds(b, 1)
        q_bd = [jnp.where(diag, q_ref[row, g * gw:(g + 1) * gw], 0.0) for g in range(N_GROUPS)]
        s_all = jnp.dot(jnp.concatenate(q_bd, axis=0).astype(BF16), kc_ref[b, 0],
                        preferred_element_type=F32)
        s = [s_all[g * HEADS:(g + 1) * HEADS, g * nk:(g + 1) * nk] for g in range(N_GROUPS)]
        s_new = [jnp.sum(q_bd[g] * kvn_refs[g][row, 0:gw], axis=-1, keepdims=True)
                 for g in range(N_GROUPS)]
        m = functools.reduce(jnp.maximum, [jnp.max(a, axis=-1, keepdims=True) for a in s] + s_new)
        p = [jnp.exp(a - m) for a in s]
        p_new = [jnp.exp(a - m) for a in s_new]
        l = sum(jnp.sum(a, axis=-1, keepdims=True) for a in p) + sum(p_new)
        acc = lax.dot_general(jnp.concatenate(p, axis=1).astype(BF16), kc_ref[b, 1],
                              (((1,), (1,)), ((), ())), preferred_element_type=F32)
        for g in range(N_GROUPS):
            acc = acc + p_new[g] * kvn_refs[g][row, gw:2 * gw]
        o_ref[row, :] = jnp.sum(jnp.where(diag, acc / l, 0.0), axis=0, keepdims=True)
        return carry

    lax.fori_loop(0, bb, body, 0, unroll=4)


def _sample_attn(q, kvn, kc, bb):
    n = q.shape[0]
    gw = HEADS * HEAD_DIM
    rows = lambda width: pl.BlockSpec((bb, width), lambda i: (i, 0))
    return pl.pallas_call(
        functools.partial(_sample_attn_kernel, bb),
        grid=(n // bb,),
        in_specs=[rows(q.shape[1])] + [rows(2 * gw)] * N_GROUPS
                 + [pl.BlockSpec((bb,) + kc.shape[1:], lambda i: (i, 0, 0, 0))],
        out_specs=rows(gw),
        out_shape=jax.ShapeDtypeStruct((n, gw), F32),
        compiler_params=_cparams(1), name="sample_attn",
    )(q, *kvn, kc)


def _rot_tables(pos):
    half = ROT_DIM // 2
    lane = lax.broadcasted_iota(jnp.int32, (pos.shape[0], LANES), 1) % HEAD_DIM
    inv_freq = ROPE_THETA ** (-(2 * (lane % half)).astype(F32) / ROT_DIM)
    ang = pos.astype(F32)[:, None] * inv_freq
    cos, sin = jnp.cos(ang), jnp.sin(ang)
    cosb = jnp.where(lane < ROT_DIM, cos, 1.0)
    sina = jnp.where(lane < half, -sin, 0.0)
    sinb = jnp.where((lane >= half) & (lane < ROT_DIM), sin, 0.0)
    return cosb, sina, sinb


def kernel(x_prompt, x_sample, state_pool, state_conv, cache_kv_w128, cache_kv_w512, cache_kv_w2048,
           g_pool, w_pool, pool_scale, g_ffn, w_up, conv_w, conv_b, w_down, g_kv, w_kv, k_gain,
           g_attn, w_q, q_gain, w_o):
    b, s, d = x_prompt.shape
    n = x_sample.shape[0]
    depth = w_up.shape[0]
    n_pool = g_pool.shape[0]
    gw = HEADS * HEAD_DIM
    rows3 = lambda a: a.reshape(a.shape[0], 1, a.shape[1])
    lane2 = lambda a: jnp.tile(a, LANES // HEAD_DIM).reshape(1, LANES)
    tm = min(512, s)
    bn = min(64, n)
    bb = min(8, n)

    ffn_w = (rows3(g_ffn), w_up.astype(BF16), conv_w, rows3(conv_b), w_down.astype(BF16))
    pool_w = (rows3(g_pool), w_pool.astype(BF16), rows3(pool_scale))
    wkv_b, wq_b, wo_b = w_kv.astype(BF16), w_q.astype(BF16), w_o.astype(BF16)
    q_scale = HEAD_DIM ** -0.5

    def proj_w(pos):
        cosb, sina, sinb = _rot_tables(pos)
        kg = lane2(k_gain)
        kv_w = ((cosb * kg, sina, sinb), g_kv.reshape(1, d), wkv_b, kg, None)
        q_w = []
        for j in range(depth - n_pool):
            qg = lane2(q_gain[j])
            q_w.append(((cosb * (qg * q_scale), sina * q_scale, sinb * q_scale), rows3(g_attn), wq_b, qg, j))
        return kv_w, q_w

    kv_w_p, q_w_p = proj_w(jnp.arange(s, dtype=jnp.int32))
    kv_w_s, q_w_s = proj_w(jnp.full((n,), PAST_LEN, jnp.int32))

    c_args, sels = _side_plan((cache_kv_w128, cache_kv_w512, cache_kv_w2048))
    units = {"done": 0, "kc": None}

    def side_for(steps, per_step=1, use_mxu=False):
        if units["done"] + steps * per_step > 2 * n:
            return None
        side = _Side(c_args, sels, units["done"], units["kc"], per_step, use_mxu)
        units["done"] += steps * per_step
        return side

    def took(kc):
        if kc is not None:
            units["kc"] = kc

    xp = x_prompt
    pool_p, conv_p = [], []
    big = max(range(N_GROUPS), key=lambda g: ATTN_GROUPS[g][0])
    n_steps = b * (s // tm)
    kvs_p = kvt_p = None
    for l in range(depth):
        if l < n_pool:
            (xp, ptail, ctail), kc = _prompt_layer("pool", xp, pool_w, l, ffn_w, l, tm, side_for(n_steps, 1))
            took(kc)
            pool_p.append(ptail[:, 2 * SUBLANES - POOL_HIST:])
        else:
            j = l - n_pool
            x2d = xp.reshape(b * s, d)
            if j == 0:
                (*kvs_p, q_p, kvt_p), kc = _proj(x2d, s // tm, kv_w_p, q_w_p[j], tm, (big,), None)
                kvs_p = [a.reshape(b, s, 2 * gw) for a in kvs_p]
            else:
                (q_p,), kc = _proj(x2d, s // tm, None, q_w_p[j], tm, (), side_for(n_steps, 2, True))
            took(kc)
            o_p, kc = _prompt_attn(q_p.reshape(b, s, N_GROUPS * gw), kvs_p, None)
            took(kc)
            (xp, ctail), kc = _prompt_layer("attn", xp, (o_p, wo_b), j, ffn_w, l, tm, side_for(n_steps, 2))
            took(kc)
        conv_p.append(ctail[:, SUBLANES - (CONV_W - 1):])
    if units["done"] < 2 * n:
        took(_compact_rest(_Side(c_args, sels, units["done"], units["kc"], 1, True), 2 * n - units["done"]))
    kc = units["kc"]

    xs = x_sample.reshape(n, d)
    pool_s, conv_s = [], []
    hist_t = jnp.swapaxes(state_conv, 1, 2)
    st_t = jnp.swapaxes(state_pool, 1, 2)
    kvs_s = None
    for l in range(depth):
        if l < n_pool:
            xs, nst, nhist = _sample_layer("pool", xs, (st_t,) + pool_w, l, ffn_w, l, hist_t, bn)
            pool_s.append(jnp.swapaxes(nst, 0, 1))
        else:
            j = l - n_pool
            if j == 0:
                (*kvs_s, q_s), _ = _proj(xs, 1, kv_w_s, q_w_s[j], n)
            else:
                (q_s,), _ = _proj(xs, 1, None, q_w_s[j], n)
            o_s = _sample_attn(q_s, kvs_s, kc, bb)
            xs, nhist = _sample_layer("attn", xs, (o_s, wo_b), j, ffn_w, l, hist_t, bn)
        conv_s.append(jnp.swapaxes(nhist, 0, 1))

    outs = [xp, xs.reshape(n, 1, d), jnp.stack(pool_p), jnp.stack(pool_s),
            jnp.stack(conv_p), jnp.stack(conv_s)]
    for g, (win, _) in enumerate(ATTN_GROUPS):
        if g == big and win >= s:
            rows_p = jnp.transpose(kvt_p.reshape(b, 2, HEADS, HEAD_DIM, s), (0, 4, 1, 2, 3))
        else:
            rows_p = kvs_p[g][:, s - min(win, s):].reshape(b, min(win, s), 2, HEADS, HEAD_DIM)
        outs += [rows_p, kvs_s[g].reshape(n, 1, 2, HEADS, HEAD_DIM)]
    return tuple(outs)
```

```python
import collections
import functools

import jax
import jax.numpy as jnp
from jax import lax
from jax.experimental import pallas as pl
from jax.experimental.pallas import tpu as pltpu

F32 = jnp.float32
BF16 = jnp.bfloat16

NORM_EPS = 1e-6
NEG_INF = -1e30
POOL_WINDOWS = (2, 4, 8, 16)
POOL_HIST = max(POOL_WINDOWS) - 1
CONV_W = 3
ATTN_GROUPS = ((128, 1), (512, 4), (2048, 16))
N_GROUPS = len(ATTN_GROUPS)
HEADS = 8
HEAD_DIM = 64
ROT_DIM = HEAD_DIM // 4
ROPE_THETA = 500000.0
ATTN_BLOCK = 128
PAST_LEN = 2048

LANES = 128
SUBLANES = 8
VMEM_LIMIT = 62 * 1024 * 1024


def _cparams(n_grid):
    return pltpu.CompilerParams(dimension_semantics=("arbitrary",) * n_grid,
                                vmem_limit_bytes=VMEM_LIMIT)


def _const_spec(shape):
    nd = len(shape)
    return pl.BlockSpec(shape, lambda *_: (0,) * nd, pipeline_mode=pl.Buffered(1))


def _layer_spec(stacked, l):
    if l is None:
        return _const_spec(stacked.shape)
    nd = stacked.ndim - 1
    return pl.BlockSpec((None,) + stacked.shape[1:], lambda *_: (l,) + (0,) * nd,
                        pipeline_mode=pl.Buffered(1))


def _rms(x, g):
    ms = jnp.mean(x * x, axis=-1, keepdims=True)
    return x * lax.rsqrt(ms + NORM_EPS) * g


def _silu(a):
    return a / (1.0 + jnp.exp(-a))


_Side = collections.namedtuple("_Side", "c_args sels base kc per_step use_mxu")

COMPACT_ROWS = 64


def _compact_tasks(c_refs, sel_refs, o_ref, use_mxu):
    nk = c_refs[0].shape[-1]
    n_rows = c_refs[0].shape[1]
    assert nk == LANES and n_rows % COMPACT_ROWS == 0
    tasks = []

    def cast_task(u):
        def run():
            o_ref[u, :, 0:nk] = c_refs[0][u].astype(BF16)
        return run

    def chunk_task(u, g, r0):
        rows, out_cols = slice(r0, r0 + COMPACT_ROWS), slice(g * nk, (g + 1) * nk)
        dil = c_refs[g].shape[-1] // nk

        def run():
            if use_mxu:
                o_ref[u, rows, out_cols] = jnp.dot(c_refs[g][u, rows, :].astype(BF16), sel_refs[g - 1][...],
                                                   preferred_element_type=F32).astype(BF16)
                return
            keep = lax.broadcasted_iota(jnp.int32, (COMPACT_ROWS, LANES), 1) % dil == 0
            acc = jnp.where(keep, c_refs[g][u, rows, 0:LANES], 0.0)
            for j in range(1, dil):
                tile = jnp.where(keep, c_refs[g][u, rows, j * LANES:(j + 1) * LANES], 0.0)
                acc = acc + pltpu.roll(tile, j, 1)
            o_ref[u, rows, out_cols] = acc.astype(BF16)
        return run

    for u in range(o_ref.shape[0]):
        tasks.append(cast_task(u))
        for g in range(1, len(c_refs)):
            tasks += [chunk_task(u, g, r0) for r0 in range(0, n_rows, COMPACT_ROWS)]
    return tasks


def _drain(tasks, n_slots, i):
    for task in tasks[len(tasks) * i // n_slots:len(tasks) * (i + 1) // n_slots]:
        task()


def _side_plan(caches):
    n = caches[0].shape[0]
    gw = HEADS * HEAD_DIM
    c_args, sels = [], []
    for c, (win, dil) in zip(caches, ATTN_GROUPS):
        n_buf, n_back = c.shape[1], win // dil
        assert n_buf == n_back * dil, "window buffer must hold exactly n_back dilated rows"
        c_args.append(jnp.transpose(c, (0, 2, 3, 4, 1)).reshape(n, 2, gw, n_buf))
        if dil > 1:
            t = lax.broadcasted_iota(jnp.int32, (n_buf, n_back), 0)
            m = lax.broadcasted_iota(jnp.int32, (n_buf, n_back), 1)
            sels.append((t == dil * m).astype(BF16))
    return c_args, sels


def _side_io(side, step_of):
    gw = HEADS * HEAD_DIM
    n = side.c_args[0].shape[0]
    n_keys = N_GROUPS * (ATTN_GROUPS[0][0] // ATTN_GROUPS[0][1])
    assert side.per_step in (1, 2) and side.base % side.per_step == 0
    unit = lambda *idx: side.base + side.per_step * step_of(*idx)
    blk_idx = lambda *idx: (unit(*idx) // 2, (unit(*idx) % 2) // side.per_step, 0, 0)
    c_specs = [pl.BlockSpec((None, side.per_step, gw, c.shape[3]), blk_idx) for c in side.c_args]
    sels = side.sels if side.use_mxu else []
    in_specs = c_specs + [_const_spec(s_.shape) for s_ in sels]
    args = list(side.c_args) + list(sels)
    if side.kc is not None:
        in_specs.append(pl.BlockSpec(memory_space=pl.ANY))
        args.append(side.kc)
    out_spec = pl.BlockSpec((None, side.per_step, gw, n_keys), blk_idx)
    out_shape = jax.ShapeDtypeStruct((n, 2, gw, n_keys), BF16)
    return in_specs, args, out_spec, out_shape


def _split_side(refs, n_in, n_out, side):
    ins, rest = refs[:n_in], refs[n_in:]
    side_in = ()
    if side is not None:
        k = N_GROUPS + (len(side.sels) if side.use_mxu else 0) + (side.kc is not None)
        side_in, rest = rest[:k], rest[k:]
    outs, rest = rest[:n_out], rest[n_out:]
    side_out = None
    if side is not None:
        side_out, rest = rest[0], rest[1:]
    return ins, side_in, outs, side_out, rest


def _side_tasks(side, side_in, side_out):
    if side is None:
        return []
    n_sel = len(side.sels) if side.use_mxu else 0
    return _compact_tasks(side_in[:N_GROUPS], side_in[N_GROUPS:N_GROUPS + n_sel], side_out, side.use_mxu)


def _host_call(kernel, side, step_of, n_in, n_out, *, grid, in_specs, out_specs, out_shape, args,
               scratch_shapes, name):
    aliases = {}
    if side is not None:
        s_in, s_args, s_out, s_shape = _side_io(side, step_of)
        if side.kc is not None:
            aliases = {len(in_specs) + len(s_in) - 1: len(out_specs)}
        in_specs, args = in_specs + s_in, args + s_args
        out_specs, out_shape = out_specs + [s_out], out_shape + [s_shape]
    assert len(in_specs) == n_in + (0 if side is None else len(s_in))
    res = pl.pallas_call(
        functools.partial(kernel, side), grid=grid, in_specs=in_specs, out_specs=out_specs,
        out_shape=out_shape, scratch_shapes=scratch_shapes, input_output_aliases=aliases,
        compiler_params=_cparams(len(grid)), name=name,
    )(*args)
    return (res[:n_out], res[n_out]) if side is not None else (res, None)


def _ffn_core(x1, gf_ref, wup_ref, cw_ref, cb_ref, wdown_ref, hbuf, act, fc, rows, side_tasks=()):
    n_blocks, to_rows, prev_rows, from_rows = rows
    d_ff = wdown_ref.shape[0]
    q = hbuf.shape[0] // n_blocks
    blocks = [slice(j * q, (j + 1) * q) for j in range(n_blocks)]
    h = _rms(x1, gf_ref[...])
    for j, rb in enumerate(blocks):
        hbuf[rb, :] = to_rows(h, j).astype(BF16)
    for c in range(d_ff // fc):
        halves = []
        for base in (0, d_ff):
            cols = slice(base + c * fc, base + (c + 1) * fc)
            u = jnp.dot(hbuf[...], wup_ref[:, cols], preferred_element_type=F32)
            ub = [u[rb] for rb in blocks]
            prev = prev_rows(cols, ub)
            halves.append([cb_ref[:, cols] + ((cw_ref[0:1, cols] * u2 + cw_ref[1:2, cols] * u1)
                                              + cw_ref[2:3, cols] * u0)
                           for u0, (u1, u2) in zip(ub, prev)])
        for rb, a, b in zip(blocks, *halves):
            act[rb, c * fc:(c + 1) * fc] = (_silu(a) * b).astype(BF16)
        _drain(side_tasks, d_ff // fc, c)
    return x1 + from_rows(jnp.dot(act[...], wdown_ref[...], preferred_element_type=F32))


PROMPT_ROW_BLOCKS = 4


def _prompt_rows(slab, sbuf, ucarry, tm):
    nb = PROMPT_ROW_BLOCKS
    q = tm // nb
    n_slab = slab.shape[0]

    def to_rows(h, j):
        if j == 0:
            for k in range(n_slab):
                slab[k] = h[:, k * LANES:(k + 1) * LANES]
        return jnp.concatenate([slab[k, pl.ds(j, q, stride=nb), :] for k in range(n_slab)], axis=1)

    def shift_down(slot, blk, carry_row):
        sbuf[slot, SUBLANES - 1:SUBLANES, :] = carry_row
        sbuf[slot, SUBLANES:SUBLANES + q, :] = blk
        return sbuf[slot, SUBLANES - 1:SUBLANES - 1 + q, :]

    def prev_rows(cols, ub):
        s3 = shift_down(0, ub[3], ucarry[SUBLANES - 1:SUBLANES, cols])
        s2 = shift_down(1, ub[2], ucarry[SUBLANES - 2:SUBLANES - 1, cols])
        ucarry[SUBLANES - 1:SUBLANES, cols] = ub[3][q - 1:q]
        ucarry[SUBLANES - 2:SUBLANES - 1, cols] = ub[2][q - 1:q]
        return [(s3, s2), (ub[0], s3), (ub[1], ub[0]), (ub[2], ub[1])]

    def from_rows(y):
        for j in range(nb):
            for k in range(n_slab):
                slab[k, pl.ds(j, q, stride=nb), :] = y[j * q:(j + 1) * q, k * LANES:(k + 1) * LANES]
        return jnp.concatenate([slab[k] for k in range(n_slab)], axis=1)

    return nb, to_rows, prev_rows, from_rows


def _sample_rows(hist_ref, newhist_ref):
    def prev_rows(cols, ub):
        newhist_ref[0, :, cols] = hist_ref[1, :, cols]
        newhist_ref[1, :, cols] = ub[0]
        return [(hist_ref[1, :, cols], hist_ref[0, :, cols])]
    return 1, (lambda h, j: h), prev_rows, (lambda y: y)


def _pool_mix(xn, window_sum, cnt_of, wp_ref, ps_ref):
    gw = xn.shape[1] // len(POOL_WINDOWS)
    parts = []
    for g, w in enumerate(POOL_WINDOWS):
        sl = slice(g * gw, (g + 1) * gw)
        cur = xn[:, sl]
        d = window_sum(w, sl, cur) / cnt_of(w, cur.shape) - cur
        parts.append(jnp.dot(d.astype(BF16), wp_ref[g], preferred_element_type=F32))
    return jnp.concatenate(parts, axis=-1) * ps_ref[...]


POOL_PAD = 2 * SUBLANES


def _prompt_window_sum(xnbuf, lv, tm):
    hist = 2 * SUBLANES
    span = hist + tm

    def window_sum(w, sl, cur):
        s = xnbuf[POOL_PAD:POOL_PAD + span, sl] + xnbuf[POOL_PAD - 1:POOL_PAD - 1 + span, sl]
        shift, slot = 2, 0
        while shift < w:
            lv[slot, POOL_PAD:POOL_PAD + span, :] = s
            s = lv[slot, POOL_PAD:POOL_PAD + span, :] + lv[slot, POOL_PAD - shift:POOL_PAD - shift + span, :]
            shift, slot = 2 * shift, 1 - slot
        return s[hist:]
    return window_sum


def _prompt_layer_kernel(mode, tm, fc, n_cast, side, *refs):
    ins, side_in, outs, side_out, scratch = _split_side(refs, (9 if mode == "pool" else 8) + n_cast,
                                                        (3 if mode == "pool" else 2) + n_cast, side)
    tasks = _side_tasks(side, side_in, side_out)
    if n_cast:
        (ins, cast_in), (outs, cast_out) = (ins[:-n_cast], ins[-n_cast:]), (outs[:-n_cast], outs[-n_cast:])

        def cast_task(src, dst):
            def run():
                dst[...] = src[...].astype(BF16)
            return run
        tasks = tasks + [cast_task(s_, d_) for s_, d_ in zip(cast_in, cast_out)]
    if mode == "pool":
        x_ref, gp_ref, wp_ref, ps_ref, gf_ref, wup_ref, cw_ref, cb_ref, wdown_ref = ins
        xo_ref, ptail_ref, ctail_ref = outs
        xnbuf, lv, hbuf, act, slab, sbuf, ucarry = scratch
    else:
        x_ref, o_ref, wo_ref, gf_ref, wup_ref, cw_ref, cb_ref, wdown_ref = ins
        xo_ref, ctail_ref = outs
        hbuf, act, slab, sbuf, ucarry = scratch
    t = pl.program_id(1)
    hist = 2 * SUBLANES

    @pl.when(t == 0)
    def _():
        ucarry[...] = jnp.zeros_like(ucarry)
        if mode == "pool":
            xnbuf[0:POOL_PAD + hist, :] = jnp.zeros((POOL_PAD + hist, xnbuf.shape[1]), F32)
            lv[:, 0:POOL_PAD, :] = jnp.zeros((lv.shape[0], POOL_PAD, lv.shape[2]), F32)

    x = x_ref[...]
    if mode == "pool":
        xn = _rms(x, gp_ref[...])
        xnbuf[POOL_PAD + hist:POOL_PAD + hist + tm, :] = xn

        def cnt_of(w, shape):
            pos = t * tm + lax.broadcasted_iota(jnp.int32, shape, 0)
            return jnp.minimum(pos + 1, w).astype(F32)

        x1 = x + _pool_mix(xn, _prompt_window_sum(xnbuf, lv, tm), cnt_of, wp_ref, ps_ref)
        tail = xnbuf[POOL_PAD + tm:POOL_PAD + tm + hist, :]
        ptail_ref[...] = tail
        xnbuf[POOL_PAD:POOL_PAD + hist, :] = tail
    else:
        x1 = x + jnp.dot(o_ref[...].astype(BF16), wo_ref[...], preferred_element_type=F32)

    xo_ref[...] = _ffn_core(x1, gf_ref, wup_ref, cw_ref, cb_ref, wdown_ref, hbuf, act, fc,
                            _prompt_rows(slab, sbuf, ucarry, tm), tasks)
    ctail_ref[...] = ucarry[...]


def _sample_layer_kernel(mode, fc, n_chained, *refs):
    n_in = 11 if mode == "pool" else 9
    refs = refs[:n_in] + refs[n_in + n_chained:]
    if mode == "pool":
        (x_ref, st_ref, gp_ref, wp_ref, ps_ref, gf_ref, wup_ref, cw_ref, cb_ref, wdown_ref,
         hist_ref, xo_ref, nst_ref, nhist_ref, hbuf, act) = refs
    else:
        (x_ref, o_ref, wo_ref, gf_ref, wup_ref, cw_ref, cb_ref, wdown_ref,
         hist_ref, xo_ref, nhist_ref, hbuf, act) = refs
    x = x_ref[...]
    if mode == "pool":
        xn = _rms(x, gp_ref[...])
        n_hist = st_ref.shape[0]

        def window_sum(w, sl, cur):
            s = cur
            for k in range(1, w):
                s = s + st_ref[n_hist - k, :, sl]
            return s

        def cnt_of(w, shape):
            return float(min(PAST_LEN + 1, w))

        x1 = x + _pool_mix(xn, window_sum, cnt_of, wp_ref, ps_ref)
        for i in range(n_hist - 1):
            nst_ref[i] = st_ref[i + 1]
        nst_ref[n_hist - 1] = xn
    else:
        x1 = x + jnp.dot(o_ref[...].astype(BF16), wo_ref[...], preferred_element_type=F32)
    xo_ref[...] = _ffn_core(x1, gf_ref, wup_ref, cw_ref, cb_ref, wdown_ref, hbuf, act, fc,
                            _sample_rows(hist_ref, nhist_ref))


def _pick_fc(f):
    for fc in (256, 128):
        if f % fc == 0:
            return fc
    return f


def _cast_rows(d, f, steps):
    if steps % 2 or d % steps or f % (steps // 2):
        return None
    ru, rd = d // steps, f // (steps // 2)
    return (ru, rd) if ru % (2 * SUBLANES) == 0 and rd % (2 * SUBLANES) == 0 else None


def _prompt_layer(mode, x, pre, pre_l, ffn_w, tm, side, cast=None):
    b, s, d = x.shape
    f2, f = ffn_w[1][0].shape[-1], ffn_w[4][0].shape[-2]
    fc = _pick_fc(f)
    row = pl.BlockSpec((None, tm, d), lambda i, j: (i, j, 0))
    hist = 2 * SUBLANES
    ctail_spec = pl.BlockSpec((None, SUBLANES, f2), lambda i, j: (i, 0, 0))
    ctail_shape = jax.ShapeDtypeStruct((b, SUBLANES, f2), F32)
    common_scratch = [pltpu.VMEM((tm, d), BF16), pltpu.VMEM((tm, f), BF16),
                      pltpu.VMEM((d // LANES, tm, LANES), F32),
                      pltpu.VMEM((2, tm // PROMPT_ROW_BLOCKS + SUBLANES, fc), F32),
                      pltpu.VMEM((SUBLANES, f2), F32)]
    if mode == "pool":
        in_specs = [row] + [_layer_spec(a, pre_l) for a in pre]
        args = [x, *pre]
        out_specs = [row, pl.BlockSpec((None, hist, d), lambda i, j: (i, 0, 0)), ctail_spec]
        out_shape = [jax.ShapeDtypeStruct(x.shape, F32), jax.ShapeDtypeStruct((b, hist, d), F32),
                     ctail_shape]
        ext = POOL_PAD + hist + tm
        scratch = [pltpu.VMEM((ext, d), F32),
                   pltpu.VMEM((2, ext, d // len(POOL_WINDOWS)), F32)] + common_scratch
    else:
        o, wo = pre
        in_specs = [row, pl.BlockSpec((None, tm, o.shape[2]), lambda i, j: (i, j, 0)),
                    _layer_spec(wo, pre_l)]
        args = [x, o, wo]
        out_specs = [row, ctail_spec]
        out_shape = [jax.ShapeDtypeStruct(x.shape, F32), ctail_shape]
        scratch = common_scratch
    nj = s // tm
    step_of = lambda i, j: i * nj + j
    in_specs = in_specs + [_layer_spec(a, l) for a, l in ffn_w]
    args = args + [a for a, _ in ffn_w]
    n_cast = 0
    if cast is not None:
        weights, nl = cast
        for w_f, (rows, span) in zip(weights, zip(_cast_rows(d, f, b * nj), (1, 2))):
            blk = lambda i, j, span=span: step_of(i, j) // span
            in_specs.append(pl.BlockSpec((None, rows, w_f.shape[2]), lambda i, j, blk=blk: (nl, blk(i, j), 0)))
            args.append(w_f)
            out_specs = out_specs + [pl.BlockSpec((rows, w_f.shape[2]), lambda i, j, blk=blk: (blk(i, j), 0))]
            out_shape = out_shape + [jax.ShapeDtypeStruct(w_f.shape[1:], BF16)]
            n_cast += 1
    return _host_call(
        functools.partial(_prompt_layer_kernel, mode, tm, fc, n_cast), side, step_of,
        len(in_specs), len(out_specs),
        grid=(b, nj), in_specs=in_specs, out_specs=out_specs, out_shape=out_shape, args=args,
        scratch_shapes=scratch, name=f"prompt_{mode}_layer")


def _sample_layer(mode, x, pre, pre_l, ffn_w, l, hist_t, bn, chained):
    n, d = x.shape
    f2, f = ffn_w[1][0].shape[-1], ffn_w[4][0].shape[-2]
    fc = _pick_fc(f)
    row = lambda width: pl.BlockSpec((bn, width), lambda i: (i, 0))
    tmaj = lambda a, k: pl.BlockSpec((None, a.shape[1], bn, a.shape[3]), lambda i: (k, 0, i, 0))
    if mode == "pool":
        st_t = pre[0]
        in_specs = [row(d), tmaj(st_t, pre_l)] + [_layer_spec(a, pre_l) for a in pre[1:]]
        args = [x, *pre]
        states = [(st_t, pre_l), (hist_t, l)]
    else:
        o, wo = pre
        in_specs = [row(d), row(o.shape[1]), _layer_spec(wo, pre_l)]
        args = [x, o, wo]
        states = [(hist_t, l)]
    in_specs = in_specs + [_layer_spec(a, k) for a, k in ffn_w] + [tmaj(hist_t, l)]
    args = args + [a for a, _ in ffn_w] + [hist_t]
    out_specs = [row(d)] + [tmaj(a, k) for a, k in states]
    out_shape = [jax.ShapeDtypeStruct(x.shape, F32)] + [jax.ShapeDtypeStruct(a.shape, F32) for a, _ in states]
    aliases = {}
    for k, prev in enumerate(chained):
        if prev is not None:
            aliases[len(in_specs)] = 1 + k
            in_specs.append(pl.BlockSpec(memory_space=pl.ANY))
            args.append(prev)
    return pl.pallas_call(
        functools.partial(_sample_layer_kernel, mode, fc, len(aliases)),
        grid=(n // bn,), in_specs=in_specs, out_specs=out_specs, out_shape=out_shape,
        scratch_shapes=[pltpu.VMEM((bn, d), BF16), pltpu.VMEM((bn, f), BF16)],
        input_output_aliases=aliases,
        compiler_params=_cparams(1), name=f"sample_{mode}_layer",
    )(*args)


HEAD_TILE = 2 * LANES


def _head_norm_rot(x, bd_ref, gain, tabs):
    gcos, sina, sinb = tabs
    ms = jnp.dot((x * x).astype(BF16), bd_ref[...], preferred_element_type=F32)
    outs = []
    for j in range(x.shape[1] // LANES):
        xs = x[:, j * LANES:(j + 1) * LANES]
        xg = xs * gain
        outs.append(xs * gcos + pltpu.roll(xg, LANES - ROT_DIM // 2, 1) * sina
                    + pltpu.roll(xg, ROT_DIM // 2, 1) * sinb)
    return jnp.concatenate(outs, axis=1) * lax.rsqrt(ms + NORM_EPS)


def _proj_kernel(do_kv, kvt_groups, side, *refs):
    n_out = (N_GROUPS + 1 + len(kvt_groups)) if do_kv else 1
    ins, side_in, outs, side_out, _ = _split_side(refs, 14 if do_kv else 8, n_out, side)
    x_ref, bd_ref = ins[0:2]
    if do_kv:
        ktabs, (gkv_ref, wkv_ref, kg_ref) = ins[2:5], ins[5:8]
        qtabs, (gq_ref, wq_ref, qg_ref) = ins[8:11], ins[11:14]
        kv_refs, q_ref, kvt_refs = outs[0:N_GROUPS], outs[N_GROUPS], outs[N_GROUPS + 1:]
    else:
        qtabs, (gq_ref, wq_ref, qg_ref) = ins[2:5], ins[5:8]
        q_ref = outs[0]
    side_tasks = _side_tasks(side, side_in, side_out)
    x = x_ref[...]
    gw = HEADS * HEAD_DIM
    if do_kv:
        h = _rms(x, gkv_ref[...]).astype(BF16)
        kv = jnp.dot(h, wkv_ref[...], preferred_element_type=F32)
        tabs = [t[...] for t in ktabs]
        for g, out_ref in enumerate(kv_refs):
            for j in range(gw // HEAD_TILE):
                c0 = g * gw + j * HEAD_TILE
                out_ref[:, j * HEAD_TILE:(j + 1) * HEAD_TILE] = _head_norm_rot(
                    kv[:, c0:c0 + HEAD_TILE], bd_ref, kg_ref[...], tabs)
            v0 = N_GROUPS * gw + g * gw
            out_ref[:, gw:2 * gw] = kv[:, v0:v0 + gw]
        for g, kvt_ref in zip(kvt_groups, kvt_refs):
            kvt_ref[...] = kv_refs[g][...].T
    h = _rms(x, gq_ref[...]).astype(BF16)
    q = jnp.dot(h, wq_ref[...], preferred_element_type=F32)
    tabs = [t[...] for t in qtabs]
    n_tiles = q.shape[1] // HEAD_TILE
    for j in range(n_tiles):
        cols = slice(j * HEAD_TILE, (j + 1) * HEAD_TILE)
        q_ref[:, cols] = _head_norm_rot(q[:, cols], bd_ref, qg_ref[...], tabs)
        _drain(side_tasks, n_tiles, j)


def _proj(x2d, n_tab_blocks, kv_w, q_w, tm, kvt_groups=(), side=None):
    r, d = x2d.shape
    row = pl.BlockSpec((tm, d), lambda i: (i, 0))
    tab = pl.BlockSpec((tm, LANES), lambda i: (i % n_tab_blocks, 0))
    gw = HEADS * HEAD_DIM
    head = lax.broadcasted_iota(jnp.int32, (HEAD_TILE, HEAD_TILE), 0) // HEAD_DIM
    bd = jnp.where(head == head.T, 1.0 / HEAD_DIM, 0.0).astype(BF16)
    in_specs, args = [row, _const_spec(bd.shape)], [x2d, bd]
    out_specs, out_shape = [], []
    for w in (kv_w, q_w):
        if w is None:
            continue
        tabs, g, wmat, hg, layer = w
        wspec = lambda a, layer=layer: _const_spec(a.shape) if layer is None else _layer_spec(a, layer)
        in_specs += [tab, tab, tab, wspec(g), wspec(wmat), _const_spec((1, LANES))]
        args += [*tabs, g, wmat, hg]
    if kv_w is not None:
        out_specs += [pl.BlockSpec((tm, 2 * gw), lambda i: (i, 0))] * N_GROUPS
        out_shape += [jax.ShapeDtypeStruct((r, 2 * gw), F32)] * N_GROUPS
    n_q = q_w[2].shape[-1]
    out_specs.append(pl.BlockSpec((tm, n_q), lambda i: (i, 0)))
    out_shape.append(jax.ShapeDtypeStruct((r, n_q), F32))
    seq = n_tab_blocks * tm
    if kv_w is not None:
        for _ in kvt_groups:
            out_specs.append(pl.BlockSpec((None, 2 * gw, tm), lambda i: (i // n_tab_blocks, 0, i % n_tab_blocks)))
            out_shape.append(jax.ShapeDtypeStruct((r // seq, 2 * gw, seq), F32))
    return _host_call(
        functools.partial(_proj_kernel, kv_w is not None, tuple(kvt_groups)), side, lambda i: i,
        len(in_specs), len(out_specs),
        grid=(r // tm,), in_specs=in_specs, out_specs=out_specs, out_shape=out_shape, args=args,
        scratch_shapes=[], name="proj_kvq" if kv_w is not None else "proj_q")


def _prompt_attn_kernel(seq, side, *refs):
    ins, side_in, outs, side_out, scratch = _split_side(refs, 3 * N_GROUPS, 1, side)
    q_refs, k_refs, v_refs = ins[0:3], ins[3:6], ins[6:9]
    o_ref = outs[0]
    qb, kb, vb, tmpf, accs, ms, ls = scratch
    blk = ATTN_BLOCK
    lo = lax.broadcasted_iota(jnp.int32, (blk, LANES), 1) < HEAD_DIM
    quarter = seq // 4
    side_tasks = _side_tasks(side, side_in, side_out)
    n_blocks_total = sum(seq // blk for _ in ATTN_GROUPS)
    done_blocks = 0

    def valid_mask(nk):
        qi = lax.broadcasted_iota(jnp.int32, (2 * blk, nk), 0) & (blk - 1)
        kj = lax.broadcasted_iota(jnp.int32, (2 * blk, nk), 1)
        if nk == blk:
            return kj <= qi
        return lax.bitcast_convert_type(kj - qi, jnp.uint32) <= jnp.uint32(blk)

    for g, (win, dil) in enumerate(ATTN_GROUPS):
        assert win // dil == blk and dil in (1, 4, 16) and seq % (blk * dil) == 0
        sub = seq // dil
        nb = sub // blk

        def stage(src, dil=dil):
            if dil == 16:
                for r1 in range(4):
                    tmpf[r1 * quarter:(r1 + 1) * quarter, :] = src[pl.ds(r1, quarter, stride=4), :]

        def rows_of(src, r, n0, cnt, dil=dil):
            if dil == 1:
                return src[pl.ds(n0, cnt), :]
            if dil == 4:
                return src[pl.ds(r + 4 * n0, cnt, stride=4), :]
            return tmpf[pl.ds((r % 4) * quarter + r // 4 + 4 * n0, cnt, stride=4), :]

        stage(q_refs[g])
        for r in range(dil):
            for n in range(nb):
                qf = rows_of(q_refs[g], r, n * blk, blk)
                base = (r * nb + n) * 2 * blk
                qb[base:base + blk, :] = jnp.where(lo, qf, 0.0).astype(BF16)
                qb[base + blk:base + 2 * blk, :] = jnp.where(lo, 0.0, qf).astype(BF16)
        for src, dst in ((k_refs[g], kb), (v_refs[g], vb)):
            stage(src)
            for r in range(dil):
                dst[r * sub:(r + 1) * sub, :] = rows_of(src, r, 0, sub).astype(BF16)

        def block(r, n, first, g=g, dil=dil, sub=sub, nb=nb):
            q2 = qb[pl.ds((r * nb + n) * 2 * blk, 2 * blk), :]
            nk = blk if first else 2 * blk
            k0 = r * sub + (n if first else n - 1) * blk
            kt = kb[pl.ds(k0, nk), :]
            vt = vb[pl.ds(k0, nk), :]
            s = lax.dot_general(q2, kt, (((1,), (1,)), ((), ())), preferred_element_type=F32)
            s = jnp.where(valid_mask(nk), s, NEG_INF)
            m = jnp.max(s, axis=-1, keepdims=True)
            p = jnp.exp(s - m)
            l = jnp.sum(p, axis=-1, keepdims=True)
            pv = jnp.dot(p.astype(BF16), vt, preferred_element_type=F32)
            if dil > 1:
                out_rows = pl.ds(r + n * blk * dil, blk, stride=dil)
            else:
                out_rows = pl.ds(n * blk, blk)
            accs[g, out_rows, :] = jnp.where(lo, pv[:blk], pv[blk:])
            ms[g, out_rows, :] = jnp.where(lo, m[:blk], m[blk:])
            ls[g, out_rows, :] = jnp.where(lo, l[:blk], l[blk:])

        for r in range(dil):
            for n in range(nb):
                block(r, n, n == 0)
                _drain(side_tasks, n_blocks_total, done_blocks)
                done_blocks += 1

    def merge(i, carry):
        rows = pl.ds(pl.multiple_of(i * blk, blk), blk)
        m = jnp.maximum(jnp.maximum(ms[0, rows, :], ms[1, rows, :]), ms[2, rows, :])
        num = jnp.zeros((blk, LANES), F32)
        den = jnp.zeros((blk, LANES), F32)
        for g in range(N_GROUPS):
            w = jnp.exp(ms[g, rows, :] - m)
            num = num + w * accs[g, rows, :]
            den = den + w * ls[g, rows, :]
        o_ref[rows, :] = (num / den).astype(o_ref.dtype)
        return carry

    lax.fori_loop(0, seq // blk, merge, 0)


def _prompt_attn(q, kvs, side):
    b, s, _ = q.shape
    gw = HEADS * HEAD_DIM
    n_hp = gw // LANES
    blk_spec = lambda col: pl.BlockSpec((None, s, LANES), lambda i, j, col=col: (i, 0, col + j))
    in_specs = ([blk_spec(g * n_hp) for g in range(N_GROUPS)]
                + [blk_spec(0)] * N_GROUPS
                + [blk_spec(n_hp)] * N_GROUPS)
    args = [q] * N_GROUPS + list(kvs) + list(kvs)
    (o,), kc = _host_call(
        functools.partial(_prompt_attn_kernel, s), side, lambda i, j: i * n_hp + j, len(in_specs), 1,
        grid=(b, n_hp), in_specs=in_specs,
        out_specs=[pl.BlockSpec((None, s, LANES), lambda i, j: (i, 0, j))],
        out_shape=[jax.ShapeDtypeStruct((b, s, gw), BF16)], args=args,
        scratch_shapes=[pltpu.VMEM((2 * s, LANES), BF16),
                        pltpu.VMEM((s, LANES), BF16),
                        pltpu.VMEM((s, LANES), BF16),
                        pltpu.VMEM((s, LANES), F32)]
                       + [pltpu.VMEM((N_GROUPS, s, LANES), F32)] * 3,
        name="prompt_attn")
    return o, kc


def _compact_kernel(side, *refs):
    _, side_in, _, side_out, _ = _split_side(refs, 0, 0, side)
    for task in _side_tasks(side, side_in, side_out):
        task()


def _compact_rest(side, n_units):
    _, kc = _host_call(_compact_kernel, side, lambda i: i, 0, 0, grid=(n_units,), in_specs=[],
                       out_specs=[], out_shape=[], args=[], scratch_shapes=[], name="compact_caches")
    return kc


def _sample_attn_kernel(bb, q_ref, kv0_ref, kv1_ref, kv2_ref, kc_ref, o_ref):
    gw = HEADS * HEAD_DIM
    head_of_row = lax.broadcasted_iota(jnp.int32, (HEADS, gw), 0)
    head_of_col = lax.broadcasted_iota(jnp.int32, (HEADS, gw), 1) // HEAD_DIM
    diag = head_of_row == head_of_col
    nk = kc_ref.shape[-1] // N_GROUPS
    kvn_refs = (kv0_ref, kv1_ref, kv2_ref)

    def body(b, carry):
        row = pl.ds(b, 1)
        q_bd = [jnp.where(diag, q_ref[row, g * gw:(g + 1) * gw], 0.0) for g in range(N_GROUPS)]
        s_all = jnp.dot(jnp.concatenate(q_bd, axis=0).astype(BF16), kc_ref[b, 0],
                        preferred_element_type=F32)
        s = [s_all[g * HEADS:(g + 1) * HEADS, g * nk:(g + 1) * nk] for g in range(N_GROUPS)]
        s_new = [jnp.sum(q_bd[g] * kvn_refs[g][row, 0:gw], axis=-1, keepdims=True)
                 for g in range(N_GROUPS)]
        m = functools.reduce(jnp.maximum, [jnp.max(a, axis=-1, keepdims=True) for a in s] + s_new)
        p = [jnp.exp(a - m) for a in s]
        p_new = [jnp.exp(a - m) for a in s_new]
        l = sum(jnp.sum(a, axis=-1, keepdims=True) for a in p) + sum(p_new)
        acc = lax.dot_general(jnp.concatenate(p, axis=1).astype(BF16), kc_ref[b, 1],
                              (((1,), (1,)), ((), ())), preferred_element_type=F32)
        for g in range(N_GROUPS):
            acc = acc + p_new[g] * kvn_refs[g][row, gw:2 * gw]
        o_ref[row, :] = jnp.sum(jnp.where(diag, acc / l, 0.0), axis=0, keepdims=True)
        return carry

    lax.fori_loop(0, bb, body, 0, unroll=4)


def _sample_attn(q, kvn, kc, bb):
    n = q.shape[0]
    gw = HEADS * HEAD_DIM
    rows = lambda width: pl.BlockSpec((bb, width), lambda i: (i, 0))
    return pl.pallas_call(
        functools.partial(_sample_attn_kernel, bb),
        grid=(n // bb,),
        in_specs=[rows(q.shape[1])] + [rows(2 * gw)] * N_GROUPS
                 + [pl.BlockSpec((bb,) + kc.shape[1:], lambda i: (i, 0, 0, 0))],
        out_specs=rows(gw),
        out_shape=jax.ShapeDtypeStruct((n, gw), F32),
        compiler_params=_cparams(1), name="sample_attn",
    )(q, *kvn, kc)


def _rot_tables(pos):
    half = ROT_DIM // 2
    lane = lax.broadcasted_iota(jnp.int32, (pos.shape[0], LANES), 1) % HEAD_DIM
    inv_freq = ROPE_THETA ** (-(2 * (lane % half)).astype(F32) / ROT_DIM)
    ang = pos.astype(F32)[:, None] * inv_freq
    cos, sin = jnp.cos(ang), jnp.sin(ang)
    cosb = jnp.where(lane < ROT_DIM, cos, 1.0)
    sina = jnp.where(lane < half, -sin, 0.0)
    sinb = jnp.where((lane >= half) & (lane < ROT_DIM), sin, 0.0)
    return cosb, sina, sinb


def kernel(x_prompt, x_sample, state_pool, state_conv, cache_kv_w128, cache_kv_w512, cache_kv_w2048,
           g_pool, w_pool, pool_scale, g_ffn, w_up, conv_w, conv_b, w_down, g_kv, w_kv, k_gain,
           g_attn, w_q, q_gain, w_o):
    b, s, d = x_prompt.shape
    n = x_sample.shape[0]
    depth = w_up.shape[0]
    n_pool = g_pool.shape[0]
    gw = HEADS * HEAD_DIM
    rows3 = lambda a: a.reshape(a.shape[0], 1, a.shape[1])
    lane2 = lambda a: jnp.tile(a, LANES // HEAD_DIM).reshape(1, LANES)
    tm = min(512, s)
    bn = min(64, n)
    bb = min(8, n)

    gf3, cb3 = rows3(g_ffn), rows3(conv_b)
    ffn_mats = [(w_up[0].astype(BF16), w_down[0].astype(BF16))]

    def ffn_w(l):
        wup_l, wdown_l = ffn_mats[l]
        return [(gf3, l), (wup_l, None), (conv_w, l), (cb3, l), (wdown_l, None)]
    pool_w = (rows3(g_pool), w_pool.astype(BF16), rows3(pool_scale))
    wkv_b, wq_b, wo_b = w_kv.astype(BF16), w_q.astype(BF16), w_o.astype(BF16)
    q_scale = HEAD_DIM ** -0.5

    def proj_w(pos):
        cosb, sina, sinb = _rot_tables(pos)
        kg = lane2(k_gain)
        kv_w = ((cosb * kg, sina, sinb), g_kv.reshape(1, d), wkv_b, kg, None)
        q_w = []
        for j in range(depth - n_pool):
            qg = lane2(q_gain[j])
            q_w.append(((cosb * (qg * q_scale), sina * q_scale, sinb * q_scale), rows3(g_attn), wq_b, qg, j))
        return kv_w, q_w

    kv_w_p, q_w_p = proj_w(jnp.arange(s, dtype=jnp.int32))
    kv_w_s, q_w_s = proj_w(jnp.full((n,), PAST_LEN, jnp.int32))

    c_args, sels = _side_plan((cache_kv_w128, cache_kv_w512, cache_kv_w2048))
    units = {"done": 0, "kc": None}

    def side_for(steps, per_step=1, use_mxu=False):
        if units["done"] + steps * per_step > 2 * n:
            return None
        side = _Side(c_args, sels, units["done"], units["kc"], per_step, use_mxu)
        units["done"] += steps * per_step
        return side

    def took(kc):
        if kc is not None:
            units["kc"] = kc

    xp = x_prompt
    pool_p, conv_p = [], []
    big = max(range(N_GROUPS), key=lambda g: ATTN_GROUPS[g][0])
    n_steps = b * (s // tm)
    can_cast = _cast_rows(d, w_down.shape[1], n_steps) is not None
    kvs_p = kvt_p = None
    for l in range(depth):
        cast = ((w_up, w_down), l + 1) if can_cast and l + 1 <= n_pool else None
        if l < n_pool:
            (xp, ptail, ctail, *nxt), kc = _prompt_layer("pool", xp, pool_w, l, ffn_w(l), tm,
                                                         side_for(n_steps, 1), cast)
            took(kc)
            pool_p.append(ptail[:, 2 * SUBLANES - POOL_HIST:])
        else:
            j = l - n_pool
            x2d = xp.reshape(b * s, d)
            if j == 0:
                (*kvs_p, q_p, kvt_p), kc = _proj(x2d, s // tm, kv_w_p, q_w_p[j], tm, (big,), None)
                kvs_p = [a.reshape(b, s, 2 * gw) for a in kvs_p]
            else:
                (q_p,), kc = _proj(x2d, s // tm, None, q_w_p[j], tm, (), side_for(n_steps, 2, True))
            took(kc)
            o_p, kc = _prompt_attn(q_p.reshape(b, s, N_GROUPS * gw), kvs_p, None)
            took(kc)
            (xp, ctail, *nxt), kc = _prompt_layer("attn", xp, (o_p, wo_b), j, ffn_w(l), tm,
                                                  side_for(n_steps, 2), cast)
            took(kc)
        if l + 1 < depth:
            nxt = list(nxt) + [w.astype(BF16) for w in (w_up[l + 1], w_down[l + 1])[len(nxt):]]
            ffn_mats.append(tuple(nxt))
        conv_p.append(ctail[:, SUBLANES - (CONV_W - 1):])
    if units["done"] < 2 * n:
        took(_compact_rest(_Side(c_args, sels, units["done"], units["kc"], 1, True), 2 * n - units["done"]))
    kc = units["kc"]

    xs = x_sample.reshape(n, d)
    hist_t = jnp.swapaxes(state_conv, 1, 2)
    st_t = jnp.swapaxes(state_pool, 1, 2)
    kvs_s = nst = nhist = None
    for l in range(depth):
        if l < n_pool:
            xs, nst, nhist = _sample_layer("pool", xs, (st_t,) + pool_w, l, ffn_w(l), l, hist_t, bn,
                                           (nst, nhist))
        else:
            j = l - n_pool
            if j == 0:
                (*kvs_s, q_s), _ = _proj(xs, 1, kv_w_s, q_w_s[j], n)
            else:
                (q_s,), _ = _proj(xs, 1, None, q_w_s[j], n)
            o_s = _sample_attn(q_s, kvs_s, kc, bb)
            xs, nhist = _sample_layer("attn", xs, (o_s, wo_b), j, ffn_w(l), l, hist_t, bn, (nhist,))

    outs = [xp, xs.reshape(n, 1, d), jnp.stack(pool_p), jnp.swapaxes(nst, 1, 2),
            jnp.stack(conv_p), jnp.swapaxes(nhist, 1, 2)]
    for g, (win, _) in enumerate(ATTN_GROUPS):
        if g == big and win >= s:
            rows_p = jnp.transpose(kvt_p.reshape(b, 2, HEADS, HEAD_DIM, s), (0, 4, 1, 2, 3))
        else:
            rows_p = kvs_p[g][:, s - min(win, s):].reshape(b, min(win, s), 2, HEADS, HEAD_DIM)
        outs += [rows_p, kvs_s[g].reshape(n, 1, 2, HEADS, HEAD_DIM)]
    return tuple(outs)
```

```python
import collections
import functools

import jax
import jax.numpy as jnp
from jax import lax
from jax.experimental import pallas as pl
from jax.experimental.pallas import tpu as pltpu

F32 = jnp.float32
BF16 = jnp.bfloat16

NORM_EPS = 1e-6
NEG_INF = -1e30
POOL_WINDOWS = (2, 4, 8, 16)
POOL_HIST = max(POOL_WINDOWS) - 1
CONV_W = 3
ATTN_GROUPS = ((128, 1), (512, 4), (2048, 16))
N_GROUPS = len(ATTN_GROUPS)
HEADS = 8
HEAD_DIM = 64
ROT_DIM = HEAD_DIM // 4
ROPE_THETA = 500000.0
ATTN_BLOCK = 128
PAST_LEN = 2048

LANES = 128
SUBLANES = 8
VMEM_LIMIT = 62 * 1024 * 1024


def _cparams(n_grid):
    return pltpu.CompilerParams(dimension_semantics=("arbitrary",) * n_grid,
                                vmem_limit_bytes=VMEM_LIMIT)


def _const_spec(shape):
    nd = len(shape)
    return pl.BlockSpec(shape, lambda *_: (0,) * nd, pipeline_mode=pl.Buffered(1))


def _layer_spec(stacked, l):
    if l is None:
        return _const_spec(stacked.shape)
    nd = stacked.ndim - 1
    return pl.BlockSpec((None,) + stacked.shape[1:], lambda *_: (l,) + (0,) * nd,
                        pipeline_mode=pl.Buffered(1))


def _rms(x, g):
    ms = jnp.mean(x * x, axis=-1, keepdims=True)
    return x * lax.rsqrt(ms + NORM_EPS) * g


def _silu(a):
    return a / (1.0 + jnp.exp(-a))


_Side = collections.namedtuple("_Side", "c_args sels base kc per_step use_mxu")

COMPACT_ROWS = 64


def _compact_tasks(c_refs, sel_refs, o_ref, use_mxu):
    nk = c_refs[0].shape[-1]
    n_rows = c_refs[0].shape[1]
    assert nk == LANES and n_rows % COMPACT_ROWS == 0
    tasks = []

    def cast_task(u):
        def run():
            o_ref[u, :, 0:nk] = c_refs[0][u].astype(BF16)
        return run

    def chunk_task(u, g, r0):
        rows, out_cols = slice(r0, r0 + COMPACT_ROWS), slice(g * nk, (g + 1) * nk)
        dil = c_refs[g].shape[-1] // nk

        def run():
            if use_mxu:
                o_ref[u, rows, out_cols] = jnp.dot(c_refs[g][u, rows, :].astype(BF16), sel_refs[g - 1][...],
                                                   preferred_element_type=F32).astype(BF16)
                return
            keep = lax.broadcasted_iota(jnp.int32, (COMPACT_ROWS, LANES), 1) % dil == 0
            acc = jnp.where(keep, c_refs[g][u, rows, 0:LANES], 0.0)
            for j in range(1, dil):
                tile = jnp.where(keep, c_refs[g][u, rows, j * LANES:(j + 1) * LANES], 0.0)
                acc = acc + pltpu.roll(tile, j, 1)
            o_ref[u, rows, out_cols] = acc.astype(BF16)
        return run

    for u in range(o_ref.shape[0]):
        tasks.append(cast_task(u))
        for g in range(1, len(c_refs)):
            tasks += [chunk_task(u, g, r0) for r0 in range(0, n_rows, COMPACT_ROWS)]
    return tasks


def _drain(tasks, n_slots, i):
    for task in tasks[len(tasks) * i // n_slots:len(tasks) * (i + 1) // n_slots]:
        task()


def _side_plan(caches):
    n = caches[0].shape[0]
    gw = HEADS * HEAD_DIM
    c_args, sels = [], []
    for c, (win, dil) in zip(caches, ATTN_GROUPS):
        n_buf, n_back = c.shape[1], win // dil
        assert n_buf == n_back * dil, "window buffer must hold exactly n_back dilated rows"
        c_args.append(jnp.transpose(c, (0, 2, 3, 4, 1)).reshape(n, 2, gw, n_buf))
        if dil > 1:
            t = lax.broadcasted_iota(jnp.int32, (n_buf, n_back), 0)
            m = lax.broadcasted_iota(jnp.int32, (n_buf, n_back), 1)
            sels.append((t == dil * m).astype(BF16))
    return c_args, sels


def _side_io(side, step_of):
    gw = HEADS * HEAD_DIM
    n = side.c_args[0].shape[0]
    n_keys = N_GROUPS * (ATTN_GROUPS[0][0] // ATTN_GROUPS[0][1])
    assert side.per_step in (1, 2) and side.base % side.per_step == 0
    unit = lambda *idx: side.base + side.per_step * step_of(*idx)
    blk_idx = lambda *idx: (unit(*idx) // 2, (unit(*idx) % 2) // side.per_step, 0, 0)
    c_specs = [pl.BlockSpec((None, side.per_step, gw, c.shape[3]), blk_idx) for c in side.c_args]
    sels = side.sels if side.use_mxu else []
    in_specs = c_specs + [_const_spec(s_.shape) for s_ in sels]
    args = list(side.c_args) + list(sels)
    if side.kc is not None:
        in_specs.append(pl.BlockSpec(memory_space=pl.ANY))
        args.append(side.kc)
    out_spec = pl.BlockSpec((None, side.per_step, gw, n_keys), blk_idx)
    out_shape = jax.ShapeDtypeStruct((n, 2, gw, n_keys), BF16)
    return in_specs, args, out_spec, out_shape


def _split_side(refs, n_in, n_out, side):
    ins, rest = refs[:n_in], refs[n_in:]
    side_in = ()
    if side is not None:
        k = N_GROUPS + (len(side.sels) if side.use_mxu else 0) + (side.kc is not None)
        side_in, rest = rest[:k], rest[k:]
    outs, rest = rest[:n_out], rest[n_out:]
    side_out = None
    if side is not None:
        side_out, rest = rest[0], rest[1:]
    return ins, side_in, outs, side_out, rest


def _side_tasks(side, side_in, side_out):
    if side is None:
        return []
    n_sel = len(side.sels) if side.use_mxu else 0
    return _compact_tasks(side_in[:N_GROUPS], side_in[N_GROUPS:N_GROUPS + n_sel], side_out, side.use_mxu)


def _host_call(kernel, side, step_of, n_in, n_out, *, grid, in_specs, out_specs, out_shape, args,
               scratch_shapes, name):
    aliases = {}
    if side is not None:
        s_in, s_args, s_out, s_shape = _side_io(side, step_of)
        if side.kc is not None:
            aliases = {len(in_specs) + len(s_in) - 1: len(out_specs)}
        in_specs, args = in_specs + s_in, args + s_args
        out_specs, out_shape = out_specs + [s_out], out_shape + [s_shape]
    assert len(in_specs) == n_in + (0 if side is None else len(s_in))
    res = pl.pallas_call(
        functools.partial(kernel, side), grid=grid, in_specs=in_specs, out_specs=out_specs,
        out_shape=out_shape, scratch_shapes=scratch_shapes, input_output_aliases=aliases,
        compiler_params=_cparams(len(grid)), name=name,
    )(*args)
    return (res[:n_out], res[n_out]) if side is not None else (res, None)


def _ffn_core(x1, gf_ref, wup_ref, cw_ref, cb_ref, wdown_ref, hbuf, act, fc, rows, side_tasks=()):
    n_blocks, to_rows, prev_rows, from_rows = rows
    d_ff = wdown_ref.shape[0]
    q = hbuf.shape[0] // n_blocks
    blocks = [slice(j * q, (j + 1) * q) for j in range(n_blocks)]
    h = _rms(x1, gf_ref[...])
    for j, rb in enumerate(blocks):
        hbuf[rb, :] = to_rows(h, j).astype(BF16)
    for c in range(d_ff // fc):
        halves = []
        for base in (0, d_ff):
            cols = slice(base + c * fc, base + (c + 1) * fc)
            u = jnp.dot(hbuf[...], wup_ref[:, cols], preferred_element_type=F32)
            ub = [u[rb] for rb in blocks]
            prev = prev_rows(cols, ub)
            halves.append([cb_ref[:, cols] + ((cw_ref[0:1, cols] * u2 + cw_ref[1:2, cols] * u1)
                                              + cw_ref[2:3, cols] * u0)
                           for u0, (u1, u2) in zip(ub, prev)])
        for rb, a, b in zip(blocks, *halves):
            act[rb, c * fc:(c + 1) * fc] = (_silu(a) * b).astype(BF16)
        _drain(side_tasks, d_ff // fc, c)
    return x1 + from_rows(jnp.dot(act[...], wdown_ref[...], preferred_element_type=F32))


PROMPT_ROW_BLOCKS = 4


def _prompt_rows(slab, sbuf, ucarry, tm):
    nb = PROMPT_ROW_BLOCKS
    q = tm // nb
    n_slab = slab.shape[0]

    def to_rows(h, j):
        if j == 0:
            for k in range(n_slab):
                slab[k] = h[:, k * LANES:(k + 1) * LANES]
        return jnp.concatenate([slab[k, pl.ds(j, q, stride=nb), :] for k in range(n_slab)], axis=1)

    def shift_down(slot, blk, carry_row):
        sbuf[slot, SUBLANES - 1:SUBLANES, :] = carry_row
        sbuf[slot, SUBLANES:SUBLANES + q, :] = blk
        return sbuf[slot, SUBLANES - 1:SUBLANES - 1 + q, :]

    def prev_rows(cols, ub):
        s3 = shift_down(0, ub[3], ucarry[SUBLANES - 1:SUBLANES, cols])
        s2 = shift_down(1, ub[2], ucarry[SUBLANES - 2:SUBLANES - 1, cols])
        ucarry[SUBLANES - 1:SUBLANES, cols] = ub[3][q - 1:q]
        ucarry[SUBLANES - 2:SUBLANES - 1, cols] = ub[2][q - 1:q]
        return [(s3, s2), (ub[0], s3), (ub[1], ub[0]), (ub[2], ub[1])]

    def from_rows(y):
        for j in range(nb):
            for k in range(n_slab):
                slab[k, pl.ds(j, q, stride=nb), :] = y[j * q:(j + 1) * q, k * LANES:(k + 1) * LANES]
        return jnp.concatenate([slab[k] for k in range(n_slab)], axis=1)

    return nb, to_rows, prev_rows, from_rows


def _sample_rows(hist_ref, newhist_ref):
    def prev_rows(cols, ub):
        newhist_ref[0, :, cols] = hist_ref[1, :, cols]
        newhist_ref[1, :, cols] = ub[0]
        return [(hist_ref[1, :, cols], hist_ref[0, :, cols])]
    return 1, (lambda h, j: h), prev_rows, (lambda y: y)


def _pool_mix(xn, window_sum, cnt_of, wp_ref, ps_ref):
    gw = xn.shape[1] // len(POOL_WINDOWS)
    parts = []
    for g, w in enumerate(POOL_WINDOWS):
        sl = slice(g * gw, (g + 1) * gw)
        cur = xn[:, sl]
        d = window_sum(w, sl, cur) / cnt_of(w, cur.shape) - cur
        parts.append(jnp.dot(d.astype(BF16), wp_ref[g], preferred_element_type=F32))
    return jnp.concatenate(parts, axis=-1) * ps_ref[...]


POOL_PAD = 2 * SUBLANES


def _prompt_window_sum(xnbuf, lv, tm):
    hist = 2 * SUBLANES
    span = hist + tm

    def window_sum(w, sl, cur):
        s = xnbuf[POOL_PAD:POOL_PAD + span, sl] + xnbuf[POOL_PAD - 1:POOL_PAD - 1 + span, sl]
        shift, slot = 2, 0
        while shift < w:
            lv[slot, POOL_PAD:POOL_PAD + span, :] = s
            s = lv[slot, POOL_PAD:POOL_PAD + span, :] + lv[slot, POOL_PAD - shift:POOL_PAD - shift + span, :]
            shift, slot = 2 * shift, 1 - slot
        return s[hist:]
    return window_sum


def _prompt_layer_kernel(mode, tm, fc, n_cast, side, *refs):
    ins, side_in, outs, side_out, scratch = _split_side(refs, (9 if mode == "pool" else 8) + n_cast,
                                                        (3 if mode == "pool" else 2) + n_cast, side)
    tasks = _side_tasks(side, side_in, side_out)
    if n_cast:
        (ins, cast_in), (outs, cast_out) = (ins[:-n_cast], ins[-n_cast:]), (outs[:-n_cast], outs[-n_cast:])
        tasks = tasks + _cast_tasks(cast_in, cast_out)
    if mode == "pool":
        x_ref, gp_ref, wp_ref, ps_ref, gf_ref, wup_ref, cw_ref, cb_ref, wdown_ref = ins
        xo_ref, ptail_ref, ctail_ref = outs
        xnbuf, lv, hbuf, act, slab, sbuf, ucarry = scratch
    else:
        x_ref, o_ref, wo_ref, gf_ref, wup_ref, cw_ref, cb_ref, wdown_ref = ins
        xo_ref, ctail_ref = outs
        hbuf, act, slab, sbuf, ucarry = scratch
    t = pl.program_id(1)
    hist = 2 * SUBLANES

    @pl.when(t == 0)
    def _():
        ucarry[...] = jnp.zeros_like(ucarry)
        if mode == "pool":
            xnbuf[0:POOL_PAD + hist, :] = jnp.zeros((POOL_PAD + hist, xnbuf.shape[1]), F32)
            lv[:, 0:POOL_PAD, :] = jnp.zeros((lv.shape[0], POOL_PAD, lv.shape[2]), F32)

    x = x_ref[...]
    if mode == "pool":
        xn = _rms(x, gp_ref[...])
        xnbuf[POOL_PAD + hist:POOL_PAD + hist + tm, :] = xn

        def cnt_of(w, shape):
            pos = t * tm + lax.broadcasted_iota(jnp.int32, shape, 0)
            return jnp.minimum(pos + 1, w).astype(F32)

        x1 = x + _pool_mix(xn, _prompt_window_sum(xnbuf, lv, tm), cnt_of, wp_ref, ps_ref)
        tail = xnbuf[POOL_PAD + tm:POOL_PAD + tm + hist, :]
        ptail_ref[...] = tail
        xnbuf[POOL_PAD:POOL_PAD + hist, :] = tail
    else:
        x1 = x + jnp.dot(o_ref[...].astype(BF16), wo_ref[...], preferred_element_type=F32)

    xo_ref[...] = _ffn_core(x1, gf_ref, wup_ref, cw_ref, cb_ref, wdown_ref, hbuf, act, fc,
                            _prompt_rows(slab, sbuf, ucarry, tm), tasks)
    ctail_ref[...] = ucarry[...]


def _sample_layer_kernel(mode, fc, n_chained, *refs):
    n_in = 11 if mode == "pool" else 9
    refs = refs[:n_in] + refs[n_in + n_chained:]
    if mode == "pool":
        (x_ref, st_ref, gp_ref, wp_ref, ps_ref, gf_ref, wup_ref, cw_ref, cb_ref, wdown_ref,
         hist_ref, xo_ref, nst_ref, nhist_ref, hbuf, act) = refs
    else:
        (x_ref, o_ref, wo_ref, gf_ref, wup_ref, cw_ref, cb_ref, wdown_ref,
         hist_ref, xo_ref, nhist_ref, hbuf, act) = refs
    x = x_ref[...]
    if mode == "pool":
        xn = _rms(x, gp_ref[...])
        n_hist = st_ref.shape[0]

        def window_sum(w, sl, cur):
            s = cur
            for k in range(1, w):
                s = s + st_ref[n_hist - k, :, sl]
            return s

        def cnt_of(w, shape):
            return float(min(PAST_LEN + 1, w))

        x1 = x + _pool_mix(xn, window_sum, cnt_of, wp_ref, ps_ref)
        for i in range(n_hist - 1):
            nst_ref[i] = st_ref[i + 1]
        nst_ref[n_hist - 1] = xn
    else:
        x1 = x + jnp.dot(o_ref[...].astype(BF16), wo_ref[...], preferred_element_type=F32)
    xo_ref[...] = _ffn_core(x1, gf_ref, wup_ref, cw_ref, cb_ref, wdown_ref, hbuf, act, fc,
                            _sample_rows(hist_ref, nhist_ref))


def _pick_fc(f):
    for fc in (256, 128):
        if f % fc == 0:
            return fc
    return f


def _cast_rows(d, f, steps):
    if steps % 2 or d % steps or f % (steps // 2):
        return None
    ru, rd = d // steps, f // (steps // 2)
    return (ru, rd) if ru % (2 * SUBLANES) == 0 and rd % (2 * SUBLANES) == 0 else None


def _cast_io(cast, d, f, steps, step_of):
    weights, nl = cast
    in_specs, args, out_specs, out_shape = [], [], [], []
    for w_f, (rows, span) in zip(weights, zip(_cast_rows(d, f, steps), (1, 2))):
        blk = lambda *idx, span=span: step_of(*idx) // span
        in_specs.append(pl.BlockSpec((None, rows, w_f.shape[2]), lambda *idx, blk=blk: (nl, blk(*idx), 0)))
        args.append(w_f)
        out_specs.append(pl.BlockSpec((rows, w_f.shape[2]), lambda *idx, blk=blk: (blk(*idx), 0)))
        out_shape.append(jax.ShapeDtypeStruct(w_f.shape[1:], BF16))
    return in_specs, args, out_specs, out_shape


def _cast_tasks(cast_in, cast_out):
    def task(src, dst):
        def run():
            dst[...] = src[...].astype(BF16)
        return run
    return [task(s_, d_) for s_, d_ in zip(cast_in, cast_out)]


def _prompt_layer(mode, x, pre, pre_l, ffn_w, tm, side, cast=None):
    b, s, d = x.shape
    f2, f = ffn_w[1][0].shape[-1], ffn_w[4][0].shape[-2]
    fc = _pick_fc(f)
    row = pl.BlockSpec((None, tm, d), lambda i, j: (i, j, 0))
    hist = 2 * SUBLANES
    ctail_spec = pl.BlockSpec((None, SUBLANES, f2), lambda i, j: (i, 0, 0))
    ctail_shape = jax.ShapeDtypeStruct((b, SUBLANES, f2), F32)
    common_scratch = [pltpu.VMEM((tm, d), BF16), pltpu.VMEM((tm, f), BF16),
                      pltpu.VMEM((d // LANES, tm, LANES), F32),
                      pltpu.VMEM((2, tm // PROMPT_ROW_BLOCKS + SUBLANES, fc), F32),
                      pltpu.VMEM((SUBLANES, f2), F32)]
    if mode == "pool":
        in_specs = [row] + [_layer_spec(a, pre_l) for a in pre]
        args = [x, *pre]
        out_specs = [row, pl.BlockSpec((None, hist, d), lambda i, j: (i, 0, 0)), ctail_spec]
        out_shape = [jax.ShapeDtypeStruct(x.shape, F32), jax.ShapeDtypeStruct((b, hist, d), F32),
                     ctail_shape]
        ext = POOL_PAD + hist + tm
        scratch = [pltpu.VMEM((ext, d), F32),
                   pltpu.VMEM((2, ext, d // len(POOL_WINDOWS)), F32)] + common_scratch
    else:
        o, wo = pre
        in_specs = [row, pl.BlockSpec((None, tm, o.shape[2]), lambda i, j: (i, j, 0)),
                    _layer_spec(wo, pre_l)]
        args = [x, o, wo]
        out_specs = [row, ctail_spec]
        out_shape = [jax.ShapeDtypeStruct(x.shape, F32), ctail_shape]
        scratch = common_scratch
    nj = s // tm
    step_of = lambda i, j: i * nj + j
    in_specs = in_specs + [_layer_spec(a, l) for a, l in ffn_w]
    args = args + [a for a, _ in ffn_w]
    n_cast = 0
    if cast is not None:
        c_in, c_args, c_out, c_shape = _cast_io(cast, d, f, b * nj, step_of)
        in_specs, args = in_specs + c_in, args + c_args
        out_specs, out_shape = out_specs + c_out, out_shape + c_shape
        n_cast = len(c_in)
    return _host_call(
        functools.partial(_prompt_layer_kernel, mode, tm, fc, n_cast), side, step_of,
        len(in_specs), len(out_specs),
        grid=(b, nj), in_specs=in_specs, out_specs=out_specs, out_shape=out_shape, args=args,
        scratch_shapes=scratch, name=f"prompt_{mode}_layer")


def _sample_layer(mode, x, pre, pre_l, ffn_w, l, hist_t, bn, chained):
    n, d = x.shape
    f2, f = ffn_w[1][0].shape[-1], ffn_w[4][0].shape[-2]
    fc = _pick_fc(f)
    row = lambda width: pl.BlockSpec((bn, width), lambda i: (i, 0))
    tmaj = lambda a, k: pl.BlockSpec((None, a.shape[1], bn, a.shape[3]), lambda i: (k, 0, i, 0))
    if mode == "pool":
        st_t = pre[0]
        in_specs = [row(d), tmaj(st_t, pre_l)] + [_layer_spec(a, pre_l) for a in pre[1:]]
        args = [x, *pre]
        states = [(st_t, pre_l), (hist_t, l)]
    else:
        o, wo = pre
        in_specs = [row(d), row(o.shape[1]), _layer_spec(wo, pre_l)]
        args = [x, o, wo]
        states = [(hist_t, l)]
    in_specs = in_specs + [_layer_spec(a, k) for a, k in ffn_w] + [tmaj(hist_t, l)]
    args = args + [a for a, _ in ffn_w] + [hist_t]
    out_specs = [row(d)] + [tmaj(a, k) for a, k in states]
    out_shape = [jax.ShapeDtypeStruct(x.shape, F32)] + [jax.ShapeDtypeStruct(a.shape, F32) for a, _ in states]
    aliases = {}
    for k, prev in enumerate(chained):
        if prev is not None:
            aliases[len(in_specs)] = 1 + k
            in_specs.append(pl.BlockSpec(memory_space=pl.ANY))
            args.append(prev)
    return pl.pallas_call(
        functools.partial(_sample_layer_kernel, mode, fc, len(aliases)),
        grid=(n // bn,), in_specs=in_specs, out_specs=out_specs, out_shape=out_shape,
        scratch_shapes=[pltpu.VMEM((bn, d), BF16), pltpu.VMEM((bn, f), BF16)],
        input_output_aliases=aliases,
        compiler_params=_cparams(1), name=f"sample_{mode}_layer",
    )(*args)


HEAD_TILE = 2 * LANES


def _head_norm_rot(x, bd_ref, gain, tabs):
    gcos, sina, sinb = tabs
    ms = jnp.dot((x * x).astype(BF16), bd_ref[...], preferred_element_type=F32)
    outs = []
    for j in range(x.shape[1] // LANES):
        xs = x[:, j * LANES:(j + 1) * LANES]
        xg = xs * gain
        outs.append(xs * gcos + pltpu.roll(xg, LANES - ROT_DIM // 2, 1) * sina
                    + pltpu.roll(xg, ROT_DIM // 2, 1) * sinb)
    return jnp.concatenate(outs, axis=1) * lax.rsqrt(ms + NORM_EPS)


def _proj_kernel(do_kv, kvt_groups, n_tab_blocks, n_cast, side, *refs):
    n_out = ((N_GROUPS + 1 + len(kvt_groups)) if do_kv else 1) + n_cast
    ins, side_in, outs, side_out, _ = _split_side(refs, (14 if do_kv else 8) + n_cast, n_out, side)
    side_tasks = _side_tasks(side, side_in, side_out)
    if n_cast:
        side_tasks = side_tasks + _cast_tasks(ins[-n_cast:], outs[-n_cast:])
        ins, outs = ins[:-n_cast], outs[:-n_cast]
    x_ref, bd_ref = ins[0:2]
    if do_kv:
        ktabs, (gkv_ref, wkv_ref, kg_ref) = ins[2:5], ins[5:8]
        qtabs, (gq_ref, wq_ref, qg_ref) = ins[8:11], ins[11:14]
        kv_refs, q_ref, kvt_refs = outs[0:N_GROUPS], outs[N_GROUPS], outs[N_GROUPS + 1:]
    else:
        qtabs, (gq_ref, wq_ref, qg_ref) = ins[2:5], ins[5:8]
        q_ref = outs[0]
    x = x_ref[...]
    gw = HEADS * HEAD_DIM
    if do_kv:
        h = _rms(x, gkv_ref[...]).astype(BF16)
        kv = jnp.dot(h, wkv_ref[...], preferred_element_type=F32)
        tabs = [t[...] for t in ktabs]
        for g, out_ref in enumerate(kv_refs):
            for j in range(gw // HEAD_TILE):
                c0 = g * gw + j * HEAD_TILE
                out_ref[:, j * HEAD_TILE:(j + 1) * HEAD_TILE] = _head_norm_rot(
                    kv[:, c0:c0 + HEAD_TILE], bd_ref, kg_ref[...], tabs)
            v0 = N_GROUPS * gw + g * gw
            out_ref[:, gw:2 * gw] = kv[:, v0:v0 + gw]
        tm = x.shape[0]
        for (g, win), kvt_ref in zip(kvt_groups, kvt_refs):
            if win == n_tab_blocks * tm:
                kvt_ref[...] = kv_refs[g][...].T
            else:
                @pl.when(pl.program_id(0) % n_tab_blocks == n_tab_blocks - 1)
                def _(g=g, win=win, kvt_ref=kvt_ref):
                    kvt_ref[...] = kv_refs[g][tm - win:tm, :].T
    h = _rms(x, gq_ref[...]).astype(BF16)
    q = jnp.dot(h, wq_ref[...], preferred_element_type=F32)
    tabs = [t[...] for t in qtabs]
    n_tiles = q.shape[1] // HEAD_TILE
    for j in range(n_tiles):
        cols = slice(j * HEAD_TILE, (j + 1) * HEAD_TILE)
        q_ref[:, cols] = _head_norm_rot(q[:, cols], bd_ref, qg_ref[...], tabs)
        _drain(side_tasks, n_tiles, j)


def _proj(x2d, n_tab_blocks, kv_w, q_w, tm, kvt_groups=(), side=None, cast=None):
    r, d = x2d.shape
    row = pl.BlockSpec((tm, d), lambda i: (i, 0))
    tab = pl.BlockSpec((tm, LANES), lambda i: (i % n_tab_blocks, 0))
    gw = HEADS * HEAD_DIM
    head = lax.broadcasted_iota(jnp.int32, (HEAD_TILE, HEAD_TILE), 0) // HEAD_DIM
    bd = jnp.where(head == head.T, 1.0 / HEAD_DIM, 0.0).astype(BF16)
    in_specs, args = [row, _const_spec(bd.shape)], [x2d, bd]
    out_specs, out_shape = [], []
    for w in (kv_w, q_w):
        if w is None:
            continue
        tabs, g, wmat, hg, layer = w
        wspec = lambda a, layer=layer: _const_spec(a.shape) if layer is None else _layer_spec(a, layer)
        in_specs += [tab, tab, tab, wspec(g), wspec(wmat), _const_spec((1, LANES))]
        args += [*tabs, g, wmat, hg]
    if kv_w is not None:
        out_specs += [pl.BlockSpec((tm, 2 * gw), lambda i: (i, 0))] * N_GROUPS
        out_shape += [jax.ShapeDtypeStruct((r, 2 * gw), F32)] * N_GROUPS
    n_q = q_w[2].shape[-1]
    out_specs.append(pl.BlockSpec((tm, n_q), lambda i: (i, 0)))
    out_shape.append(jax.ShapeDtypeStruct((r, n_q), F32))
    seq = n_tab_blocks * tm
    if kv_w is not None:
        for _, win in kvt_groups:
            assert win == seq or win <= tm
            if win == seq:
                out_specs.append(pl.BlockSpec((None, 2 * gw, tm), lambda i: (i // n_tab_blocks, 0, i % n_tab_blocks)))
            else:
                out_specs.append(pl.BlockSpec((None, 2 * gw, win), lambda i: (i // n_tab_blocks, 0, 0)))
            out_shape.append(jax.ShapeDtypeStruct((r // seq, 2 * gw, win), F32))
    n_cast = 0
    if cast is not None:
        c_in, c_args, c_out, c_shape = _cast_io(cast, d, cast[0][1].shape[1], r // tm, lambda i: i)
        in_specs, args = in_specs + c_in, args + c_args
        out_specs, out_shape = out_specs + c_out, out_shape + c_shape
        n_cast = len(c_in)
    return _host_call(
        functools.partial(_proj_kernel, kv_w is not None, tuple(kvt_groups), n_tab_blocks, n_cast), side,
        lambda i: i, len(in_specs), len(out_specs),
        grid=(r // tm,), in_specs=in_specs, out_specs=out_specs, out_shape=out_shape, args=args,
        scratch_shapes=[], name="proj_kvq" if kv_w is not None else "proj_q")


def _prompt_attn_kernel(seq, side, *refs):
    ins, side_in, outs, side_out, scratch = _split_side(refs, 3 * N_GROUPS, 1, side)
    q_refs, k_refs, v_refs = ins[0:3], ins[3:6], ins[6:9]
    o_ref = outs[0]
    qb, kb, vb, tmpf, accs, ms, ls = scratch
    blk = ATTN_BLOCK
    lo = lax.broadcasted_iota(jnp.int32, (blk, LANES), 1) < HEAD_DIM
    quarter = seq // 4
    side_tasks = _side_tasks(side, side_in, side_out)
    n_blocks_total = sum(seq // blk for _ in ATTN_GROUPS)
    done_blocks = 0

    def valid_mask(nk):
        qi = lax.broadcasted_iota(jnp.int32, (2 * blk, nk), 0) & (blk - 1)
        kj = lax.broadcasted_iota(jnp.int32, (2 * blk, nk), 1)
        if nk == blk:
            return kj <= qi
        return lax.bitcast_convert_type(kj - qi, jnp.uint32) <= jnp.uint32(blk)

    for g, (win, dil) in enumerate(ATTN_GROUPS):
        assert win // dil == blk and dil in (1, 4, 16) and seq % (blk * dil) == 0
        sub = seq // dil
        nb = sub // blk

        def stage(src, dil=dil):
            if dil == 16:
                for r1 in range(4):
                    tmpf[r1 * quarter:(r1 + 1) * quarter, :] = src[pl.ds(r1, quarter, stride=4), :]

        def rows_of(src, r, n0, cnt, dil=dil):
            if dil == 1:
                return src[pl.ds(n0, cnt), :]
            if dil == 4:
                return src[pl.ds(r + 4 * n0, cnt, stride=4), :]
            return tmpf[pl.ds((r % 4) * quarter + r // 4 + 4 * n0, cnt, stride=4), :]

        stage(q_refs[g])
        for r in range(dil):
            for n in range(nb):
                qf = rows_of(q_refs[g], r, n * blk, blk)
                base = (r * nb + n) * 2 * blk
                qb[base:base + blk, :] = jnp.where(lo, qf, 0.0).astype(BF16)
                qb[base + blk:base + 2 * blk, :] = jnp.where(lo, 0.0, qf).astype(BF16)
        for src, dst in ((k_refs[g], kb), (v_refs[g], vb)):
            stage(src)
            for r in range(dil):
                dst[r * sub:(r + 1) * sub, :] = rows_of(src, r, 0, sub).astype(BF16)

        def block(r, n, first, g=g, dil=dil, sub=sub, nb=nb):
            q2 = qb[pl.ds((r * nb + n) * 2 * blk, 2 * blk), :]
            nk = blk if first else 2 * blk
            k0 = r * sub + (n if first else n - 1) * blk
            kt = kb[pl.ds(k0, nk), :]
            vt = vb[pl.ds(k0, nk), :]
            s = lax.dot_general(q2, kt, (((1,), (1,)), ((), ())), preferred_element_type=F32)
            s = jnp.where(valid_mask(nk), s, NEG_INF)
            m = jnp.max(s, axis=-1, keepdims=True)
            p = jnp.exp(s - m)
            l = jnp.sum(p, axis=-1, keepdims=True)
            pv = jnp.dot(p.astype(BF16), vt, preferred_element_type=F32)
            if dil > 1:
                out_rows = pl.ds(r + n * blk * dil, blk, stride=dil)
            else:
                out_rows = pl.ds(n * blk, blk)
            accs[g, out_rows, :] = jnp.where(lo, pv[:blk], pv[blk:])
            ms[g, out_rows, :] = jnp.where(lo, m[:blk], m[blk:])
            ls[g, out_rows, :] = jnp.where(lo, l[:blk], l[blk:])

        for r in range(dil):
            for n in range(nb):
                block(r, n, n == 0)
                _drain(side_tasks, n_blocks_total, done_blocks)
                done_blocks += 1

    def merge(i, carry):
        rows = pl.ds(pl.multiple_of(i * blk, blk), blk)
        m = jnp.maximum(jnp.maximum(ms[0, rows, :], ms[1, rows, :]), ms[2, rows, :])
        num = jnp.zeros((blk, LANES), F32)
        den = jnp.zeros((blk, LANES), F32)
        for g in range(N_GROUPS):
            w = jnp.exp(ms[g, rows, :] - m)
            num = num + w * accs[g, rows, :]
            den = den + w * ls[g, rows, :]
        o_ref[rows, :] = (num / den).astype(o_ref.dtype)
        return carry

    lax.fori_loop(0, seq // blk, merge, 0)


def _prompt_attn(q, kvs, side):
    b, s, _ = q.shape
    gw = HEADS * HEAD_DIM
    n_hp = gw // LANES
    blk_spec = lambda col: pl.BlockSpec((None, s, LANES), lambda i, j, col=col: (i, 0, col + j))
    in_specs = ([blk_spec(g * n_hp) for g in range(N_GROUPS)]
                + [blk_spec(0)] * N_GROUPS
                + [blk_spec(n_hp)] * N_GROUPS)
    args = [q] * N_GROUPS + list(kvs) + list(kvs)
    (o,), kc = _host_call(
        functools.partial(_prompt_attn_kernel, s), side, lambda i, j: i * n_hp + j, len(in_specs), 1,
        grid=(b, n_hp), in_specs=in_specs,
        out_specs=[pl.BlockSpec((None, s, LANES), lambda i, j: (i, 0, j))],
        out_shape=[jax.ShapeDtypeStruct((b, s, gw), BF16)], args=args,
        scratch_shapes=[pltpu.VMEM((2 * s, LANES), BF16),
                        pltpu.VMEM((s, LANES), BF16),
                        pltpu.VMEM((s, LANES), BF16),
                        pltpu.VMEM((s, LANES), F32)]
                       + [pltpu.VMEM((N_GROUPS, s, LANES), F32)] * 3,
        name="prompt_attn")
    return o, kc


def _compact_kernel(side, *refs):
    _, side_in, _, side_out, _ = _split_side(refs, 0, 0, side)
    for task in _side_tasks(side, side_in, side_out):
        task()


def _compact_rest(side, n_units):
    _, kc = _host_call(_compact_kernel, side, lambda i: i, 0, 0, grid=(n_units,), in_specs=[],
                       out_specs=[], out_shape=[], args=[], scratch_shapes=[], name="compact_caches")
    return kc


def _sample_attn_kernel(bb, q_ref, kv0_ref, kv1_ref, kv2_ref, kc_ref, o_ref):
    gw = HEADS * HEAD_DIM
    head_of_row = lax.broadcasted_iota(jnp.int32, (HEADS, gw), 0)
    head_of_col = lax.broadcasted_iota(jnp.int32, (HEADS, gw), 1) // HEAD_DIM
    diag = head_of_row == head_of_col
    nk = kc_ref.shape[-1] // N_GROUPS
    kvn_refs = (kv0_ref, kv1_ref, kv2_ref)

    def body(b, carry):
        row = pl.ds(b, 1)
        q_bd = [jnp.where(diag, q_ref[row, g * gw:(g + 1) * gw], 0.0) for g in range(N_GROUPS)]
        s_all = jnp.dot(jnp.concatenate(q_bd, axis=0).astype(BF16), kc_ref[b, 0],
                        preferred_element_type=F32)
        s = [s_all[g * HEADS:(g + 1) * HEADS, g * nk:(g + 1) * nk] for g in range(N_GROUPS)]
        s_new = [jnp.sum(q_bd[g] * kvn_refs[g][row, 0:gw], axis=-1, keepdims=True)
                 for g in range(N_GROUPS)]
        m = functools.reduce(jnp.maximum, [jnp.max(a, axis=-1, keepdims=True) for a in s] + s_new)
        p = [jnp.exp(a - m) for a in s]
        p_new = [jnp.exp(a - m) for a in s_new]
        l = sum(jnp.sum(a, axis=-1, keepdims=True) for a in p) + sum(p_new)
        acc = lax.dot_general(jnp.concatenate(p, axis=1).astype(BF16), kc_ref[b, 1],
                              (((1,), (1,)), ((), ())), preferred_element_type=F32)
        for g in range(N_GROUPS):
            acc = acc + p_new[g] * kvn_refs[g][row, gw:2 * gw]
        o_ref[row, :] = jnp.sum(jnp.where(diag, acc / l, 0.0), axis=0, keepdims=True)
        return carry

    lax.fori_loop(0, bb, body, 0, unroll=4)


def _sample_attn(q, kvn, kc, bb):
    n = q.shape[0]
    gw = HEADS * HEAD_DIM
    rows = lambda width: pl.BlockSpec((bb, width), lambda i: (i, 0))
    return pl.pallas_call(
        functools.partial(_sample_attn_kernel, bb),
        grid=(n // bb,),
        in_specs=[rows(q.shape[1])] + [rows(2 * gw)] * N_GROUPS
                 + [pl.BlockSpec((bb,) + kc.shape[1:], lambda i: (i, 0, 0, 0))],
        out_specs=rows(gw),
        out_shape=jax.ShapeDtypeStruct((n, gw), F32),
        compiler_params=_cparams(1), name="sample_attn",
    )(q, *kvn, kc)


def _rot_tables(pos):
    half = ROT_DIM // 2
    lane = lax.broadcasted_iota(jnp.int32, (pos.shape[0], LANES), 1) % HEAD_DIM
    inv_freq = ROPE_THETA ** (-(2 * (lane % half)).astype(F32) / ROT_DIM)
    ang = pos.astype(F32)[:, None] * inv_freq
    cos, sin = jnp.cos(ang), jnp.sin(ang)
    cosb = jnp.where(lane < ROT_DIM, cos, 1.0)
    sina = jnp.where(lane < half, -sin, 0.0)
    sinb = jnp.where((lane >= half) & (lane < ROT_DIM), sin, 0.0)
    return cosb, sina, sinb


def kernel(x_prompt, x_sample, state_pool, state_conv, cache_kv_w128, cache_kv_w512, cache_kv_w2048,
           g_pool, w_pool, pool_scale, g_ffn, w_up, conv_w, conv_b, w_down, g_kv, w_kv, k_gain,
           g_attn, w_q, q_gain, w_o):
    b, s, d = x_prompt.shape
    n = x_sample.shape[0]
    depth = w_up.shape[0]
    n_pool = g_pool.shape[0]
    gw = HEADS * HEAD_DIM
    rows3 = lambda a: a.reshape(a.shape[0], 1, a.shape[1])
    lane2 = lambda a: jnp.tile(a, LANES // HEAD_DIM).reshape(1, LANES)
    tm = min(512, s)
    bn = min(64, n)
    bb = min(8, n)

    gf3, cb3 = rows3(g_ffn), rows3(conv_b)
    ffn_mats = {}

    def ffn_w(l):
        if l not in ffn_mats:
            ffn_mats[l] = (w_up[l].astype(BF16), w_down[l].astype(BF16))
        wup_l, wdown_l = ffn_mats[l]
        return [(gf3, l), (wup_l, None), (conv_w, l), (cb3, l), (wdown_l, None)]
    pool_w = (rows3(g_pool), w_pool.astype(BF16), rows3(pool_scale))
    wkv_b, wq_b, wo_b = w_kv.astype(BF16), w_q.astype(BF16), w_o.astype(BF16)
    q_scale = HEAD_DIM ** -0.5

    def proj_w(pos):
        cosb, sina, sinb = _rot_tables(pos)
        kg = lane2(k_gain)
        kv_w = ((cosb * kg, sina, sinb), g_kv.reshape(1, d), wkv_b, kg, None)
        q_w = []
        for j in range(depth - n_pool):
            qg = lane2(q_gain[j])
            q_w.append(((cosb * (qg * q_scale), sina * q_scale, sinb * q_scale), rows3(g_attn), wq_b, qg, j))
        return kv_w, q_w

    kv_w_p, q_w_p = proj_w(jnp.arange(s, dtype=jnp.int32))
    kv_w_s, q_w_s = proj_w(jnp.full((n,), PAST_LEN, jnp.int32))

    c_args, sels = _side_plan((cache_kv_w128, cache_kv_w512, cache_kv_w2048))
    units = {"done": 0, "kc": None}

    def side_for(steps, per_step=1, use_mxu=False):
        if units["done"] + steps * per_step > 2 * n:
            return None
        side = _Side(c_args, sels, units["done"], units["kc"], per_step, use_mxu)
        units["done"] += steps * per_step
        return side

    def took(kc):
        if kc is not None:
            units["kc"] = kc

    xp = x_prompt
    pool_p, conv_p = [], []
    n_steps = b * (s // tm)
    can_cast = _cast_rows(d, w_down.shape[1], n_steps) is not None
    kvt_groups = [(g, min(win, s)) for g, (win, _) in enumerate(ATTN_GROUPS) if win >= s or win <= tm]
    kvs_p = kvts_p = None
    for l in range(depth):
        cast = lambda nl: ((w_up, w_down), nl) if can_cast and nl < depth else None
        if l < n_pool:
            (xp, ptail, ctail, *nxt), kc = _prompt_layer("pool", xp, pool_w, l, ffn_w(l), tm,
                                                         side_for(n_steps, 1), cast(l + 1))
            took(kc)
            if nxt:
                ffn_mats[l + 1] = tuple(nxt)
            pool_p.append(ptail[:, 2 * SUBLANES - POOL_HIST:])
        else:
            j = l - n_pool
            x2d = xp.reshape(b * s, d)
            if j == 0:
                res, kc = _proj(x2d, s // tm, kv_w_p, q_w_p[j], tm, kvt_groups, None, cast(l + 1))
                kvs_p, q_p = [a.reshape(b, s, 2 * gw) for a in res[:N_GROUPS]], res[N_GROUPS]
                kvts_p = dict(zip((g for g, _ in kvt_groups), res[N_GROUPS + 1:]))
                if cast(l + 1):
                    ffn_mats[l + 1] = tuple(res[N_GROUPS + 1 + len(kvt_groups):])
            else:
                (q_p,), kc = _proj(x2d, s // tm, None, q_w_p[j], tm, (), side_for(n_steps, 2, True))
            took(kc)
            o_p, kc = _prompt_attn(q_p.reshape(b, s, N_GROUPS * gw), kvs_p, None)
            took(kc)
            (xp, ctail), kc = _prompt_layer("attn", xp, (o_p, wo_b), j, ffn_w(l), tm, side_for(n_steps, 2))
            took(kc)
        conv_p.append(ctail[:, SUBLANES - (CONV_W - 1):])
    if units["done"] < 2 * n:
        took(_compact_rest(_Side(c_args, sels, units["done"], units["kc"], 1, True), 2 * n - units["done"]))
    kc = units["kc"]

    xs = x_sample.reshape(n, d)
    hist_t = jnp.swapaxes(state_conv, 1, 2)
    st_t = jnp.swapaxes(state_pool, 1, 2)
    kvs_s = nst = nhist = None
    for l in range(depth):
        if l < n_pool:
            xs, nst, nhist = _sample_layer("pool", xs, (st_t,) + pool_w, l, ffn_w(l), l, hist_t, bn,
                                           (nst, nhist))
        else:
            j = l - n_pool
            if j == 0:
                (*kvs_s, q_s), _ = _proj(xs, 1, kv_w_s, q_w_s[j], n)
            else:
                (q_s,), _ = _proj(xs, 1, None, q_w_s[j], n)
            o_s = _sample_attn(q_s, kvs_s, kc, bb)
            xs, nhist = _sample_layer("attn", xs, (o_s, wo_b), j, ffn_w(l), l, hist_t, bn, (nhist,))

    outs = [xp, xs.reshape(n, 1, d), jnp.stack(pool_p), jnp.swapaxes(nst, 1, 2),
            jnp.stack(conv_p), jnp.swapaxes(nhist, 1, 2)]
    for g, (win, _) in enumerate(ATTN_GROUPS):
        if g in kvts_p:
            rows_p = jnp.transpose(kvts_p[g].reshape(b, 2, HEADS, HEAD_DIM, min(win, s)), (0, 4, 1, 2, 3))
        else:
            rows_p = kvs_p[g][:, s - min(win, s):].reshape(b, min(win, s), 2, HEADS, HEAD_DIM)
        outs += [rows_p, kvs_s[g].reshape(n, 1, 2, HEADS, HEAD_DIM)]
    return tuple(outs)
```

```python
import collections
import functools

import jax
import jax.numpy as jnp
from jax import lax
from jax.experimental import pallas as pl
from jax.experimental.pallas import tpu as pltpu

F32 = jnp.float32
BF16 = jnp.bfloat16

NORM_EPS = 1e-6
NEG_INF = -1e30
POOL_WINDOWS = (2, 4, 8, 16)
POOL_HIST = max(POOL_WINDOWS) - 1
CONV_W = 3
ATTN_GROUPS = ((128, 1), (512, 4), (2048, 16))
N_GROUPS = len(ATTN_GROUPS)
HEADS = 8
HEAD_DIM = 64
ROT_DIM = HEAD_DIM // 4
ROPE_THETA = 500000.0
ATTN_BLOCK = 128
PAST_LEN = 2048

LANES = 128
SUBLANES = 8
VMEM_LIMIT = 62 * 1024 * 1024


def _cparams(n_grid):
    return pltpu.CompilerParams(dimension_semantics=("arbitrary",) * n_grid,
                                vmem_limit_bytes=VMEM_LIMIT)


def _const_spec(shape):
    nd = len(shape)
    return pl.BlockSpec(shape, lambda *_: (0,) * nd, pipeline_mode=pl.Buffered(1))


def _layer_spec(stacked, l):
    if l is None:
        return _const_spec(stacked.shape)
    nd = stacked.ndim - 1
    return pl.BlockSpec((None,) + stacked.shape[1:], lambda *_: (l,) + (0,) * nd,
                        pipeline_mode=pl.Buffered(1))


def _rms(x, g):
    ms = jnp.mean(x * x, axis=-1, keepdims=True)
    return x * lax.rsqrt(ms + NORM_EPS) * g


def _silu(a):
    return a / (1.0 + jnp.exp(-a))


_Side = collections.namedtuple("_Side", "c_args sels base kc per_step use_mxu")

COMPACT_ROWS = 64


def _compact_tasks(c_refs, sel_refs, o_ref, use_mxu):
    nk = c_refs[0].shape[-1]
    n_rows = c_refs[0].shape[1]
    assert nk == LANES and n_rows % COMPACT_ROWS == 0
    tasks = []

    def cast_task(u):
        def run():
            o_ref[u, :, 0:nk] = c_refs[0][u].astype(BF16)
        return run

    def chunk_task(u, g, r0):
        rows, out_cols = slice(r0, r0 + COMPACT_ROWS), slice(g * nk, (g + 1) * nk)
        dil = c_refs[g].shape[-1] // nk

        def run():
            if use_mxu:
                o_ref[u, rows, out_cols] = jnp.dot(c_refs[g][u, rows, :].astype(BF16), sel_refs[g - 1][...],
                                                   preferred_element_type=F32).astype(BF16)
                return
            keep = lax.broadcasted_iota(jnp.int32, (COMPACT_ROWS, LANES), 1) % dil == 0
            acc = jnp.where(keep, c_refs[g][u, rows, 0:LANES], 0.0)
            for j in range(1, dil):
                tile = jnp.where(keep, c_refs[g][u, rows, j * LANES:(j + 1) * LANES], 0.0)
                acc = acc + pltpu.roll(tile, j, 1)
            o_ref[u, rows, out_cols] = acc.astype(BF16)
        return run

    for u in range(o_ref.shape[0]):
        tasks.append(cast_task(u))
        for g in range(1, len(c_refs)):
            tasks += [chunk_task(u, g, r0) for r0 in range(0, n_rows, COMPACT_ROWS)]
    return tasks


def _drain(tasks, n_slots, i):
    for task in tasks[len(tasks) * i // n_slots:len(tasks) * (i + 1) // n_slots]:
        task()


def _side_plan(caches):
    n = caches[0].shape[0]
    gw = HEADS * HEAD_DIM
    c_args, sels = [], []
    for c, (win, dil) in zip(caches, ATTN_GROUPS):
        n_buf, n_back = c.shape[1], win // dil
        assert n_buf == n_back * dil, "window buffer must hold exactly n_back dilated rows"
        c_args.append(jnp.transpose(c, (0, 2, 3, 4, 1)).reshape(n, 2, gw, n_buf))
        if dil > 1:
            t = lax.broadcasted_iota(jnp.int32, (n_buf, n_back), 0)
            m = lax.broadcasted_iota(jnp.int32, (n_buf, n_back), 1)
            sels.append((t == dil * m).astype(BF16))
    return c_args, sels


def _side_io(side, step_of):
    gw = HEADS * HEAD_DIM
    n = side.c_args[0].shape[0]
    n_keys = N_GROUPS * (ATTN_GROUPS[0][0] // ATTN_GROUPS[0][1])
    assert side.per_step in (1, 2) and side.base % side.per_step == 0
    unit = lambda *idx: side.base + side.per_step * step_of(*idx)
    blk_idx = lambda *idx: (unit(*idx) // 2, (unit(*idx) % 2) // side.per_step, 0, 0)
    c_specs = [pl.BlockSpec((None, side.per_step, gw, c.shape[3]), blk_idx) for c in side.c_args]
    sels = side.sels if side.use_mxu else []
    in_specs = c_specs + [_const_spec(s_.shape) for s_ in sels]
    args = list(side.c_args) + list(sels)
    if side.kc is not None:
        in_specs.append(pl.BlockSpec(memory_space=pl.ANY))
        args.append(side.kc)
    out_spec = pl.BlockSpec((None, side.per_step, gw, n_keys), blk_idx)
    out_shape = jax.ShapeDtypeStruct((n, 2, gw, n_keys), BF16)
    return in_specs, args, out_spec, out_shape


def _split_side(refs, n_in, n_out, side):
    ins, rest = refs[:n_in], refs[n_in:]
    side_in = ()
    if side is not None:
        k = N_GROUPS + (len(side.sels) if side.use_mxu else 0) + (side.kc is not None)
        side_in, rest = rest[:k], rest[k:]
    outs, rest = rest[:n_out], rest[n_out:]
    side_out = None
    if side is not None:
        side_out, rest = rest[0], rest[1:]
    return ins, side_in, outs, side_out, rest


def _side_tasks(side, side_in, side_out):
    if side is None:
        return []
    n_sel = len(side.sels) if side.use_mxu else 0
    return _compact_tasks(side_in[:N_GROUPS], side_in[N_GROUPS:N_GROUPS + n_sel], side_out, side.use_mxu)


def _host_call(kernel, side, step_of, n_in, n_out, *, grid, in_specs, out_specs, out_shape, args,
               scratch_shapes, name):
    aliases = {}
    if side is not None:
        s_in, s_args, s_out, s_shape = _side_io(side, step_of)
        if side.kc is not None:
            aliases = {len(in_specs) + len(s_in) - 1: len(out_specs)}
        in_specs, args = in_specs + s_in, args + s_args
        out_specs, out_shape = out_specs + [s_out], out_shape + [s_shape]
    assert len(in_specs) == n_in + (0 if side is None else len(s_in))
    res = pl.pallas_call(
        functools.partial(kernel, side), grid=grid, in_specs=in_specs, out_specs=out_specs,
        out_shape=out_shape, scratch_shapes=scratch_shapes, input_output_aliases=aliases,
        compiler_params=_cparams(len(grid)), name=name,
    )(*args)
    return (res[:n_out], res[n_out]) if side is not None else (res, None)


def _ffn_core(x1, gf_ref, wup_ref, cw_ref, cb_ref, wdown_ref, hbuf, act, fc, rows, side_tasks=()):
    n_blocks, to_rows, prev_rows, from_rows = rows
    d_ff = wdown_ref.shape[0]
    q = hbuf.shape[0] // n_blocks
    blocks = [slice(j * q, (j + 1) * q) for j in range(n_blocks)]
    h = _rms(x1, gf_ref[...])
    for j, rb in enumerate(blocks):
        hbuf[rb, :] = to_rows(h, j).astype(BF16)
    for c in range(d_ff // fc):
        halves = []
        for base in (0, d_ff):
            cols = slice(base + c * fc, base + (c + 1) * fc)
            u = jnp.dot(hbuf[...], wup_ref[:, cols], preferred_element_type=F32)
            ub = [u[rb] for rb in blocks]
            prev = prev_rows(cols, ub)
            halves.append([cb_ref[:, cols] + ((cw_ref[0:1, cols] * u2 + cw_ref[1:2, cols] * u1)
                                              + cw_ref[2:3, cols] * u0)
                           for u0, (u1, u2) in zip(ub, prev)])
        for rb, a, b in zip(blocks, *halves):
            act[rb, c * fc:(c + 1) * fc] = (_silu(a) * b).astype(BF16)
        _drain(side_tasks, d_ff // fc, c)
    return x1 + from_rows(jnp.dot(act[...], wdown_ref[...], preferred_element_type=F32))


PROMPT_ROW_BLOCKS = 4


def _prompt_rows(slab, sbuf, ucarry, tm):
    nb = PROMPT_ROW_BLOCKS
    q = tm // nb
    n_slab = slab.shape[0]

    def to_rows(h, j):
        if j == 0:
            for k in range(n_slab):
                slab[k] = h[:, k * LANES:(k + 1) * LANES]
        return jnp.concatenate([slab[k, pl.ds(j, q, stride=nb), :] for k in range(n_slab)], axis=1)

    def shift_down(slot, blk, carry_row):
        sbuf[slot, SUBLANES - 1:SUBLANES, :] = carry_row
        sbuf[slot, SUBLANES:SUBLANES + q, :] = blk
        return sbuf[slot, SUBLANES - 1:SUBLANES - 1 + q, :]

    def prev_rows(cols, ub):
        s3 = shift_down(0, ub[3], ucarry[SUBLANES - 1:SUBLANES, cols])
        s2 = shift_down(1, ub[2], ucarry[SUBLANES - 2:SUBLANES - 1, cols])
        ucarry[SUBLANES - 1:SUBLANES, cols] = ub[3][q - 1:q]
        ucarry[SUBLANES - 2:SUBLANES - 1, cols] = ub[2][q - 1:q]
        return [(s3, s2), (ub[0], s3), (ub[1], ub[0]), (ub[2], ub[1])]

    def from_rows(y):
        for j in range(nb):
            for k in range(n_slab):
                slab[k, pl.ds(j, q, stride=nb), :] = y[j * q:(j + 1) * q, k * LANES:(k + 1) * LANES]
        return jnp.concatenate([slab[k] for k in range(n_slab)], axis=1)

    return nb, to_rows, prev_rows, from_rows


def _sample_rows(hist_ref, newhist_ref):
    def prev_rows(cols, ub):
        newhist_ref[0, :, cols] = hist_ref[1, :, cols]
        newhist_ref[1, :, cols] = ub[0]
        return [(hist_ref[1, :, cols], hist_ref[0, :, cols])]
    return 1, (lambda h, j: h), prev_rows, (lambda y: y)


def _pool_mix(xn, window_sum, cnt_of, wp_ref, ps_ref):
    gw = xn.shape[1] // len(POOL_WINDOWS)
    parts = []
    for g, w in enumerate(POOL_WINDOWS):
        sl = slice(g * gw, (g + 1) * gw)
        cur = xn[:, sl]
        d = window_sum(w, sl, cur) / cnt_of(w, cur.shape) - cur
        parts.append(jnp.dot(d.astype(BF16), wp_ref[g], preferred_element_type=F32))
    return jnp.concatenate(parts, axis=-1) * ps_ref[...]


POOL_PAD = 2 * SUBLANES


def _prompt_window_sum(xnbuf, lv, tm):
    hist = 2 * SUBLANES
    span = hist + tm

    def window_sum(w, sl, cur):
        s = xnbuf[POOL_PAD:POOL_PAD + span, sl] + xnbuf[POOL_PAD - 1:POOL_PAD - 1 + span, sl]
        shift, slot = 2, 0
        while shift < w:
            lv[slot, POOL_PAD:POOL_PAD + span, :] = s
            s = lv[slot, POOL_PAD:POOL_PAD + span, :] + lv[slot, POOL_PAD - shift:POOL_PAD - shift + span, :]
            shift, slot = 2 * shift, 1 - slot
        return s[hist:]
    return window_sum


def _prompt_layer_kernel(mode, tm, fc, n_cast, side, *refs):
    ins, side_in, outs, side_out, scratch = _split_side(refs, (9 if mode == "pool" else 8) + n_cast,
                                                        (3 if mode == "pool" else 2) + n_cast, side)
    tasks = _side_tasks(side, side_in, side_out)
    if n_cast:
        (ins, cast_in), (outs, cast_out) = (ins[:-n_cast], ins[-n_cast:]), (outs[:-n_cast], outs[-n_cast:])
        tasks = tasks + _cast_tasks(cast_in, cast_out)
    if mode == "pool":
        x_ref, gp_ref, wp_ref, ps_ref, gf_ref, wup_ref, cw_ref, cb_ref, wdown_ref = ins
        xo_ref, ptail_ref, ctail_ref = outs
        xnbuf, lv, hbuf, act, slab, sbuf, ucarry = scratch
    else:
        x_ref, o_ref, wo_ref, gf_ref, wup_ref, cw_ref, cb_ref, wdown_ref = ins
        xo_ref, ctail_ref = outs
        hbuf, act, slab, sbuf, ucarry = scratch
    t = pl.program_id(1)
    hist = 2 * SUBLANES

    @pl.when(t == 0)
    def _():
        ucarry[...] = jnp.zeros_like(ucarry)
        if mode == "pool":
            xnbuf[0:POOL_PAD + hist, :] = jnp.zeros((POOL_PAD + hist, xnbuf.shape[1]), F32)
            lv[:, 0:POOL_PAD, :] = jnp.zeros((lv.shape[0], POOL_PAD, lv.shape[2]), F32)

    x = x_ref[...]
    if mode == "pool":
        xn = _rms(x, gp_ref[...])
        xnbuf[POOL_PAD + hist:POOL_PAD + hist + tm, :] = xn

        def cnt_of(w, shape):
            pos = t * tm + lax.broadcasted_iota(jnp.int32, shape, 0)
            return jnp.minimum(pos + 1, w).astype(F32)

        x1 = x + _pool_mix(xn, _prompt_window_sum(xnbuf, lv, tm), cnt_of, wp_ref, ps_ref)
        tail = xnbuf[POOL_PAD + tm:POOL_PAD + tm + hist, :]
        ptail_ref[...] = tail
        xnbuf[POOL_PAD:POOL_PAD + hist, :] = tail
    else:
        x1 = x + jnp.dot(o_ref[...].astype(BF16), wo_ref[...], preferred_element_type=F32)

    xo_ref[...] = _ffn_core(x1, gf_ref, wup_ref, cw_ref, cb_ref, wdown_ref, hbuf, act, fc,
                            _prompt_rows(slab, sbuf, ucarry, tm), tasks)
    ctail_ref[...] = ucarry[...]


def _sample_layer_kernel(mode, fc, n_chained, *refs):
    n_in = 11 if mode == "pool" else 9
    refs = refs[:n_in] + refs[n_in + n_chained:]
    if mode == "pool":
        (x_ref, st_ref, gp_ref, wp_ref, ps_ref, gf_ref, wup_ref, cw_ref, cb_ref, wdown_ref,
         hist_ref, xo_ref, nst_ref, nhist_ref, hbuf, act) = refs
    else:
        (x_ref, o_ref, wo_ref, gf_ref, wup_ref, cw_ref, cb_ref, wdown_ref,
         hist_ref, xo_ref, nhist_ref, hbuf, act) = refs
    x = x_ref[...]
    if mode == "pool":
        xn = _rms(x, gp_ref[...])
        n_hist = st_ref.shape[0]

        def window_sum(w, sl, cur):
            s = cur
            for k in range(1, w):
                s = s + st_ref[n_hist - k, :, sl]
            return s

        def cnt_of(w, shape):
            return float(min(PAST_LEN + 1, w))

        x1 = x + _pool_mix(xn, window_sum, cnt_of, wp_ref, ps_ref)
        for i in range(n_hist - 1):
            nst_ref[i] = st_ref[i + 1]
        nst_ref[n_hist - 1] = xn
    else:
        x1 = x + jnp.dot(o_ref[...].astype(BF16), wo_ref[...], preferred_element_type=F32)
    xo_ref[...] = _ffn_core(x1, gf_ref, wup_ref, cw_ref, cb_ref, wdown_ref, hbuf, act, fc,
                            _sample_rows(hist_ref, nhist_ref))


MXU_TILE = 256


def _pick_fc(f):
    for fc in (MXU_TILE, MXU_TILE // 2):
        if f % fc == 0:
            return fc
    return f


def _cast_rows(d, f, steps):
    if steps % 2 or d % steps or f % (steps // 2):
        return None
    ru, rd = d // steps, f // (steps // 2)
    return (ru, rd) if ru % (2 * SUBLANES) == 0 and rd % (2 * SUBLANES) == 0 else None


def _cast_io(cast, d, f, steps, step_of):
    weights, nl = cast
    in_specs, args, out_specs, out_shape = [], [], [], []
    for w_f, (rows, span) in zip(weights, zip(_cast_rows(d, f, steps), (1, 2))):
        blk = lambda *idx, span=span: step_of(*idx) // span
        in_specs.append(pl.BlockSpec((None, rows, w_f.shape[2]), lambda *idx, blk=blk: (nl, blk(*idx), 0)))
        args.append(w_f)
        out_specs.append(pl.BlockSpec((rows, w_f.shape[2]), lambda *idx, blk=blk: (blk(*idx), 0)))
        out_shape.append(jax.ShapeDtypeStruct(w_f.shape[1:], BF16))
    return in_specs, args, out_specs, out_shape


def _cast_tasks(cast_in, cast_out):
    def task(src, dst):
        def run():
            dst[...] = src[...].astype(BF16)
        return run
    return [task(s_, d_) for s_, d_ in zip(cast_in, cast_out)]


def _prompt_layer(mode, x, pre, pre_l, ffn_w, tm, side, cast=None):
    b, s, d = x.shape
    f2, f = ffn_w[1][0].shape[-1], ffn_w[4][0].shape[-2]
    fc = _pick_fc(f)
    row = pl.BlockSpec((None, tm, d), lambda i, j: (i, j, 0))
    hist = 2 * SUBLANES
    ctail_spec = pl.BlockSpec((None, SUBLANES, f2), lambda i, j: (i, 0, 0))
    ctail_shape = jax.ShapeDtypeStruct((b, SUBLANES, f2), F32)
    common_scratch = [pltpu.VMEM((tm, d), BF16), pltpu.VMEM((tm, f), BF16),
                      pltpu.VMEM((d // LANES, tm, LANES), F32),
                      pltpu.VMEM((2, tm // PROMPT_ROW_BLOCKS + SUBLANES, fc), F32),
                      pltpu.VMEM((SUBLANES, f2), F32)]
    if mode == "pool":
        in_specs = [row] + [_layer_spec(a, pre_l) for a in pre]
        args = [x, *pre]
        out_specs = [row, pl.BlockSpec((None, hist, d), lambda i, j: (i, 0, 0)), ctail_spec]
        out_shape = [jax.ShapeDtypeStruct(x.shape, F32), jax.ShapeDtypeStruct((b, hist, d), F32),
                     ctail_shape]
        ext = POOL_PAD + hist + tm
        scratch = [pltpu.VMEM((ext, d), F32),
                   pltpu.VMEM((2, ext, d // len(POOL_WINDOWS)), F32)] + common_scratch
    else:
        o, wo = pre
        in_specs = [row, pl.BlockSpec((None, tm, o.shape[2]), lambda i, j: (i, j, 0)),
                    _layer_spec(wo, pre_l)]
        args = [x, o, wo]
        out_specs = [row, ctail_spec]
        out_shape = [jax.ShapeDtypeStruct(x.shape, F32), ctail_shape]
        scratch = common_scratch
    nj = s // tm
    step_of = lambda i, j: i * nj + j
    in_specs = in_specs + [_layer_spec(a, l) for a, l in ffn_w]
    args = args + [a for a, _ in ffn_w]
    n_cast = 0
    if cast is not None:
        c_in, c_args, c_out, c_shape = _cast_io(cast, d, f, b * nj, step_of)
        in_specs, args = in_specs + c_in, args + c_args
        out_specs, out_shape = out_specs + c_out, out_shape + c_shape
        n_cast = len(c_in)
    return _host_call(
        functools.partial(_prompt_layer_kernel, mode, tm, fc, n_cast), side, step_of,
        len(in_specs), len(out_specs),
        grid=(b, nj), in_specs=in_specs, out_specs=out_specs, out_shape=out_shape, args=args,
        scratch_shapes=scratch, name=f"prompt_{mode}_layer")


def _sample_layer(mode, x, pre, pre_l, ffn_w, l, hist_t, bn, chained):
    n, d = x.shape
    f2, f = ffn_w[1][0].shape[-1], ffn_w[4][0].shape[-2]
    fc = _pick_fc(f)
    row = lambda width: pl.BlockSpec((bn, width), lambda i: (i, 0))
    tmaj = lambda a, k: pl.BlockSpec((None, a.shape[1], bn, a.shape[3]), lambda i: (k, 0, i, 0))
    if mode == "pool":
        st_t = pre[0]
        in_specs = [row(d), tmaj(st_t, pre_l)] + [_layer_spec(a, pre_l) for a in pre[1:]]
        args = [x, *pre]
        states = [(st_t, pre_l), (hist_t, l)]
    else:
        o, wo = pre
        in_specs = [row(d), row(o.shape[1]), _layer_spec(wo, pre_l)]
        args = [x, o, wo]
        states = [(hist_t, l)]
    in_specs = in_specs + [_layer_spec(a, k) for a, k in ffn_w] + [tmaj(hist_t, l)]
    args = args + [a for a, _ in ffn_w] + [hist_t]
    out_specs = [row(d)] + [tmaj(a, k) for a, k in states]
    out_shape = [jax.ShapeDtypeStruct(x.shape, F32)] + [jax.ShapeDtypeStruct(a.shape, F32) for a, _ in states]
    aliases = {}
    for k, prev in enumerate(chained):
        if prev is not None:
            aliases[len(in_specs)] = 1 + k
            in_specs.append(pl.BlockSpec(memory_space=pl.ANY))
            args.append(prev)
    return pl.pallas_call(
        functools.partial(_sample_layer_kernel, mode, fc, len(aliases)),
        grid=(n // bn,), in_specs=in_specs, out_specs=out_specs, out_shape=out_shape,
        scratch_shapes=[pltpu.VMEM((bn, d), BF16), pltpu.VMEM((bn, f), BF16)],
        input_output_aliases=aliases,
        compiler_params=_cparams(1), name=f"sample_{mode}_layer",
    )(*args)


HEAD_TILE = 2 * LANES


def _head_norm_rot(x, bd_ref, gain, tabs):
    gcos, sina, sinb = tabs
    ms = jnp.dot((x * x).astype(BF16), bd_ref[...], preferred_element_type=F32)
    outs = []
    for j in range(x.shape[1] // LANES):
        xs = x[:, j * LANES:(j + 1) * LANES]
        xg = xs * gain
        outs.append(xs * gcos + pltpu.roll(xg, LANES - ROT_DIM // 2, 1) * sina
                    + pltpu.roll(xg, ROT_DIM // 2, 1) * sinb)
    return jnp.concatenate(outs, axis=1) * lax.rsqrt(ms + NORM_EPS)


def _proj_kernel(do_kv, kvt_groups, n_tab_blocks, n_cast, side, *refs):
    n_out = ((N_GROUPS + 1 + len(kvt_groups)) if do_kv else 1) + n_cast
    ins, side_in, outs, side_out, _ = _split_side(refs, (14 if do_kv else 8) + n_cast, n_out, side)
    side_tasks = _side_tasks(side, side_in, side_out)
    if n_cast:
        side_tasks = side_tasks + _cast_tasks(ins[-n_cast:], outs[-n_cast:])
        ins, outs = ins[:-n_cast], outs[:-n_cast]
    x_ref, bd_ref = ins[0:2]
    if do_kv:
        ktabs, (gkv_ref, wkv_ref, kg_ref) = ins[2:5], ins[5:8]
        qtabs, (gq_ref, wq_ref, qg_ref) = ins[8:11], ins[11:14]
        kv_refs, q_ref, kvt_refs = outs[0:N_GROUPS], outs[N_GROUPS], outs[N_GROUPS + 1:]
    else:
        qtabs, (gq_ref, wq_ref, qg_ref) = ins[2:5], ins[5:8]
        q_ref = outs[0]
    x = x_ref[...]
    gw = HEADS * HEAD_DIM
    if do_kv:
        h = _rms(x, gkv_ref[...]).astype(BF16)
        kv = jnp.dot(h, wkv_ref[...], preferred_element_type=F32)
        tabs = [t[...] for t in ktabs]
        for g, out_ref in enumerate(kv_refs):
            for j in range(gw // HEAD_TILE):
                c0 = g * gw + j * HEAD_TILE
                out_ref[:, j * HEAD_TILE:(j + 1) * HEAD_TILE] = _head_norm_rot(
                    kv[:, c0:c0 + HEAD_TILE], bd_ref, kg_ref[...], tabs)
            v0 = N_GROUPS * gw + g * gw
            out_ref[:, gw:2 * gw] = kv[:, v0:v0 + gw]
        tm = x.shape[0]
        for (g, win), kvt_ref in zip(kvt_groups, kvt_refs):
            if win == n_tab_blocks * tm:
                kvt_ref[...] = kv_refs[g][...].T
            else:
                @pl.when(pl.program_id(0) % n_tab_blocks == n_tab_blocks - 1)
                def _(g=g, win=win, kvt_ref=kvt_ref):
                    kvt_ref[...] = kv_refs[g][tm - win:tm, :].T
    h = _rms(x, gq_ref[...]).astype(BF16)
    q = jnp.dot(h, wq_ref[...], preferred_element_type=F32)
    tabs = [t[...] for t in qtabs]
    n_tiles = q.shape[1] // HEAD_TILE
    for j in range(n_tiles):
        cols = slice(j * HEAD_TILE, (j + 1) * HEAD_TILE)
        q_ref[:, cols] = _head_norm_rot(q[:, cols], bd_ref, qg_ref[...], tabs)
        _drain(side_tasks, n_tiles, j)


def _proj(x2d, n_tab_blocks, kv_w, q_w, tm, kvt_groups=(), side=None, cast=None):
    r, d = x2d.shape
    row = pl.BlockSpec((tm, d), lambda i: (i, 0))
    tab = pl.BlockSpec((tm, LANES), lambda i: (i % n_tab_blocks, 0))
    gw = HEADS * HEAD_DIM
    head = lax.broadcasted_iota(jnp.int32, (HEAD_TILE, HEAD_TILE), 0) // HEAD_DIM
    bd = jnp.where(head == head.T, 1.0 / HEAD_DIM, 0.0).astype(BF16)
    in_specs, args = [row, _const_spec(bd.shape)], [x2d, bd]
    out_specs, out_shape = [], []
    for w in (kv_w, q_w):
        if w is None:
            continue
        tabs, g, wmat, hg, layer = w
        wspec = lambda a, layer=layer: _const_spec(a.shape) if layer is None else _layer_spec(a, layer)
        in_specs += [tab, tab, tab, wspec(g), wspec(wmat), _const_spec((1, LANES))]
        args += [*tabs, g, wmat, hg]
    if kv_w is not None:
        out_specs += [pl.BlockSpec((tm, 2 * gw), lambda i: (i, 0))] * N_GROUPS
        out_shape += [jax.ShapeDtypeStruct((r, 2 * gw), F32)] * N_GROUPS
    n_q = q_w[2].shape[-1]
    out_specs.append(pl.BlockSpec((tm, n_q), lambda i: (i, 0)))
    out_shape.append(jax.ShapeDtypeStruct((r, n_q), F32))
    seq = n_tab_blocks * tm
    if kv_w is not None:
        for _, win in kvt_groups:
            assert win == seq or win <= tm
            if win == seq:
                out_specs.append(pl.BlockSpec((None, 2 * gw, tm), lambda i: (i // n_tab_blocks, 0, i % n_tab_blocks)))
            else:
                out_specs.append(pl.BlockSpec((None, 2 * gw, win), lambda i: (i // n_tab_blocks, 0, 0)))
            out_shape.append(jax.ShapeDtypeStruct((r // seq, 2 * gw, win), F32))
    n_cast = 0
    if cast is not None:
        c_in, c_args, c_out, c_shape = _cast_io(cast, d, cast[0][1].shape[1], r // tm, lambda i: i)
        in_specs, args = in_specs + c_in, args + c_args
        out_specs, out_shape = out_specs + c_out, out_shape + c_shape
        n_cast = len(c_in)
    return _host_call(
        functools.partial(_proj_kernel, kv_w is not None, tuple(kvt_groups), n_tab_blocks, n_cast), side,
        lambda i: i, len(in_specs), len(out_specs),
        grid=(r // tm,), in_specs=in_specs, out_specs=out_specs, out_shape=out_shape, args=args,
        scratch_shapes=[], name="proj_kvq" if kv_w is not None else "proj_q")


def _prompt_attn_kernel(seq, side, *refs):
    ins, side_in, outs, side_out, scratch = _split_side(refs, 3 * N_GROUPS, 1, side)
    q_refs, k_refs, v_refs = ins[0:3], ins[3:6], ins[6:9]
    o_ref = outs[0]
    qb, kb, vb, tmpf, accs, ms, ls = scratch
    blk = ATTN_BLOCK
    lo = lax.broadcasted_iota(jnp.int32, (blk, LANES), 1) < HEAD_DIM
    quarter = seq // 4
    side_tasks = _side_tasks(side, side_in, side_out)
    n_blocks_total = sum(seq // blk for _ in ATTN_GROUPS)
    done_blocks = 0

    def valid_mask(nk):
        qi = lax.broadcasted_iota(jnp.int32, (2 * blk, nk), 0) & (blk - 1)
        kj = lax.broadcasted_iota(jnp.int32, (2 * blk, nk), 1)
        if nk == blk:
            return kj <= qi
        return lax.bitcast_convert_type(kj - qi, jnp.uint32) <= jnp.uint32(blk)

    for g, (win, dil) in enumerate(ATTN_GROUPS):
        assert win // dil == blk and dil in (1, 4, 16) and seq % (blk * dil) == 0
        sub = seq // dil
        nb = sub // blk

        def stage(src, dil=dil):
            if dil == 16:
                for r1 in range(4):
                    tmpf[r1 * quarter:(r1 + 1) * quarter, :] = src[pl.ds(r1, quarter, stride=4), :]

        def rows_of(src, r, n0, cnt, dil=dil):
            if dil == 1:
                return src[pl.ds(n0, cnt), :]
            if dil == 4:
                return src[pl.ds(r + 4 * n0, cnt, stride=4), :]
            return tmpf[pl.ds((r % 4) * quarter + r // 4 + 4 * n0, cnt, stride=4), :]

        stage(q_refs[g])
        for r in range(dil):
            for n in range(nb):
                qf = rows_of(q_refs[g], r, n * blk, blk)
                base = (r * nb + n) * 2 * blk
                qb[base:base + blk, :] = jnp.where(lo, qf, 0.0).astype(BF16)
                qb[base + blk:base + 2 * blk, :] = jnp.where(lo, 0.0, qf).astype(BF16)
        for src, dst in ((k_refs[g], kb), (v_refs[g], vb)):
            stage(src)
            for r in range(dil):
                dst[r * sub:(r + 1) * sub, :] = rows_of(src, r, 0, sub).astype(BF16)

        def block(r, n, first, g=g, dil=dil, sub=sub, nb=nb):
            q2 = qb[pl.ds((r * nb + n) * 2 * blk, 2 * blk), :]
            nk = blk if first else 2 * blk
            k0 = r * sub + (n if first else n - 1) * blk
            kt = kb[pl.ds(k0, nk), :]
            vt = vb[pl.ds(k0, nk), :]
            s = lax.dot_general(q2, kt, (((1,), (1,)), ((), ())), preferred_element_type=F32)
            s = jnp.where(valid_mask(nk), s, NEG_INF)
            m = jnp.max(s, axis=-1, keepdims=True)
            p = jnp.exp(s - m)
            l = jnp.sum(p, axis=-1, keepdims=True)
            pv = jnp.dot(p.astype(BF16), vt, preferred_element_type=F32)
            if dil > 1:
                out_rows = pl.ds(r + n * blk * dil, blk, stride=dil)
            else:
                out_rows = pl.ds(n * blk, blk)
            accs[g, out_rows, :] = jnp.where(lo, pv[:blk], pv[blk:])
            ms[g, out_rows, :] = jnp.where(lo, m[:blk], m[blk:])
            ls[g, out_rows, :] = jnp.where(lo, l[:blk], l[blk:])

        for r in range(dil):
            for n in range(nb):
                block(r, n, n == 0)
                _drain(side_tasks, n_blocks_total, done_blocks)
                done_blocks += 1

    def merge(i, carry):
        rows = pl.ds(pl.multiple_of(i * blk, blk), blk)
        m = jnp.maximum(jnp.maximum(ms[0, rows, :], ms[1, rows, :]), ms[2, rows, :])
        num = jnp.zeros((blk, LANES), F32)
        den = jnp.zeros((blk, LANES), F32)
        for g in range(N_GROUPS):
            w = jnp.exp(ms[g, rows, :] - m)
            num = num + w * accs[g, rows, :]
            den = den + w * ls[g, rows, :]
        o_ref[rows, :] = (num / den).astype(o_ref.dtype)
        return carry

    lax.fori_loop(0, seq // blk, merge, 0)


def _prompt_attn(q, kvs, side):
    b, s, _ = q.shape
    gw = HEADS * HEAD_DIM
    n_hp = gw // LANES
    blk_spec = lambda col: pl.BlockSpec((None, s, LANES), lambda i, j, col=col: (i, 0, col + j))
    in_specs = ([blk_spec(g * n_hp) for g in range(N_GROUPS)]
                + [blk_spec(0)] * N_GROUPS
                + [blk_spec(n_hp)] * N_GROUPS)
    args = [q] * N_GROUPS + list(kvs) + list(kvs)
    (o,), kc = _host_call(
        functools.partial(_prompt_attn_kernel, s), side, lambda i, j: i * n_hp + j, len(in_specs), 1,
        grid=(b, n_hp), in_specs=in_specs,
        out_specs=[pl.BlockSpec((None, s, LANES), lambda i, j: (i, 0, j))],
        out_shape=[jax.ShapeDtypeStruct((b, s, gw), BF16)], args=args,
        scratch_shapes=[pltpu.VMEM((2 * s, LANES), BF16),
                        pltpu.VMEM((s, LANES), BF16),
                        pltpu.VMEM((s, LANES), BF16),
                        pltpu.VMEM((s, LANES), F32)]
                       + [pltpu.VMEM((N_GROUPS, s, LANES), F32)] * 3,
        name="prompt_attn")
    return o, kc


def _compact_kernel(side, *refs):
    _, side_in, _, side_out, _ = _split_side(refs, 0, 0, side)
    for task in _side_tasks(side, side_in, side_out):
        task()


def _compact_rest(side, n_units):
    _, kc = _host_call(_compact_kernel, side, lambda i: i, 0, 0, grid=(n_units,), in_specs=[],
                       out_specs=[], out_shape=[], args=[], scratch_shapes=[], name="compact_caches")
    return kc


def _sample_attn_kernel(bb, q_ref, kv0_ref, kv1_ref, kv2_ref, kc_ref, o_ref):
    gw = HEADS * HEAD_DIM
    head_of_row = lax.broadcasted_iota(jnp.int32, (HEADS, gw), 0)
    head_of_col = lax.broadcasted_iota(jnp.int32, (HEADS, gw), 1) // HEAD_DIM
    diag = head_of_row == head_of_col
    nk = kc_ref.shape[-1] // N_GROUPS
    kvn_refs = (kv0_ref, kv1_ref, kv2_ref)

    def body(b, carry):
        row = pl.ds(b, 1)
        q_bd = [jnp.where(diag, q_ref[row, g * gw:(g + 1) * gw], 0.0) for g in range(N_GROUPS)]
        s_all = jnp.dot(jnp.concatenate(q_bd, axis=0).astype(BF16), kc_ref[b, 0],
                        preferred_element_type=F32)
        s = [s_all[g * HEADS:(g + 1) * HEADS, g * nk:(g + 1) * nk] for g in range(N_GROUPS)]
        s_new = [jnp.sum(q_bd[g] * kvn_refs[g][row, 0:gw], axis=-1, keepdims=True)
                 for g in range(N_GROUPS)]
        m = functools.reduce(jnp.maximum, [jnp.max(a, axis=-1, keepdims=True) for a in s] + s_new)
        p = [jnp.exp(a - m) for a in s]
        p_new = [jnp.exp(a - m) for a in s_new]
        l = sum(jnp.sum(a, axis=-1, keepdims=True) for a in p) + sum(p_new)
        acc = lax.dot_general(jnp.concatenate(p, axis=1).astype(BF16), kc_ref[b, 1],
                              (((1,), (1,)), ((), ())), preferred_element_type=F32)
        for g in range(N_GROUPS):
            acc = acc + p_new[g] * kvn_refs[g][row, gw:2 * gw]
        o_ref[row, :] = jnp.sum(jnp.where(diag, acc / l, 0.0), axis=0, keepdims=True)
        return carry

    lax.fori_loop(0, bb, body, 0, unroll=True)


def _sample_attn(q, kvn, kc, bb):
    n = q.shape[0]
    gw = HEADS * HEAD_DIM
    rows = lambda width: pl.BlockSpec((bb, width), lambda i: (i, 0))
    return pl.pallas_call(
        functools.partial(_sample_attn_kernel, bb),
        grid=(n // bb,),
        in_specs=[rows(q.shape[1])] + [rows(2 * gw)] * N_GROUPS
                 + [pl.BlockSpec((bb,) + kc.shape[1:], lambda i: (i, 0, 0, 0))],
        out_specs=rows(gw),
        out_shape=jax.ShapeDtypeStruct((n, gw), F32),
        compiler_params=_cparams(1), name="sample_attn",
    )(q, *kvn, kc)


def _rot_tables(pos):
    half = ROT_DIM // 2
    lane = lax.broadcasted_iota(jnp.int32, (pos.shape[0], LANES), 1) % HEAD_DIM
    inv_freq = ROPE_THETA ** (-(2 * (lane % half)).astype(F32) / ROT_DIM)
    ang = pos.astype(F32)[:, None] * inv_freq
    cos, sin = jnp.cos(ang), jnp.sin(ang)
    cosb = jnp.where(lane < ROT_DIM, cos, 1.0)
    sina = jnp.where(lane < half, -sin, 0.0)
    sinb = jnp.where((lane >= half) & (lane < ROT_DIM), sin, 0.0)
    return cosb, sina, sinb


def kernel(x_prompt, x_sample, state_pool, state_conv, cache_kv_w128, cache_kv_w512, cache_kv_w2048,
           g_pool, w_pool, pool_scale, g_ffn, w_up, conv_w, conv_b, w_down, g_kv, w_kv, k_gain,
           g_attn, w_q, q_gain, w_o):
    b, s, d = x_prompt.shape
    n = x_sample.shape[0]
    depth = w_up.shape[0]
    n_pool = g_pool.shape[0]
    gw = HEADS * HEAD_DIM
    rows3 = lambda a: a.reshape(a.shape[0], 1, a.shape[1])
    lane2 = lambda a: jnp.tile(a, LANES // HEAD_DIM).reshape(1, LANES)
    tm = min(512, s)
    bn = min(64, n)
    bb = min(16, n)

    gf3, cb3 = rows3(g_ffn), rows3(conv_b)
    ffn_mats = {}

    def ffn_w(l):
        if l not in ffn_mats:
            ffn_mats[l] = (w_up[l].astype(BF16), w_down[l].astype(BF16))
        wup_l, wdown_l = ffn_mats[l]
        return [(gf3, l), (wup_l, None), (conv_w, l), (cb3, l), (wdown_l, None)]
    pool_w = (rows3(g_pool), w_pool.astype(BF16), rows3(pool_scale))
    wkv_b, wq_b, wo_b = w_kv.astype(BF16), w_q.astype(BF16), w_o.astype(BF16)
    q_scale = HEAD_DIM ** -0.5

    def proj_w(pos):
        cosb, sina, sinb = _rot_tables(pos)
        kg = lane2(k_gain)
        kv_w = ((cosb * kg, sina, sinb), g_kv.reshape(1, d), wkv_b, kg, None)
        q_w = []
        for j in range(depth - n_pool):
            qg = lane2(q_gain[j])
            q_w.append(((cosb * (qg * q_scale), sina * q_scale, sinb * q_scale), rows3(g_attn), wq_b, qg, j))
        return kv_w, q_w

    kv_w_p, q_w_p = proj_w(jnp.arange(s, dtype=jnp.int32))
    kv_w_s, q_w_s = proj_w(jnp.full((n,), PAST_LEN, jnp.int32))

    c_args, sels = _side_plan((cache_kv_w128, cache_kv_w512, cache_kv_w2048))
    units = {"done": 0, "kc": None}

    def side_for(steps, per_step=1, use_mxu=False):
        if units["done"] + steps * per_step > 2 * n:
            return None
        side = _Side(c_args, sels, units["done"], units["kc"], per_step, use_mxu)
        units["done"] += steps * per_step
        return side

    def took(kc):
        if kc is not None:
            units["kc"] = kc

    xp = x_prompt
    pool_p, conv_p = [], []
    n_steps = b * (s // tm)
    can_cast = _cast_rows(d, w_down.shape[1], n_steps) is not None
    kvt_groups = [(g, min(win, s)) for g, (win, _) in enumerate(ATTN_GROUPS) if win >= s or win <= tm]
    kvs_p = kvts_p = None
    for l in range(depth):
        cast = lambda nl: ((w_up, w_down), nl) if can_cast and nl < depth else None
        if l < n_pool:
            (xp, ptail, ctail, *nxt), kc = _prompt_layer("pool", xp, pool_w, l, ffn_w(l), tm,
                                                         side_for(n_steps, 1), cast(l + 1))
            took(kc)
            if nxt:
                ffn_mats[l + 1] = tuple(nxt)
            pool_p.append(ptail[:, 2 * SUBLANES - POOL_HIST:])
        else:
            j = l - n_pool
            x2d = xp.reshape(b * s, d)
            if j == 0:
                res, kc = _proj(x2d, s // tm, kv_w_p, q_w_p[j], tm, kvt_groups, None, cast(l + 1))
                kvs_p, q_p = [a.reshape(b, s, 2 * gw) for a in res[:N_GROUPS]], res[N_GROUPS]
                kvts_p = dict(zip((g for g, _ in kvt_groups), res[N_GROUPS + 1:]))
                if cast(l + 1):
                    ffn_mats[l + 1] = tuple(res[N_GROUPS + 1 + len(kvt_groups):])
            else:
                (q_p,), kc = _proj(x2d, s // tm, None, q_w_p[j], tm, (), side_for(n_steps, 2, True))
            took(kc)
            o_p, kc = _prompt_attn(q_p.reshape(b, s, N_GROUPS * gw), kvs_p, None)
            took(kc)
            (xp, ctail), kc = _prompt_layer("attn", xp, (o_p, wo_b), j, ffn_w(l), tm, side_for(n_steps, 2))
            took(kc)
        conv_p.append(ctail[:, SUBLANES - (CONV_W - 1):])
    if units["done"] < 2 * n:
        took(_compact_rest(_Side(c_args, sels, units["done"], units["kc"], 1, True), 2 * n - units["done"]))
    kc = units["kc"]

    xs = x_sample.reshape(n, d)
    hist_t = jnp.swapaxes(state_conv, 1, 2)
    st_t = jnp.swapaxes(state_pool, 1, 2)
    kvs_s = nst = nhist = None
    for l in range(depth):
        if l < n_pool:
            xs, nst, nhist = _sample_layer("pool", xs, (st_t,) + pool_w, l, ffn_w(l), l, hist_t, bn,
                                           (nst, nhist))
        else:
            j = l - n_pool
            if j == 0:
                (*kvs_s, q_s), _ = _proj(xs, 1, kv_w_s, q_w_s[j], n)
            else:
                (q_s,), _ = _proj(xs, 1, None, q_w_s[j], n)
            o_s = _sample_attn(q_s, kvs_s, kc, bb)
            xs, nhist = _sample_layer("attn", xs, (o_s, wo_b), j, ffn_w(l), l, hist_t, min(2 * bn, n), (nhist,))

    outs = [xp, xs.reshape(n, 1, d), jnp.stack(pool_p), jnp.swapaxes(nst, 1, 2),
            jnp.stack(conv_p), jnp.swapaxes(nhist, 1, 2)]
    for g, (win, _) in enumerate(ATTN_GROUPS):
        if g in kvts_p:
            rows_p = jnp.transpose(kvts_p[g].reshape(b, 2, HEADS, HEAD_DIM, min(win, s)), (0, 4, 1, 2, 3))
        else:
            rows_p = kvs_p[g][:, s - min(win, s):].reshape(b, min(win, s), 2, HEADS, HEAD_DIM)
        outs += [rows_p, kvs_s[g].reshape(n, 1, 2, HEADS, HEAD_DIM)]
    return tuple(outs)
```

```python
import collections
import functools

import jax
import jax.numpy as jnp
from jax import lax
from jax.experimental import pallas as pl
from jax.experimental.pallas import tpu as pltpu

F32 = jnp.float32
BF16 = jnp.bfloat16

NORM_EPS = 1e-6
NEG_INF = -1e30
POOL_WINDOWS = (2, 4, 8, 16)
POOL_HIST = max(POOL_WINDOWS) - 1
CONV_W = 3
ATTN_GROUPS = ((128, 1), (512, 4), (2048, 16))
N_GROUPS = len(ATTN_GROUPS)
HEADS = 8
HEAD_DIM = 64
ROT_DIM = HEAD_DIM // 4
ROPE_THETA = 500000.0
ATTN_BLOCK = 128
PAST_LEN = 2048

LANES = 128
SUBLANES = 8
VMEM_LIMIT = 64 * 1024 * 1024 - 128 * 1024


def _cparams(n_grid):
    return pltpu.CompilerParams(dimension_semantics=("arbitrary",) * n_grid,
                                vmem_limit_bytes=VMEM_LIMIT)


def _const_spec(shape):
    nd = len(shape)
    return pl.BlockSpec(shape, lambda *_: (0,) * nd, pipeline_mode=pl.Buffered(1))


def _layer_spec(stacked, l):
    if l is None:
        return _const_spec(stacked.shape)
    nd = stacked.ndim - 1
    return pl.BlockSpec((None,) + stacked.shape[1:], lambda *_: (l,) + (0,) * nd,
                        pipeline_mode=pl.Buffered(1))


def _rms(x, g):
    ms = jnp.mean(x * x, axis=-1, keepdims=True)
    return x * lax.rsqrt(ms + NORM_EPS) * g


def _silu(a):
    return a / (1.0 + jnp.exp(-a))


_Side = collections.namedtuple("_Side", "c_args sels base kc per_step use_mxu")

COMPACT_ROWS = 64


def _compact_tasks(c_refs, sel_refs, o_ref, use_mxu):
    nk = c_refs[0].shape[-1]
    n_rows = c_refs[0].shape[1]
    assert nk == LANES and n_rows % COMPACT_ROWS == 0
    tasks = []

    def cast_task(u):
        def run():
            o_ref[u, :, 0:nk] = c_refs[0][u].astype(BF16)
        return run

    def chunk_task(u, g, r0):
        rows, out_cols = slice(r0, r0 + COMPACT_ROWS), slice(g * nk, (g + 1) * nk)
        dil = c_refs[g].shape[-1] // nk

        def run():
            if use_mxu:
                o_ref[u, rows, out_cols] = jnp.dot(c_refs[g][u, rows, :].astype(BF16), sel_refs[g - 1][...],
                                                   preferred_element_type=F32).astype(BF16)
                return
            keep = lax.broadcasted_iota(jnp.int32, (COMPACT_ROWS, LANES), 1) % dil == 0
            acc = jnp.where(keep, c_refs[g][u, rows, 0:LANES], 0.0)
            for j in range(1, dil):
                tile = jnp.where(keep, c_refs[g][u, rows, j * LANES:(j + 1) * LANES], 0.0)
                acc = acc + pltpu.roll(tile, j, 1)
            o_ref[u, rows, out_cols] = acc.astype(BF16)
        return run

    for u in range(o_ref.shape[0]):
        tasks.append(cast_task(u))
        for g in range(1, len(c_refs)):
            tasks += [chunk_task(u, g, r0) for r0 in range(0, n_rows, COMPACT_ROWS)]
    return tasks


def _drain(tasks, n_slots, i):
    for task in tasks[len(tasks) * i // n_slots:len(tasks) * (i + 1) // n_slots]:
        task()


def _side_plan(caches):
    n = caches[0].shape[0]
    gw = HEADS * HEAD_DIM
    c_args, sels = [], []
    for c, (win, dil) in zip(caches, ATTN_GROUPS):
        n_buf, n_back = c.shape[1], win // dil
        assert n_buf == n_back * dil, "window buffer must hold exactly n_back dilated rows"
        c_args.append(jnp.transpose(c, (0, 2, 3, 4, 1)).reshape(n, 2, gw, n_buf))
        if dil > 1:
            t = lax.broadcasted_iota(jnp.int32, (n_buf, n_back), 0)
            m = lax.broadcasted_iota(jnp.int32, (n_buf, n_back), 1)
            sels.append((t == dil * m).astype(BF16))
    return c_args, sels


def _side_io(side, step_of):
    gw = HEADS * HEAD_DIM
    n = side.c_args[0].shape[0]
    n_keys = N_GROUPS * (ATTN_GROUPS[0][0] // ATTN_GROUPS[0][1])
    assert side.per_step in (1, 2) and side.base % side.per_step == 0
    unit = lambda *idx: side.base + side.per_step * step_of(*idx)
    blk_idx = lambda *idx: (unit(*idx) // 2, (unit(*idx) % 2) // side.per_step, 0, 0)
    c_specs = [pl.BlockSpec((None, side.per_step, gw, c.shape[3]), blk_idx) for c in side.c_args]
    sels = side.sels if side.use_mxu else []
    in_specs = c_specs + [_const_spec(s_.shape) for s_ in sels]
    args = list(side.c_args) + list(sels)
    if side.kc is not None:
        in_specs.append(pl.BlockSpec(memory_space=pl.ANY))
        args.append(side.kc)
    out_spec = pl.BlockSpec((None, side.per_step, gw, n_keys), blk_idx)
    out_shape = jax.ShapeDtypeStruct((n, 2, gw, n_keys), BF16)
    return in_specs, args, out_spec, out_shape


def _split_side(refs, n_in, n_out, side):
    ins, rest = refs[:n_in], refs[n_in:]
    side_in = ()
    if side is not None:
        k = N_GROUPS + (len(side.sels) if side.use_mxu else 0) + (side.kc is not None)
        side_in, rest = rest[:k], rest[k:]
    outs, rest = rest[:n_out], rest[n_out:]
    side_out = None
    if side is not None:
        side_out, rest = rest[0], rest[1:]
    return ins, side_in, outs, side_out, rest


def _side_tasks(side, side_in, side_out):
    if side is None:
        return []
    n_sel = len(side.sels) if side.use_mxu else 0
    return _compact_tasks(side_in[:N_GROUPS], side_in[N_GROUPS:N_GROUPS + n_sel], side_out, side.use_mxu)


def _host_call(kernel, side, step_of, n_in, n_out, *, grid, in_specs, out_specs, out_shape, args,
               scratch_shapes, name):
    aliases = {}
    if side is not None:
        s_in, s_args, s_out, s_shape = _side_io(side, step_of)
        if side.kc is not None:
            aliases = {len(in_specs) + len(s_in) - 1: len(out_specs)}
        in_specs, args = in_specs + s_in, args + s_args
        out_specs, out_shape = out_specs + [s_out], out_shape + [s_shape]
    assert len(in_specs) == n_in + (0 if side is None else len(s_in))
    res = pl.pallas_call(
        functools.partial(kernel, side), grid=grid, in_specs=in_specs, out_specs=out_specs,
        out_shape=out_shape, scratch_shapes=scratch_shapes, input_output_aliases=aliases,
        compiler_params=_cparams(len(grid)), name=name,
    )(*args)
    return (res[:n_out], res[n_out]) if side is not None else (res, None)


def _ffn_core(x1, gf_ref, wup_ref, cw_ref, cb_ref, wdown_ref, hbuf, act, fc, rows, side_tasks=()):
    n_blocks, to_rows, prev_rows, from_rows = rows
    d_ff = wdown_ref.shape[0]
    q = hbuf.shape[0] // n_blocks
    blocks = [slice(j * q, (j + 1) * q) for j in range(n_blocks)]
    h = _rms(x1, gf_ref[...])
    for j, rb in enumerate(blocks):
        hbuf[rb, :] = to_rows(h, j).astype(BF16)
    for c in range(d_ff // fc):
        halves = []
        for base in (0, d_ff):
            cols = slice(base + c * fc, base + (c + 1) * fc)
            u = jnp.dot(hbuf[...], wup_ref[:, cols], preferred_element_type=F32)
            ub = [u[rb] for rb in blocks]
            prev = prev_rows(cols, ub)
            halves.append([cb_ref[:, cols] + ((cw_ref[0:1, cols] * u2 + cw_ref[1:2, cols] * u1)
                                              + cw_ref[2:3, cols] * u0)
                           for u0, (u1, u2) in zip(ub, prev)])
        for rb, a, b in zip(blocks, *halves):
            act[rb, c * fc:(c + 1) * fc] = (_silu(a) * b).astype(BF16)
        _drain(side_tasks, d_ff // fc, c)
    return x1 + from_rows(jnp.dot(act[...], wdown_ref[...], preferred_element_type=F32))


PROMPT_ROW_BLOCKS = 4


def _prompt_rows(slab, sbuf, ucarry, tm):
    nb = PROMPT_ROW_BLOCKS
    q = tm // nb
    n_slab = slab.shape[0]

    def to_rows(h, j):
        if j == 0:
            for k in range(n_slab):
                slab[k] = h[:, k * LANES:(k + 1) * LANES]
        return jnp.concatenate([slab[k, pl.ds(j, q, stride=nb), :] for k in range(n_slab)], axis=1)

    def shift_down(slot, blk, carry_row):
        sbuf[slot, SUBLANES - 1:SUBLANES, :] = carry_row
        sbuf[slot, SUBLANES:SUBLANES + q, :] = blk
        return sbuf[slot, SUBLANES - 1:SUBLANES - 1 + q, :]

    def prev_rows(cols, ub):
        s3 = shift_down(0, ub[3], ucarry[SUBLANES - 1:SUBLANES, cols])
        s2 = shift_down(1, ub[2], ucarry[SUBLANES - 2:SUBLANES - 1, cols])
        ucarry[SUBLANES - 1:SUBLANES, cols] = ub[3][q - 1:q]
        ucarry[SUBLANES - 2:SUBLANES - 1, cols] = ub[2][q - 1:q]
        return [(s3, s2), (ub[0], s3), (ub[1], ub[0]), (ub[2], ub[1])]

    def from_rows(y):
        for j in range(nb):
            for k in range(n_slab):
                slab[k, pl.ds(j, q, stride=nb), :] = y[j * q:(j + 1) * q, k * LANES:(k + 1) * LANES]
        return jnp.concatenate([slab[k] for k in range(n_slab)], axis=1)

    return nb, to_rows, prev_rows, from_rows


def _sample_rows(hist_ref, newhist_ref):
    def prev_rows(cols, ub):
        newhist_ref[0, :, cols] = hist_ref[1, :, cols]
        newhist_ref[1, :, cols] = ub[0]
        return [(hist_ref[1, :, cols], hist_ref[0, :, cols])]
    return 1, (lambda h, j: h), prev_rows, (lambda y: y)


def _pool_mix(xn, window_sum, cnt_of, wp_ref, ps_ref):
    gw = xn.shape[1] // len(POOL_WINDOWS)
    parts = []
    for g, w in enumerate(POOL_WINDOWS):
        sl = slice(g * gw, (g + 1) * gw)
        cur = xn[:, sl]
        d = window_sum(w, sl, cur) / cnt_of(w, cur.shape) - cur
        parts.append(jnp.dot(d.astype(BF16), wp_ref[g], preferred_element_type=F32))
    return jnp.concatenate(parts, axis=-1) * ps_ref[...]


POOL_PAD = 2 * SUBLANES


def _prompt_window_sum(xnbuf, lv, tm):
    hist = 2 * SUBLANES
    span = hist + tm

    def window_sum(w, sl, cur):
        s = xnbuf[POOL_PAD:POOL_PAD + span, sl] + xnbuf[POOL_PAD - 1:POOL_PAD - 1 + span, sl]
        shift, slot = 2, 0
        while shift < w:
            lv[slot, POOL_PAD:POOL_PAD + span, :] = s
            s = lv[slot, POOL_PAD:POOL_PAD + span, :] + lv[slot, POOL_PAD - shift:POOL_PAD - shift + span, :]
            shift, slot = 2 * shift, 1 - slot
        return s[hist:]
    return window_sum


def _prompt_layer_kernel(mode, tm, fc, n_cast, side, *refs):
    ins, side_in, outs, side_out, scratch = _split_side(refs, (9 if mode == "pool" else 8) + n_cast,
                                                        (3 if mode == "pool" else 2) + n_cast, side)
    tasks = _side_tasks(side, side_in, side_out)
    if n_cast:
        (ins, cast_in), (outs, cast_out) = (ins[:-n_cast], ins[-n_cast:]), (outs[:-n_cast], outs[-n_cast:])
        tasks = tasks + _cast_tasks(cast_in, cast_out)
    if mode == "pool":
        x_ref, gp_ref, wp_ref, ps_ref, gf_ref, wup_ref, cw_ref, cb_ref, wdown_ref = ins
        xo_ref, ptail_ref, ctail_ref = outs
        xnbuf, lv, hbuf, act, slab, sbuf, ucarry = scratch
    else:
        x_ref, o_ref, wo_ref, gf_ref, wup_ref, cw_ref, cb_ref, wdown_ref = ins
        xo_ref, ctail_ref = outs
        hbuf, act, slab, sbuf, ucarry = scratch
    t = pl.program_id(1)
    hist = 2 * SUBLANES

    @pl.when(t == 0)
    def _():
        ucarry[...] = jnp.zeros_like(ucarry)
        if mode == "pool":
            xnbuf[0:POOL_PAD + hist, :] = jnp.zeros((POOL_PAD + hist, xnbuf.shape[1]), F32)
            lv[:, 0:POOL_PAD, :] = jnp.zeros((lv.shape[0], POOL_PAD, lv.shape[2]), F32)

    x = x_ref[...]
    if mode == "pool":
        xn = _rms(x, gp_ref[...])
        xnbuf[POOL_PAD + hist:POOL_PAD + hist + tm, :] = xn

        def cnt_of(w, shape):
            pos = t * tm + lax.broadcasted_iota(jnp.int32, shape, 0)
            return jnp.minimum(pos + 1, w).astype(F32)

        x1 = x + _pool_mix(xn, _prompt_window_sum(xnbuf, lv, tm), cnt_of, wp_ref, ps_ref)
        tail = xnbuf[POOL_PAD + tm:POOL_PAD + tm + hist, :]
        ptail_ref[...] = tail
        xnbuf[POOL_PAD:POOL_PAD + hist, :] = tail
    else:
        x1 = x + jnp.dot(o_ref[...].astype(BF16), wo_ref[...], preferred_element_type=F32)

    xo_ref[...] = _ffn_core(x1, gf_ref, wup_ref, cw_ref, cb_ref, wdown_ref, hbuf, act, fc,
                            _prompt_rows(slab, sbuf, ucarry, tm), tasks)
    ctail_ref[...] = ucarry[...]


def _sample_layer_kernel(mode, fc, n_chained, *refs):
    n_in = 11 if mode == "pool" else 9
    refs = refs[:n_in] + refs[n_in + n_chained:]
    if mode == "pool":
        (x_ref, st_ref, gp_ref, wp_ref, ps_ref, gf_ref, wup_ref, cw_ref, cb_ref, wdown_ref,
         hist_ref, xo_ref, nst_ref, nhist_ref, hbuf, act) = refs
    else:
        (x_ref, o_ref, wo_ref, gf_ref, wup_ref, cw_ref, cb_ref, wdown_ref,
         hist_ref, xo_ref, nhist_ref, hbuf, act) = refs
    x = x_ref[...]
    if mode == "pool":
        xn = _rms(x, gp_ref[...])
        n_hist = st_ref.shape[0]

        def window_sum(w, sl, cur):
            s = cur
            for k in range(1, w):
                s = s + st_ref[n_hist - k, :, sl]
            return s

        def cnt_of(w, shape):
            return float(min(PAST_LEN + 1, w))

        x1 = x + _pool_mix(xn, window_sum, cnt_of, wp_ref, ps_ref)
        for i in range(n_hist - 1):
            nst_ref[i] = st_ref[i + 1]
        nst_ref[n_hist - 1] = xn
    else:
        x1 = x + jnp.dot(o_ref[...].astype(BF16), wo_ref[...], preferred_element_type=F32)
    xo_ref[...] = _ffn_core(x1, gf_ref, wup_ref, cw_ref, cb_ref, wdown_ref, hbuf, act, fc,
                            _sample_rows(hist_ref, nhist_ref))


MXU_TILE = 256


def _pick_fc(f):
    for fc in (MXU_TILE, MXU_TILE // 2):
        if f % fc == 0:
            return fc
    return f


def _cast_rows(d, f, steps):
    if steps % 2 or d % steps or f % (steps // 2):
        return None
    ru, rd = d // steps, f // (steps // 2)
    return (ru, rd) if ru % (2 * SUBLANES) == 0 and rd % (2 * SUBLANES) == 0 else None


def _cast_io(cast, d, f, steps, step_of):
    weights, nl = cast
    in_specs, args, out_specs, out_shape = [], [], [], []
    for w_f, (rows, span) in zip(weights, zip(_cast_rows(d, f, steps), (1, 2))):
        blk = lambda *idx, span=span: step_of(*idx) // span
        in_specs.append(pl.BlockSpec((None, rows, w_f.shape[2]), lambda *idx, blk=blk: (nl, blk(*idx), 0)))
        args.append(w_f)
        out_specs.append(pl.BlockSpec((rows, w_f.shape[2]), lambda *idx, blk=blk: (blk(*idx), 0)))
        out_shape.append(jax.ShapeDtypeStruct(w_f.shape[1:], BF16))
    return in_specs, args, out_specs, out_shape


def _cast_tasks(cast_in, cast_out):
    def task(src, dst):
        def run():
            dst[...] = src[...].astype(BF16)
        return run
    return [task(s_, d_) for s_, d_ in zip(cast_in, cast_out)]


def _prompt_layer(mode, x, pre, pre_l, ffn_w, tm, side, cast=None):
    b, s, d = x.shape
    f2, f = ffn_w[1][0].shape[-1], ffn_w[4][0].shape[-2]
    fc = _pick_fc(f)
    row = pl.BlockSpec((None, tm, d), lambda i, j: (i, j, 0))
    hist = 2 * SUBLANES
    ctail_spec = pl.BlockSpec((None, SUBLANES, f2), lambda i, j: (i, 0, 0))
    ctail_shape = jax.ShapeDtypeStruct((b, SUBLANES, f2), F32)
    common_scratch = [pltpu.VMEM((tm, d), BF16), pltpu.VMEM((tm, f), BF16),
                      pltpu.VMEM((d // LANES, tm, LANES), F32),
                      pltpu.VMEM((2, tm // PROMPT_ROW_BLOCKS + SUBLANES, fc), F32),
                      pltpu.VMEM((SUBLANES, f2), F32)]
    if mode == "pool":
        in_specs = [row] + [_layer_spec(a, pre_l) for a in pre]
        args = [x, *pre]
        out_specs = [row, pl.BlockSpec((None, hist, d), lambda i, j: (i, 0, 0)), ctail_spec]
        out_shape = [jax.ShapeDtypeStruct(x.shape, F32), jax.ShapeDtypeStruct((b, hist, d), F32),
                     ctail_shape]
        ext = POOL_PAD + hist + tm
        scratch = [pltpu.VMEM((ext, d), F32),
                   pltpu.VMEM((2, ext, d // len(POOL_WINDOWS)), F32)] + common_scratch
    else:
        o, wo = pre
        in_specs = [row, pl.BlockSpec((None, tm, o.shape[2]), lambda i, j: (i, j, 0)),
                    _layer_spec(wo, pre_l)]
        args = [x, o, wo]
        out_specs = [row, ctail_spec]
        out_shape = [jax.ShapeDtypeStruct(x.shape, F32), ctail_shape]
        scratch = common_scratch
    nj = s // tm
    step_of = lambda i, j: i * nj + j
    in_specs = in_specs + [_layer_spec(a, l) for a, l in ffn_w]
    args = args + [a for a, _ in ffn_w]
    n_cast = 0
    if cast is not None:
        c_in, c_args, c_out, c_shape = _cast_io(cast, d, f, b * nj, step_of)
        in_specs, args = in_specs + c_in, args + c_args
        out_specs, out_shape = out_specs + c_out, out_shape + c_shape
        n_cast = len(c_in)
    return _host_call(
        functools.partial(_prompt_layer_kernel, mode, tm, fc, n_cast), side, step_of,
        len(in_specs), len(out_specs),
        grid=(b, nj), in_specs=in_specs, out_specs=out_specs, out_shape=out_shape, args=args,
        scratch_shapes=scratch, name=f"prompt_{mode}_layer")


def _sample_layer(mode, x, pre, pre_l, ffn_w, l, hist_t, bn, chained):
    n, d = x.shape
    f2, f = ffn_w[1][0].shape[-1], ffn_w[4][0].shape[-2]
    fc = _pick_fc(f)
    row = lambda width: pl.BlockSpec((bn, width), lambda i: (i, 0))
    tmaj = lambda a, k: pl.BlockSpec((None, a.shape[1], bn, a.shape[3]), lambda i: (k, 0, i, 0))
    if mode == "pool":
        st_t = pre[0]
        in_specs = [row(d), tmaj(st_t, pre_l)] + [_layer_spec(a, pre_l) for a in pre[1:]]
        args = [x, *pre]
        states = [(st_t, pre_l), (hist_t, l)]
    else:
        o, wo = pre
        in_specs = [row(d), row(o.shape[1]), _layer_spec(wo, pre_l)]
        args = [x, o, wo]
        states = [(hist_t, l)]
    in_specs = in_specs + [_layer_spec(a, k) for a, k in ffn_w] + [tmaj(hist_t, l)]
    args = args + [a for a, _ in ffn_w] + [hist_t]
    out_specs = [row(d)] + [tmaj(a, k) for a, k in states]
    out_shape = [jax.ShapeDtypeStruct(x.shape, F32)] + [jax.ShapeDtypeStruct(a.shape, F32) for a, _ in states]
    aliases = {}
    for k, prev in enumerate(chained):
        if prev is not None:
            aliases[len(in_specs)] = 1 + k
            in_specs.append(pl.BlockSpec(memory_space=pl.ANY))
            args.append(prev)
    return pl.pallas_call(
        functools.partial(_sample_layer_kernel, mode, fc, len(aliases)),
        grid=(n // bn,), in_specs=in_specs, out_specs=out_specs, out_shape=out_shape,
        scratch_shapes=[pltpu.VMEM((bn, d), BF16), pltpu.VMEM((bn, f), BF16)],
        input_output_aliases=aliases,
        compiler_params=_cparams(1), name=f"sample_{mode}_layer",
    )(*args)


HEAD_TILE = 2 * LANES


def _head_norm_rot(x, bd_ref, gain, tabs):
    gcos, sina, sinb = tabs
    ms = jnp.dot((x * x).astype(BF16), bd_ref[...], preferred_element_type=F32)
    outs = []
    for j in range(x.shape[1] // LANES):
        xs = x[:, j * LANES:(j + 1) * LANES]
        xg = xs * gain
        outs.append(xs * gcos + pltpu.roll(xg, LANES - ROT_DIM // 2, 1) * sina
                    + pltpu.roll(xg, ROT_DIM // 2, 1) * sinb)
    return jnp.concatenate(outs, axis=1) * lax.rsqrt(ms + NORM_EPS)


def _proj_kernel(do_kv, kvt_groups, n_tab_blocks, n_cast, side, *refs):
    n_out = ((N_GROUPS + 1 + len(kvt_groups)) if do_kv else 1) + n_cast
    ins, side_in, outs, side_out, _ = _split_side(refs, (14 if do_kv else 8) + n_cast, n_out, side)
    side_tasks = _side_tasks(side, side_in, side_out)
    if n_cast:
        side_tasks = side_tasks + _cast_tasks(ins[-n_cast:], outs[-n_cast:])
        ins, outs = ins[:-n_cast], outs[:-n_cast]
    x_ref, bd_ref = ins[0:2]
    if do_kv:
        ktabs, (gkv_ref, wkv_ref, kg_ref) = ins[2:5], ins[5:8]
        qtabs, (gq_ref, wq_ref, qg_ref) = ins[8:11], ins[11:14]
        kv_refs, q_ref, kvt_refs = outs[0:N_GROUPS], outs[N_GROUPS], outs[N_GROUPS + 1:]
    else:
        qtabs, (gq_ref, wq_ref, qg_ref) = ins[2:5], ins[5:8]
        q_ref = outs[0]
    x = x_ref[...]
    gw = HEADS * HEAD_DIM
    if do_kv:
        h = _rms(x, gkv_ref[...]).astype(BF16)
        kv = jnp.dot(h, wkv_ref[...], preferred_element_type=F32)
        tabs = [t[...] for t in ktabs]
        for g, out_ref in enumerate(kv_refs):
            for j in range(gw // HEAD_TILE):
                c0 = g * gw + j * HEAD_TILE
                out_ref[:, j * HEAD_TILE:(j + 1) * HEAD_TILE] = _head_norm_rot(
                    kv[:, c0:c0 + HEAD_TILE], bd_ref, kg_ref[...], tabs)
            v0 = N_GROUPS * gw + g * gw
            out_ref[:, gw:2 * gw] = kv[:, v0:v0 + gw]
        tm = x.shape[0]
        for (g, win), kvt_ref in zip(kvt_groups, kvt_refs):
            if win == n_tab_blocks * tm:
                kvt_ref[...] = kv_refs[g][...].T
            else:
                @pl.when(pl.program_id(0) % n_tab_blocks == n_tab_blocks - 1)
                def _(g=g, win=win, kvt_ref=kvt_ref):
                    kvt_ref[...] = kv_refs[g][tm - win:tm, :].T
    h = _rms(x, gq_ref[...]).astype(BF16)
    q = jnp.dot(h, wq_ref[...], preferred_element_type=F32)
    tabs = [t[...] for t in qtabs]
    n_tiles = q.shape[1] // HEAD_TILE
    for j in range(n_tiles):
        cols = slice(j * HEAD_TILE, (j + 1) * HEAD_TILE)
        q_ref[:, cols] = _head_norm_rot(q[:, cols], bd_ref, qg_ref[...], tabs)
        _drain(side_tasks, n_tiles, j)


def _proj(x2d, n_tab_blocks, kv_w, q_w, tm, kvt_groups=(), side=None, cast=None):
    r, d = x2d.shape
    row = pl.BlockSpec((tm, d), lambda i: (i, 0))
    tab = pl.BlockSpec((tm, LANES), lambda i: (i % n_tab_blocks, 0))
    gw = HEADS * HEAD_DIM
    head = lax.broadcasted_iota(jnp.int32, (HEAD_TILE, HEAD_TILE), 0) // HEAD_DIM
    bd = jnp.where(head == head.T, 1.0 / HEAD_DIM, 0.0).astype(BF16)
    in_specs, args = [row, _const_spec(bd.shape)], [x2d, bd]
    out_specs, out_shape = [], []
    for w in (kv_w, q_w):
        if w is None:
            continue
        tabs, g, wmat, hg, layer = w
        wspec = lambda a, layer=layer: _const_spec(a.shape) if layer is None else _layer_spec(a, layer)
        in_specs += [tab, tab, tab, wspec(g), wspec(wmat), _const_spec((1, LANES))]
        args += [*tabs, g, wmat, hg]
    if kv_w is not None:
        out_specs += [pl.BlockSpec((tm, 2 * gw), lambda i: (i, 0))] * N_GROUPS
        out_shape += [jax.ShapeDtypeStruct((r, 2 * gw), F32)] * N_GROUPS
    n_q = q_w[2].shape[-1]
    out_specs.append(pl.BlockSpec((tm, n_q), lambda i: (i, 0)))
    out_shape.append(jax.ShapeDtypeStruct((r, n_q), F32))
    seq = n_tab_blocks * tm
    if kv_w is not None:
        for _, win in kvt_groups:
            assert win == seq or win <= tm
            if win == seq:
                out_specs.append(pl.BlockSpec((None, 2 * gw, tm), lambda i: (i // n_tab_blocks, 0, i % n_tab_blocks)))
            else:
                out_specs.append(pl.BlockSpec((None, 2 * gw, win), lambda i: (i // n_tab_blocks, 0, 0)))
            out_shape.append(jax.ShapeDtypeStruct((r // seq, 2 * gw, win), F32))
    n_cast = 0
    if cast is not None:
        c_in, c_args, c_out, c_shape = _cast_io(cast, d, cast[0][1].shape[1], r // tm, lambda i: i)
        in_specs, args = in_specs + c_in, args + c_args
        out_specs, out_shape = out_specs + c_out, out_shape + c_shape
        n_cast = len(c_in)
    return _host_call(
        functools.partial(_proj_kernel, kv_w is not None, tuple(kvt_groups), n_tab_blocks, n_cast), side,
        lambda i: i, len(in_specs), len(out_specs),
        grid=(r // tm,), in_specs=in_specs, out_specs=out_specs, out_shape=out_shape, args=args,
        scratch_shapes=[], name="proj_kvq" if kv_w is not None else "proj_q")


def _prompt_attn_kernel(seq, side, *refs):
    ins, side_in, outs, side_out, scratch = _split_side(refs, 3 * N_GROUPS, 1, side)
    q_refs, k_refs, v_refs = ins[0:3], ins[3:6], ins[6:9]
    o_ref = outs[0]
    qb, kb, vb, tmpf, accs, ms, ls = scratch
    blk = ATTN_BLOCK
    lo = lax.broadcasted_iota(jnp.int32, (blk, LANES), 1) < HEAD_DIM
    quarter = seq // 4
    side_tasks = _side_tasks(side, side_in, side_out)
    n_blocks_total = sum(seq // blk for _ in ATTN_GROUPS)
    done_blocks = 0

    def valid_mask(nk):
        qi = lax.broadcasted_iota(jnp.int32, (2 * blk, nk), 0) & (blk - 1)
        kj = lax.broadcasted_iota(jnp.int32, (2 * blk, nk), 1)
        if nk == blk:
            return kj <= qi
        return lax.bitcast_convert_type(kj - qi, jnp.uint32) <= jnp.uint32(blk)

    for g, (win, dil) in enumerate(ATTN_GROUPS):
        assert win // dil == blk and dil in (1, 4, 16) and seq % (blk * dil) == 0
        sub = seq // dil
        nb = sub // blk

        def stage(src, dil=dil):
            if dil == 16:
                for r1 in range(4):
                    tmpf[r1 * quarter:(r1 + 1) * quarter, :] = src[pl.ds(r1, quarter, stride=4), :]

        def rows_of(src, r, n0, cnt, dil=dil):
            if dil == 1:
                return src[pl.ds(n0, cnt), :]
            if dil == 4:
                return src[pl.ds(r + 4 * n0, cnt, stride=4), :]
            return tmpf[pl.ds((r % 4) * quarter + r // 4 + 4 * n0, cnt, stride=4), :]

        stage(q_refs[g])
        for r in range(dil):
            for n in range(nb):
                qf = rows_of(q_refs[g], r, n * blk, blk)
                base = (r * nb + n) * 2 * blk
                qb[base:base + blk, :] = jnp.where(lo, qf, 0.0).astype(BF16)
                qb[base + blk:base + 2 * blk, :] = jnp.where(lo, 0.0, qf).astype(BF16)
        for src, dst in ((k_refs[g], kb), (v_refs[g], vb)):
            stage(src)
            for r in range(dil):
                dst[r * sub:(r + 1) * sub, :] = rows_of(src, r, 0, sub).astype(BF16)

        def block(r, n, first, g=g, dil=dil, sub=sub, nb=nb):
            q2 = qb[pl.ds((r * nb + n) * 2 * blk, 2 * blk), :]
            nk = blk if first else 2 * blk
            k0 = r * sub + (n if first else n - 1) * blk
            kt = kb[pl.ds(k0, nk), :]
            vt = vb[pl.ds(k0, nk), :]
            s = lax.dot_general(q2, kt, (((1,), (1,)), ((), ())), preferred_element_type=F32)
            s = jnp.where(valid_mask(nk), s, NEG_INF)
            m = jnp.max(s, axis=-1, keepdims=True)
            p = jnp.exp(s - m)
            l = jnp.sum(p, axis=-1, keepdims=True)
            pv = jnp.dot(p.astype(BF16), vt, preferred_element_type=F32)
            if dil > 1:
                out_rows = pl.ds(r + n * blk * dil, blk, stride=dil)
            else:
                out_rows = pl.ds(n * blk, blk)
            accs[g, out_rows, :] = jnp.where(lo, pv[:blk], pv[blk:])
            ms[g, out_rows, :] = jnp.where(lo, m[:blk], m[blk:])
            ls[g, out_rows, :] = jnp.where(lo, l[:blk], l[blk:])

        for r in range(dil):
            for n in range(nb):
                block(r, n, n == 0)
                _drain(side_tasks, n_blocks_total, done_blocks)
                done_blocks += 1

    def merge(i, carry):
        rows = pl.ds(pl.multiple_of(i * blk, blk), blk)
        m = jnp.maximum(jnp.maximum(ms[0, rows, :], ms[1, rows, :]), ms[2, rows, :])
        num = jnp.zeros((blk, LANES), F32)
        den = jnp.zeros((blk, LANES), F32)
        for g in range(N_GROUPS):
            w = jnp.exp(ms[g, rows, :] - m)
            num = num + w * accs[g, rows, :]
            den = den + w * ls[g, rows, :]
        o_ref[rows, :] = (num / den).astype(o_ref.dtype)
        return carry

    lax.fori_loop(0, seq // blk, merge, 0)


def _prompt_attn(q, kvs, side):
    b, s, _ = q.shape
    gw = HEADS * HEAD_DIM
    n_hp = gw // LANES
    blk_spec = lambda col: pl.BlockSpec((None, s, LANES), lambda i, j, col=col: (i, 0, col + j))
    in_specs = ([blk_spec(g * n_hp) for g in range(N_GROUPS)]
                + [blk_spec(0)] * N_GROUPS
                + [blk_spec(n_hp)] * N_GROUPS)
    args = [q] * N_GROUPS + list(kvs) + list(kvs)
    (o,), kc = _host_call(
        functools.partial(_prompt_attn_kernel, s), side, lambda i, j: i * n_hp + j, len(in_specs), 1,
        grid=(b, n_hp), in_specs=in_specs,
        out_specs=[pl.BlockSpec((None, s, LANES), lambda i, j: (i, 0, j))],
        out_shape=[jax.ShapeDtypeStruct((b, s, gw), BF16)], args=args,
        scratch_shapes=[pltpu.VMEM((2 * s, LANES), BF16),
                        pltpu.VMEM((s, LANES), BF16),
                        pltpu.VMEM((s, LANES), BF16),
                        pltpu.VMEM((s, LANES), F32)]
                       + [pltpu.VMEM((N_GROUPS, s, LANES), F32)] * 3,
        name="prompt_attn")
    return o, kc


def _compact_kernel(side, *refs):
    _, side_in, _, side_out, _ = _split_side(refs, 0, 0, side)
    for task in _side_tasks(side, side_in, side_out):
        task()


def _compact_rest(side, n_units):
    _, kc = _host_call(_compact_kernel, side, lambda i: i, 0, 0, grid=(n_units,), in_specs=[],
                       out_specs=[], out_shape=[], args=[], scratch_shapes=[], name="compact_caches")
    return kc


def _sample_attn_kernel(bb, q_ref, kv0_ref, kv1_ref, kv2_ref, kc_ref, o_ref):
    gw = HEADS * HEAD_DIM
    head_of_row = lax.broadcasted_iota(jnp.int32, (HEADS, gw), 0)
    head_of_col = lax.broadcasted_iota(jnp.int32, (HEADS, gw), 1) // HEAD_DIM
    diag = head_of_row == head_of_col
    nk = kc_ref.shape[-1] // N_GROUPS
    kvn_refs = (kv0_ref, kv1_ref, kv2_ref)

    def body(b, carry):
        row = pl.ds(b, 1)
        q_bd = [jnp.where(diag, q_ref[row, g * gw:(g + 1) * gw], 0.0) for g in range(N_GROUPS)]
        s_all = jnp.dot(jnp.concatenate(q_bd, axis=0).astype(BF16), kc_ref[b, 0],
                        preferred_element_type=F32)
        s = [s_all[g * HEADS:(g + 1) * HEADS, g * nk:(g + 1) * nk] for g in range(N_GROUPS)]
        s_new = [jnp.sum(q_bd[g] * kvn_refs[g][row, 0:gw], axis=-1, keepdims=True)
                 for g in range(N_GROUPS)]
        m = functools.reduce(jnp.maximum, [jnp.max(a, axis=-1, keepdims=True) for a in s] + s_new)
        p = [jnp.exp(a - m) for a in s]
        p_new = [jnp.exp(a - m) for a in s_new]
        l = sum(jnp.sum(a, axis=-1, keepdims=True) for a in p) + sum(p_new)
        acc = lax.dot_general(jnp.concatenate(p, axis=1).astype(BF16), kc_ref[b, 1],
                              (((1,), (1,)), ((), ())), preferred_element_type=F32)
        for g in range(N_GROUPS):
            acc = acc + p_new[g] * kvn_refs[g][row, gw:2 * gw]
        o_ref[row, :] = jnp.sum(jnp.where(diag, acc / l, 0.0), axis=0, keepdims=True)
        return carry

    lax.fori_loop(0, bb, body, 0, unroll=True)


def _sample_attn(q, kvn, kc, bb):
    n = q.shape[0]
    gw = HEADS * HEAD_DIM
    rows = lambda width: pl.BlockSpec((bb, width), lambda i: (i, 0))
    return pl.pallas_call(
        functools.partial(_sample_attn_kernel, bb),
        grid=(n // bb,),
        in_specs=[rows(q.shape[1])] + [rows(2 * gw)] * N_GROUPS
                 + [pl.BlockSpec((bb,) + kc.shape[1:], lambda i: (i, 0, 0, 0))],
        out_specs=rows(gw),
        out_shape=jax.ShapeDtypeStruct((n, gw), F32),
        compiler_params=_cparams(1), name="sample_attn",
    )(q, *kvn, kc)


def _rot_tables(pos):
    half = ROT_DIM // 2
    lane = lax.broadcasted_iota(jnp.int32, (pos.shape[0], LANES), 1) % HEAD_DIM
    inv_freq = ROPE_THETA ** (-(2 * (lane % half)).astype(F32) / ROT_DIM)
    ang = pos.astype(F32)[:, None] * inv_freq
    cos, sin = jnp.cos(ang), jnp.sin(ang)
    cosb = jnp.where(lane < ROT_DIM, cos, 1.0)
    sina = jnp.where(lane < half, -sin, 0.0)
    sinb = jnp.where((lane >= half) & (lane < ROT_DIM), sin, 0.0)
    return cosb, sina, sinb


def kernel(x_prompt, x_sample, state_pool, state_conv, cache_kv_w128, cache_kv_w512, cache_kv_w2048,
           g_pool, w_pool, pool_scale, g_ffn, w_up, conv_w, conv_b, w_down, g_kv, w_kv, k_gain,
           g_attn, w_q, q_gain, w_o):
    b, s, d = x_prompt.shape
    n = x_sample.shape[0]
    depth = w_up.shape[0]
    n_pool = g_pool.shape[0]
    gw = HEADS * HEAD_DIM
    rows3 = lambda a: a.reshape(a.shape[0], 1, a.shape[1])
    lane2 = lambda a: jnp.tile(a, LANES // HEAD_DIM).reshape(1, LANES)
    tm = min(512, s)
    bn = min(64, n)
    bb = min(16, n)

    gf3, cb3 = rows3(g_ffn), rows3(conv_b)
    ffn_mats = {}

    def ffn_w(l):
        if l not in ffn_mats:
            ffn_mats[l] = (w_up[l].astype(BF16), w_down[l].astype(BF16))
        wup_l, wdown_l = ffn_mats[l]
        return [(gf3, l), (wup_l, None), (conv_w, l), (cb3, l), (wdown_l, None)]
    pool_w = (rows3(g_pool), w_pool.astype(BF16), rows3(pool_scale))
    wkv_b, wq_b, wo_b = w_kv.astype(BF16), w_q.astype(BF16), w_o.astype(BF16)
    q_scale = HEAD_DIM ** -0.5

    def proj_w(pos):
        cosb, sina, sinb = _rot_tables(pos)
        kg = lane2(k_gain)
        kv_w = ((cosb * kg, sina, sinb), g_kv.reshape(1, d), wkv_b, kg, None)
        q_w = []
        for j in range(depth - n_pool):
            qg = lane2(q_gain[j])
            q_w.append(((cosb * (qg * q_scale), sina * q_scale, sinb * q_scale), rows3(g_attn), wq_b, qg, j))
        return kv_w, q_w

    kv_w_p, q_w_p = proj_w(jnp.arange(s, dtype=jnp.int32))
    kv_w_s, q_w_s = proj_w(jnp.full((n,), PAST_LEN, jnp.int32))

    c_args, sels = _side_plan((cache_kv_w128, cache_kv_w512, cache_kv_w2048))
    units = {"done": 0, "kc": None}

    def side_for(steps, per_step=1, use_mxu=False):
        if units["done"] + steps * per_step > 2 * n:
            return None
        side = _Side(c_args, sels, units["done"], units["kc"], per_step, use_mxu)
        units["done"] += steps * per_step
        return side

    def took(kc):
        if kc is not None:
            units["kc"] = kc

    xp = x_prompt
    pool_p, conv_p = [], []
    n_steps = b * (s // tm)
    can_cast = _cast_rows(d, w_down.shape[1], n_steps) is not None
    kvt_groups = [(g, min(win, s)) for g, (win, _) in enumerate(ATTN_GROUPS) if win >= s or win <= tm]
    kvs_p = kvts_p = None
    for l in range(depth):
        cast = ((w_up, w_down), l) if can_cast and l >= n_pool else None
        if l < n_pool:
            (xp, ptail, ctail), kc = _prompt_layer("pool", xp, pool_w, l, ffn_w(l), tm, side_for(n_steps, 2))
            took(kc)
            pool_p.append(ptail[:, 2 * SUBLANES - POOL_HIST:])
        else:
            j = l - n_pool
            x2d = xp.reshape(b * s, d)
            if j == 0:
                res, kc = _proj(x2d, s // tm, kv_w_p, q_w_p[j], tm, kvt_groups, None, cast)
                kvs_p, q_p = [a.reshape(b, s, 2 * gw) for a in res[:N_GROUPS]], res[N_GROUPS]
                kvts_p = dict(zip((g for g, _ in kvt_groups), res[N_GROUPS + 1:]))
                nxt = res[N_GROUPS + 1 + len(kvt_groups):]
            else:
                (q_p, *nxt), kc = _proj(x2d, s // tm, None, q_w_p[j], tm, (), None, cast)
            if cast:
                ffn_mats[l] = tuple(nxt)
            o_p, kc = _prompt_attn(q_p.reshape(b, s, N_GROUPS * gw), kvs_p, None)
            took(kc)
            (xp, ctail), kc = _prompt_layer("attn", xp, (o_p, wo_b), j, ffn_w(l), tm, side_for(n_steps, 2))
            took(kc)
        conv_p.append(ctail[:, SUBLANES - (CONV_W - 1):])
    if units["done"] < 2 * n:
        took(_compact_rest(_Side(c_args, sels, units["done"], units["kc"], 1, True), 2 * n - units["done"]))
    kc = units["kc"]

    xs = x_sample.reshape(n, d)
    hist_t = jnp.swapaxes(state_conv, 1, 2)
    st_t = jnp.swapaxes(state_pool, 1, 2)
    kvs_s = nst = nhist = None
    for l in range(depth):
        if l < n_pool:
            xs, nst, nhist = _sample_layer("pool", xs, (st_t,) + pool_w, l, ffn_w(l), l, hist_t, bn,
                                           (nst, nhist))
        else:
            j = l - n_pool
            if j == 0:
                (*kvs_s, q_s), _ = _proj(xs, 1, kv_w_s, q_w_s[j], n)
            else:
                (q_s,), _ = _proj(xs, 1, None, q_w_s[j], n)
            o_s = _sample_attn(q_s, kvs_s, kc, bb)
            xs, nhist = _sample_layer("attn", xs, (o_s, wo_b), j, ffn_w(l), l, hist_t, min(2 * bn, n), (nhist,))

    outs = [xp, xs.reshape(n, 1, d), jnp.stack(pool_p), jnp.swapaxes(nst, 1, 2),
            jnp.stack(conv_p), jnp.swapaxes(nhist, 1, 2)]
    for g, (win, _) in enumerate(ATTN_GROUPS):
        if g in kvts_p:
            rows_p = jnp.transpose(kvts_p[g].reshape(b, 2, HEADS, HEAD_DIM, min(win, s)), (0, 4, 1, 2, 3))
        else:
            rows_p = kvs_p[g][:, s - min(win, s):].reshape(b, min(win, s), 2, HEADS, HEAD_DIM)
        outs += [rows_p, kvs_s[g].reshape(n, 1, 2, HEADS, HEAD_DIM)]
    return tuple(outs)
```

```python
import collections
import functools

import jax
import jax.numpy as jnp
from jax import lax
from jax.experimental import pallas as pl
from jax.experimental.pallas import tpu as pltpu

F32 = jnp.float32
BF16 = jnp.bfloat16

NORM_EPS = 1e-6
NEG_INF = -1e30
POOL_WINDOWS = (2, 4, 8, 16)
POOL_HIST = max(POOL_WINDOWS) - 1
CONV_W = 3
ATTN_GROUPS = ((128, 1), (512, 4), (2048, 16))
N_GROUPS = len(ATTN_GROUPS)
HEADS = 8
HEAD_DIM = 64
ROT_DIM = HEAD_DIM // 4
ROPE_THETA = 500000.0
ATTN_BLOCK = 128
PAST_LEN = 2048

LANES = 128
SUBLANES = 8
VMEM_LIMIT = 64 * 1024 * 1024 - 128 * 1024


def _cparams(n_grid):
    return pltpu.CompilerParams(dimension_semantics=("arbitrary",) * n_grid,
                                vmem_limit_bytes=VMEM_LIMIT)


def _const_spec(shape):
    nd = len(shape)
    return pl.BlockSpec(shape, lambda *_: (0,) * nd, pipeline_mode=pl.Buffered(1))


def _layer_spec(stacked, l):
    if l is None:
        return _const_spec(stacked.shape)
    nd = stacked.ndim - 1
    return pl.BlockSpec((None,) + stacked.shape[1:], lambda *_: (l,) + (0,) * nd,
                        pipeline_mode=pl.Buffered(1))


def _rms(x, g):
    ms = jnp.mean(x * x, axis=-1, keepdims=True)
    return x * lax.rsqrt(ms + NORM_EPS) * g


def _silu(a):
    return a / (1.0 + jnp.exp(-a))


_Side = collections.namedtuple("_Side", "c_args sels base kc per_step use_mxu")

COMPACT_ROWS = 64


def _compact_tasks(c_refs, sel_refs, o_ref, use_mxu):
    nk = c_refs[0].shape[-1]
    n_rows = c_refs[0].shape[1]
    assert nk == LANES and n_rows % COMPACT_ROWS == 0
    tasks = []

    def cast_task(u):
        def run():
            o_ref[u, :, 0:nk] = c_refs[0][u].astype(BF16)
        return run

    def chunk_task(u, g, r0):
        rows, out_cols = slice(r0, r0 + COMPACT_ROWS), slice(g * nk, (g + 1) * nk)
        dil = c_refs[g].shape[-1] // nk

        def run():
            if use_mxu:
                o_ref[u, rows, out_cols] = jnp.dot(c_refs[g][u, rows, :].astype(BF16), sel_refs[g - 1][...],
                                                   preferred_element_type=F32).astype(BF16)
                return
            keep = lax.broadcasted_iota(jnp.int32, (COMPACT_ROWS, LANES), 1) % dil == 0
            acc = jnp.where(keep, c_refs[g][u, rows, 0:LANES], 0.0)
            for j in range(1, dil):
                tile = jnp.where(keep, c_refs[g][u, rows, j * LANES:(j + 1) * LANES], 0.0)
                acc = acc + pltpu.roll(tile, j, 1)
            o_ref[u, rows, out_cols] = acc.astype(BF16)
        return run

    for u in range(o_ref.shape[0]):
        tasks.append(cast_task(u))
        for g in range(1, len(c_refs)):
            tasks += [chunk_task(u, g, r0) for r0 in range(0, n_rows, COMPACT_ROWS)]
    return tasks


def _drain(tasks, n_slots, i):
    for task in tasks[len(tasks) * i // n_slots:len(tasks) * (i + 1) // n_slots]:
        task()


def _side_plan(caches):
    n = caches[0].shape[0]
    gw = HEADS * HEAD_DIM
    c_args, sels = [], []
    for c, (win, dil) in zip(caches, ATTN_GROUPS):
        n_buf, n_back = c.shape[1], win // dil
        assert n_buf == n_back * dil, "window buffer must hold exactly n_back dilated rows"
        c_args.append(jnp.transpose(c, (0, 2, 3, 4, 1)).reshape(n, 2, gw, n_buf))
        if dil > 1:
            t = lax.broadcasted_iota(jnp.int32, (n_buf, n_back), 0)
            m = lax.broadcasted_iota(jnp.int32, (n_buf, n_back), 1)
            sels.append((t == dil * m).astype(BF16))
    return c_args, sels


def _side_io(side, step_of):
    gw = HEADS * HEAD_DIM
    n = side.c_args[0].shape[0]
    n_keys = N_GROUPS * (ATTN_GROUPS[0][0] // ATTN_GROUPS[0][1])
    assert side.per_step in (1, 2) and side.base % side.per_step == 0
    unit = lambda *idx: side.base + side.per_step * step_of(*idx)
    blk_idx = lambda *idx: (unit(*idx) // 2, (unit(*idx) % 2) // side.per_step, 0, 0)
    c_specs = [pl.BlockSpec((None, side.per_step, gw, c.shape[3]), blk_idx) for c in side.c_args]
    sels = side.sels if side.use_mxu else []
    in_specs = c_specs + [_const_spec(s_.shape) for s_ in sels]
    args = list(side.c_args) + list(sels)
    if side.kc is not None:
        in_specs.append(pl.BlockSpec(memory_space=pl.ANY))
        args.append(side.kc)
    out_spec = pl.BlockSpec((None, side.per_step, gw, n_keys), blk_idx)
    out_shape = jax.ShapeDtypeStruct((n, 2, gw, n_keys), BF16)
    return in_specs, args, out_spec, out_shape


def _split_side(refs, n_in, n_out, side):
    ins, rest = refs[:n_in], refs[n_in:]
    side_in = ()
    if side is not None:
        k = N_GROUPS + (len(side.sels) if side.use_mxu else 0) + (side.kc is not None)
        side_in, rest = rest[:k], rest[k:]
    outs, rest = rest[:n_out], rest[n_out:]
    side_out = None
    if side is not None:
        side_out, rest = rest[0], rest[1:]
    return ins, side_in, outs, side_out, rest


def _side_tasks(side, side_in, side_out):
    if side is None:
        return []
    n_sel = len(side.sels) if side.use_mxu else 0
    return _compact_tasks(side_in[:N_GROUPS], side_in[N_GROUPS:N_GROUPS + n_sel], side_out, side.use_mxu)


def _host_call(kernel, side, step_of, n_in, n_out, *, grid, in_specs, out_specs, out_shape, args,
               scratch_shapes, name):
    aliases = {}
    if side is not None:
        s_in, s_args, s_out, s_shape = _side_io(side, step_of)
        if side.kc is not None:
            aliases = {len(in_specs) + len(s_in) - 1: len(out_specs)}
        in_specs, args = in_specs + s_in, args + s_args
        out_specs, out_shape = out_specs + [s_out], out_shape + [s_shape]
    assert len(in_specs) == n_in + (0 if side is None else len(s_in))
    res = pl.pallas_call(
        functools.partial(kernel, side), grid=grid, in_specs=in_specs, out_specs=out_specs,
        out_shape=out_shape, scratch_shapes=scratch_shapes, input_output_aliases=aliases,
        compiler_params=_cparams(len(grid)), name=name,
    )(*args)
    return (res[:n_out], res[n_out]) if side is not None else (res, None)


def _ffn_core(x1, gf_ref, wup_ref, cw_ref, cb_ref, wdown_ref, hbuf, act, fc, rows, side_tasks=()):
    n_blocks, to_rows, prev_rows, from_rows = rows
    d_ff = wdown_ref.shape[0]
    q = hbuf.shape[0] // n_blocks
    blocks = [slice(j * q, (j + 1) * q) for j in range(n_blocks)]
    h = _rms(x1, gf_ref[...])
    for j, rb in enumerate(blocks):
        hbuf[rb, :] = to_rows(h, j).astype(BF16)
    for c in range(d_ff // fc):
        halves = []
        for base in (0, d_ff):
            cols = slice(base + c * fc, base + (c + 1) * fc)
            u = jnp.dot(hbuf[...], wup_ref[:, cols], preferred_element_type=F32)
            ub = [u[rb] for rb in blocks]
            prev = prev_rows(cols, ub)
            halves.append([cb_ref[:, cols] + ((cw_ref[0:1, cols] * u2 + cw_ref[1:2, cols] * u1)
                                              + cw_ref[2:3, cols] * u0)
                           for u0, (u1, u2) in zip(ub, prev)])
        for rb, a, b in zip(blocks, *halves):
            act[rb, c * fc:(c + 1) * fc] = (_silu(a) * b).astype(BF16)
        _drain(side_tasks, d_ff // fc, c)
    return x1 + from_rows(jnp.dot(act[...], wdown_ref[...], preferred_element_type=F32))


PROMPT_ROW_BLOCKS = 4


def _prompt_rows(slab, sbuf, ucarry, tm):
    nb = PROMPT_ROW_BLOCKS
    q = tm // nb
    n_slab = slab.shape[0]

    def to_rows(h, j):
        if j == 0:
            for k in range(n_slab):
                slab[k] = h[:, k * LANES:(k + 1) * LANES]
        return jnp.concatenate([slab[k, pl.ds(j, q, stride=nb), :] for k in range(n_slab)], axis=1)

    def shift_down(slot, blk, carry_row):
        sbuf[slot, SUBLANES - 1:SUBLANES, :] = carry_row
        sbuf[slot, SUBLANES:SUBLANES + q, :] = blk
        return sbuf[slot, SUBLANES - 1:SUBLANES - 1 + q, :]

    def prev_rows(cols, ub):
        s3 = shift_down(0, ub[3], ucarry[SUBLANES - 1:SUBLANES, cols])
        s2 = shift_down(1, ub[2], ucarry[SUBLANES - 2:SUBLANES - 1, cols])
        ucarry[SUBLANES - 1:SUBLANES, cols] = ub[3][q - 1:q]
        ucarry[SUBLANES - 2:SUBLANES - 1, cols] = ub[2][q - 1:q]
        return [(s3, s2), (ub[0], s3), (ub[1], ub[0]), (ub[2], ub[1])]

    def from_rows(y):
        for j in range(nb):
            for k in range(n_slab):
                slab[k, pl.ds(j, q, stride=nb), :] = y[j * q:(j + 1) * q, k * LANES:(k + 1) * LANES]
        return jnp.concatenate([slab[k] for k in range(n_slab)], axis=1)

    return nb, to_rows, prev_rows, from_rows


def _sample_rows(hist_ref, newhist_ref):
    def prev_rows(cols, ub):
        newhist_ref[0, :, cols] = hist_ref[1, :, cols]
        newhist_ref[1, :, cols] = ub[0]
        return [(hist_ref[1, :, cols], hist_ref[0, :, cols])]
    return 1, (lambda h, j: h), prev_rows, (lambda y: y)


def _pool_mix(xn, window_sum, cnt_of, wp_ref, ps_ref):
    gw = xn.shape[1] // len(POOL_WINDOWS)
    parts = []
    for g, w in enumerate(POOL_WINDOWS):
        sl = slice(g * gw, (g + 1) * gw)
        cur = xn[:, sl]
        d = window_sum(w, sl, cur) / cnt_of(w, cur.shape) - cur
        parts.append(jnp.dot(d.astype(BF16), wp_ref[g], preferred_element_type=F32))
    return jnp.concatenate(parts, axis=-1) * ps_ref[...]


POOL_PAD = 2 * SUBLANES


def _prompt_window_sum(xnbuf, lv, tm):
    hist = 2 * SUBLANES
    span = hist + tm

    def window_sum(w, sl, cur):
        s = xnbuf[POOL_PAD:POOL_PAD + span, sl] + xnbuf[POOL_PAD - 1:POOL_PAD - 1 + span, sl]
        shift, slot = 2, 0
        while shift < w:
            lv[slot, POOL_PAD:POOL_PAD + span, :] = s
            s = lv[slot, POOL_PAD:POOL_PAD + span, :] + lv[slot, POOL_PAD - shift:POOL_PAD - shift + span, :]
            shift, slot = 2 * shift, 1 - slot
        return s[hist:]
    return window_sum


def _prompt_layer_kernel(mode, tm, fc, n_cast, side, *refs):
    ins, side_in, outs, side_out, scratch = _split_side(refs, (9 if mode == "pool" else 8) + n_cast,
                                                        (3 if mode == "pool" else 2) + n_cast, side)
    tasks = _side_tasks(side, side_in, side_out)
    if n_cast:
        (ins, cast_in), (outs, cast_out) = (ins[:-n_cast], ins[-n_cast:]), (outs[:-n_cast], outs[-n_cast:])
        tasks = tasks + _cast_tasks(cast_in, cast_out)
    if mode == "pool":
        x_ref, gp_ref, wp_ref, ps_ref, gf_ref, wup_ref, cw_ref, cb_ref, wdown_ref = ins
        xo_ref, ptail_ref, ctail_ref = outs
        xnbuf, lv, hbuf, act, slab, sbuf, ucarry = scratch
    else:
        x_ref, o_ref, wo_ref, gf_ref, wup_ref, cw_ref, cb_ref, wdown_ref = ins
        xo_ref, ctail_ref = outs
        hbuf, act, slab, sbuf, ucarry = scratch
    t = pl.program_id(1)
    hist = 2 * SUBLANES

    @pl.when(t == 0)
    def _():
        ucarry[...] = jnp.zeros_like(ucarry)
        if mode == "pool":
            xnbuf[0:POOL_PAD + hist, :] = jnp.zeros((POOL_PAD + hist, xnbuf.shape[1]), F32)
            lv[:, 0:POOL_PAD, :] = jnp.zeros((lv.shape[0], POOL_PAD, lv.shape[2]), F32)

    x = x_ref[...]
    if mode == "pool":
        xn = _rms(x, gp_ref[...])
        xnbuf[POOL_PAD + hist:POOL_PAD + hist + tm, :] = xn

        def cnt_of(w, shape):
            pos = t * tm + lax.broadcasted_iota(jnp.int32, shape, 0)
            return jnp.minimum(pos + 1, w).astype(F32)

        x1 = x + _pool_mix(xn, _prompt_window_sum(xnbuf, lv, tm), cnt_of, wp_ref, ps_ref)
        tail = xnbuf[POOL_PAD + tm:POOL_PAD + tm + hist, :]
        ptail_ref[...] = tail
        xnbuf[POOL_PAD:POOL_PAD + hist, :] = tail
    else:
        x1 = x + jnp.dot(o_ref[...].astype(BF16), wo_ref[...], preferred_element_type=F32)

    xo_ref[...] = _ffn_core(x1, gf_ref, wup_ref, cw_ref, cb_ref, wdown_ref, hbuf, act, fc,
                            _prompt_rows(slab, sbuf, ucarry, tm), tasks)
    ctail_ref[...] = ucarry[...]


def _sample_layer_kernel(mode, fc, n_chained, *refs):
    n_in = 11 if mode == "pool" else 9
    refs = refs[:n_in] + refs[n_in + n_chained:]
    if mode == "pool":
        (x_ref, st_ref, gp_ref, wp_ref, ps_ref, gf_ref, wup_ref, cw_ref, cb_ref, wdown_ref,
         hist_ref, xo_ref, nst_ref, nhist_ref, hbuf, act) = refs
    else:
        (x_ref, o_ref, wo_ref, gf_ref, wup_ref, cw_ref, cb_ref, wdown_ref,
         hist_ref, xo_ref, nhist_ref, hbuf, act) = refs
    x = x_ref[...]
    if mode == "pool":
        xn = _rms(x, gp_ref[...])
        n_hist = st_ref.shape[0]

        def window_sum(w, sl, cur):
            s = cur
            for k in range(1, w):
                s = s + st_ref[n_hist - k, :, sl]
            return s

        def cnt_of(w, shape):
            return float(min(PAST_LEN + 1, w))

        x1 = x + _pool_mix(xn, window_sum, cnt_of, wp_ref, ps_ref)
        for i in range(n_hist - 1):
            nst_ref[i] = st_ref[i + 1]
        nst_ref[n_hist - 1] = xn
    else:
        x1 = x + jnp.dot(o_ref[...].astype(BF16), wo_ref[...], preferred_element_type=F32)
    xo_ref[...] = _ffn_core(x1, gf_ref, wup_ref, cw_ref, cb_ref, wdown_ref, hbuf, act, fc,
                            _sample_rows(hist_ref, nhist_ref))


MXU_TILE = 256


def _pick_fc(f):
    for fc in (MXU_TILE, MXU_TILE // 2):
        if f % fc == 0:
            return fc
    return f


def _cast_rows(d, f, steps):
    if steps % 2 or d % steps or f % (steps // 2):
        return None
    ru, rd = d // steps, f // (steps // 2)
    return (ru, rd) if ru % (2 * SUBLANES) == 0 and rd % (2 * SUBLANES) == 0 else None


def _cast_io(cast, d, f, steps, step_of):
    weights, nl = cast
    in_specs, args, out_specs, out_shape = [], [], [], []
    for w_f, (rows, span) in zip(weights, zip(_cast_rows(d, f, steps), (1, 2))):
        blk = lambda *idx, span=span: step_of(*idx) // span
        in_specs.append(pl.BlockSpec((None, rows, w_f.shape[2]), lambda *idx, blk=blk: (nl, blk(*idx), 0)))
        args.append(w_f)
        out_specs.append(pl.BlockSpec((rows, w_f.shape[2]), lambda *idx, blk=blk: (blk(*idx), 0)))
        out_shape.append(jax.ShapeDtypeStruct(w_f.shape[1:], BF16))
    return in_specs, args, out_specs, out_shape


def _cast_tasks(cast_in, cast_out):
    def task(src, dst):
        def run():
            dst[...] = src[...].astype(BF16)
        return run
    return [task(s_, d_) for s_, d_ in zip(cast_in, cast_out)]


CAST_STEPS = 16


def _cast_kernel(*refs):
    for task in _cast_tasks(refs[:len(refs) // 2], refs[len(refs) // 2:]):
        task()


def _cast_layer(w_up, w_down, l):
    d, f = w_up.shape[1], w_down.shape[1]
    if _cast_rows(d, f, CAST_STEPS) is None:
        return w_up[l].astype(BF16), w_down[l].astype(BF16)
    in_specs, args, out_specs, out_shape = _cast_io(((w_up, w_down), l), d, f, CAST_STEPS, lambda i: i)
    return tuple(pl.pallas_call(_cast_kernel, grid=(CAST_STEPS,), in_specs=in_specs, out_specs=out_specs,
                                out_shape=out_shape, compiler_params=_cparams(1), name="cast_ffn_weights")(*args))


def _prompt_layer(mode, x, pre, pre_l, ffn_w, tm, side, cast=None):
    b, s, d = x.shape
    f2, f = ffn_w[1][0].shape[-1], ffn_w[4][0].shape[-2]
    fc = _pick_fc(f)
    row = pl.BlockSpec((None, tm, d), lambda i, j: (i, j, 0))
    hist = 2 * SUBLANES
    ctail_spec = pl.BlockSpec((None, SUBLANES, f2), lambda i, j: (i, 0, 0))
    ctail_shape = jax.ShapeDtypeStruct((b, SUBLANES, f2), F32)
    common_scratch = [pltpu.VMEM((tm, d), BF16), pltpu.VMEM((tm, f), BF16),
                      pltpu.VMEM((d // LANES, tm, LANES), F32),
                      pltpu.VMEM((2, tm // PROMPT_ROW_BLOCKS + SUBLANES, fc), F32),
                      pltpu.VMEM((SUBLANES, f2), F32)]
    if mode == "pool":
        in_specs = [row] + [_layer_spec(a, pre_l) for a in pre]
        args = [x, *pre]
        out_specs = [row, pl.BlockSpec((None, hist, d), lambda i, j: (i, 0, 0)), ctail_spec]
        out_shape = [jax.ShapeDtypeStruct(x.shape, F32), jax.ShapeDtypeStruct((b, hist, d), F32),
                     ctail_shape]
        ext = POOL_PAD + hist + tm
        scratch = [pltpu.VMEM((ext, d), F32),
                   pltpu.VMEM((2, ext, d // len(POOL_WINDOWS)), F32)] + common_scratch
    else:
        o, wo = pre
        in_specs = [row, pl.BlockSpec((None, tm, o.shape[2]), lambda i, j: (i, j, 0)),
                    _layer_spec(wo, pre_l)]
        args = [x, o, wo]
        out_specs = [row, ctail_spec]
        out_shape = [jax.ShapeDtypeStruct(x.shape, F32), ctail_shape]
        scratch = common_scratch
    nj = s // tm
    step_of = lambda i, j: i * nj + j
    in_specs = in_specs + [_layer_spec(a, l) for a, l in ffn_w]
    args = args + [a for a, _ in ffn_w]
    n_cast = 0
    if cast is not None:
        c_in, c_args, c_out, c_shape = _cast_io(cast, d, f, b * nj, step_of)
        in_specs, args = in_specs + c_in, args + c_args
        out_specs, out_shape = out_specs + c_out, out_shape + c_shape
        n_cast = len(c_in)
    return _host_call(
        functools.partial(_prompt_layer_kernel, mode, tm, fc, n_cast), side, step_of,
        len(in_specs), len(out_specs),
        grid=(b, nj), in_specs=in_specs, out_specs=out_specs, out_shape=out_shape, args=args,
        scratch_shapes=scratch, name=f"prompt_{mode}_layer")


def _sample_layer(mode, x, pre, pre_l, ffn_w, l, hist_t, bn, chained):
    n, d = x.shape
    f2, f = ffn_w[1][0].shape[-1], ffn_w[4][0].shape[-2]
    fc = _pick_fc(f)
    row = lambda width: pl.BlockSpec((bn, width), lambda i: (i, 0))
    tmaj = lambda a, k: pl.BlockSpec((None, a.shape[1], bn, a.shape[3]), lambda i: (k, 0, i, 0))
    if mode == "pool":
        st_t = pre[0]
        in_specs = [row(d), tmaj(st_t, pre_l)] + [_layer_spec(a, pre_l) for a in pre[1:]]
        args = [x, *pre]
        states = [(st_t, pre_l), (hist_t, l)]
    else:
        o, wo = pre
        in_specs = [row(d), row(o.shape[1]), _layer_spec(wo, pre_l)]
        args = [x, o, wo]
        states = [(hist_t, l)]
    in_specs = in_specs + [_layer_spec(a, k) for a, k in ffn_w] + [tmaj(hist_t, l)]
    args = args + [a for a, _ in ffn_w] + [hist_t]
    out_specs = [row(d)] + [tmaj(a, k) for a, k in states]
    out_shape = [jax.ShapeDtypeStruct(x.shape, F32)] + [jax.ShapeDtypeStruct(a.shape, F32) for a, _ in states]
    aliases = {}
    for k, prev in enumerate(chained):
        if prev is not None:
            aliases[len(in_specs)] = 1 + k
            in_specs.append(pl.BlockSpec(memory_space=pl.ANY))
            args.append(prev)
    return pl.pallas_call(
        functools.partial(_sample_layer_kernel, mode, fc, len(aliases)),
        grid=(n // bn,), in_specs=in_specs, out_specs=out_specs, out_shape=out_shape,
        scratch_shapes=[pltpu.VMEM((bn, d), BF16), pltpu.VMEM((bn, f), BF16)],
        input_output_aliases=aliases,
        compiler_params=_cparams(1), name=f"sample_{mode}_layer",
    )(*args)


HEAD_TILE = 2 * LANES


def _head_norm_rot(x, bd_ref, gain, tabs):
    gcos, sina, sinb = tabs
    ms = jnp.dot((x * x).astype(BF16), bd_ref[...], preferred_element_type=F32)
    outs = []
    for j in range(x.shape[1] // LANES):
        xs = x[:, j * LANES:(j + 1) * LANES]
        xg = xs * gain
        outs.append(xs * gcos + pltpu.roll(xg, LANES - ROT_DIM // 2, 1) * sina
                    + pltpu.roll(xg, ROT_DIM // 2, 1) * sinb)
    return jnp.concatenate(outs, axis=1) * lax.rsqrt(ms + NORM_EPS)


def _proj_kernel(do_kv, kvt_groups, n_tab_blocks, n_cast, side, *refs):
    n_out = ((N_GROUPS + 1 + len(kvt_groups)) if do_kv else 1) + n_cast
    ins, side_in, outs, side_out, _ = _split_side(refs, (14 if do_kv else 8) + n_cast, n_out, side)
    side_tasks = _side_tasks(side, side_in, side_out)
    if n_cast:
        side_tasks = side_tasks + _cast_tasks(ins[-n_cast:], outs[-n_cast:])
        ins, outs = ins[:-n_cast], outs[:-n_cast]
    x_ref, bd_ref = ins[0:2]
    if do_kv:
        ktabs, (gkv_ref, wkv_ref, kg_ref) = ins[2:5], ins[5:8]
        qtabs, (gq_ref, wq_ref, qg_ref) = ins[8:11], ins[11:14]
        kv_refs, q_ref, kvt_refs = outs[0:N_GROUPS], outs[N_GROUPS], outs[N_GROUPS + 1:]
    else:
        qtabs, (gq_ref, wq_ref, qg_ref) = ins[2:5], ins[5:8]
        q_ref = outs[0]
    x = x_ref[...]
    gw = HEADS * HEAD_DIM
    if do_kv:
        h = _rms(x, gkv_ref[...]).astype(BF16)
        kv = jnp.dot(h, wkv_ref[...], preferred_element_type=F32)
        tabs = [t[...] for t in ktabs]
        for g, out_ref in enumerate(kv_refs):
            for j in range(gw // HEAD_TILE):
                c0 = g * gw + j * HEAD_TILE
                out_ref[:, j * HEAD_TILE:(j + 1) * HEAD_TILE] = _head_norm_rot(
                    kv[:, c0:c0 + HEAD_TILE], bd_ref, kg_ref[...], tabs)
            v0 = N_GROUPS * gw + g * gw
            out_ref[:, gw:2 * gw] = kv[:, v0:v0 + gw]
        tm = x.shape[0]
        for (g, win), kvt_ref in zip(kvt_groups, kvt_refs):
            if win == n_tab_blocks * tm:
                kvt_ref[...] = kv_refs[g][...].T
            else:
                @pl.when(pl.program_id(0) % n_tab_blocks == n_tab_blocks - 1)
                def _(g=g, win=win, kvt_ref=kvt_ref):
                    kvt_ref[...] = kv_refs[g][tm - win:tm, :].T
    h = _rms(x, gq_ref[...]).astype(BF16)
    q = jnp.dot(h, wq_ref[...], preferred_element_type=F32)
    tabs = [t[...] for t in qtabs]
    n_tiles = q.shape[1] // HEAD_TILE
    for j in range(n_tiles):
        cols = slice(j * HEAD_TILE, (j + 1) * HEAD_TILE)
        q_ref[:, cols] = _head_norm_rot(q[:, cols], bd_ref, qg_ref[...], tabs)
        _drain(side_tasks, n_tiles, j)


def _proj(x2d, n_tab_blocks, kv_w, q_w, tm, kvt_groups=(), side=None, cast=None):
    r, d = x2d.shape
    row = pl.BlockSpec((tm, d), lambda i: (i, 0))
    tab = pl.BlockSpec((tm, LANES), lambda i: (i % n_tab_blocks, 0))
    gw = HEADS * HEAD_DIM
    head = lax.broadcasted_iota(jnp.int32, (HEAD_TILE, HEAD_TILE), 0) // HEAD_DIM
    bd = jnp.where(head == head.T, 1.0 / HEAD_DIM, 0.0).astype(BF16)
    in_specs, args = [row, _const_spec(bd.shape)], [x2d, bd]
    out_specs, out_shape = [], []
    for w in (kv_w, q_w):
        if w is None:
            continue
        tabs, g, wmat, hg, layer = w
        wspec = lambda a, layer=layer: _const_spec(a.shape) if layer is None else _layer_spec(a, layer)
        in_specs += [tab, tab, tab, wspec(g), wspec(wmat), _const_spec((1, LANES))]
        args += [*tabs, g, wmat, hg]
    if kv_w is not None:
        out_specs += [pl.BlockSpec((tm, 2 * gw), lambda i: (i, 0))] * N_GROUPS
        out_shape += [jax.ShapeDtypeStruct((r, 2 * gw), F32)] * N_GROUPS
    n_q = q_w[2].shape[-1]
    out_specs.append(pl.BlockSpec((tm, n_q), lambda i: (i, 0)))
    out_shape.append(jax.ShapeDtypeStruct((r, n_q), F32))
    seq = n_tab_blocks * tm
    if kv_w is not None:
        for _, win in kvt_groups:
            assert win == seq or win <= tm
            if win == seq:
                out_specs.append(pl.BlockSpec((None, 2 * gw, tm), lambda i: (i // n_tab_blocks, 0, i % n_tab_blocks)))
            else:
                out_specs.append(pl.BlockSpec((None, 2 * gw, win), lambda i: (i // n_tab_blocks, 0, 0)))
            out_shape.append(jax.ShapeDtypeStruct((r // seq, 2 * gw, win), F32))
    n_cast = 0
    if cast is not None:
        c_in, c_args, c_out, c_shape = _cast_io(cast, d, cast[0][1].shape[1], r // tm, lambda i: i)
        in_specs, args = in_specs + c_in, args + c_args
        out_specs, out_shape = out_specs + c_out, out_shape + c_shape
        n_cast = len(c_in)
    return _host_call(
        functools.partial(_proj_kernel, kv_w is not None, tuple(kvt_groups), n_tab_blocks, n_cast), side,
        lambda i: i, len(in_specs), len(out_specs),
        grid=(r // tm,), in_specs=in_specs, out_specs=out_specs, out_shape=out_shape, args=args,
        scratch_shapes=[], name="proj_kvq" if kv_w is not None else "proj_q")


def _prompt_attn_kernel(seq, side, *refs):
    ins, side_in, outs, side_out, scratch = _split_side(refs, 3 * N_GROUPS, 1, side)
    q_refs, k_refs, v_refs = ins[0:3], ins[3:6], ins[6:9]
    o_ref = outs[0]
    qb, kb, vb, tmpf, accs, ms, ls = scratch
    blk = ATTN_BLOCK
    lo = lax.broadcasted_iota(jnp.int32, (blk, LANES), 1) < HEAD_DIM
    quarter = seq // 4
    side_tasks = _side_tasks(side, side_in, side_out)
    n_blocks_total = sum(seq // blk for _ in ATTN_GROUPS)
    done_blocks = 0

    def valid_mask(nk):
        qi = lax.broadcasted_iota(jnp.int32, (2 * blk, nk), 0) & (blk - 1)
        kj = lax.broadcasted_iota(jnp.int32, (2 * blk, nk), 1)
        if nk == blk:
            return kj <= qi
        return lax.bitcast_convert_type(kj - qi, jnp.uint32) <= jnp.uint32(blk)

    for g, (win, dil) in enumerate(ATTN_GROUPS):
        assert win // dil == blk and dil in (1, 4, 16) and seq % (blk * dil) == 0
        sub = seq // dil
        nb = sub // blk

        def stage(src, dil=dil):
            if dil == 16:
                for r1 in range(4):
                    tmpf[r1 * quarter:(r1 + 1) * quarter, :] = src[pl.ds(r1, quarter, stride=4), :]

        def rows_of(src, r, n0, cnt, dil=dil):
            if dil == 1:
                return src[pl.ds(n0, cnt), :]
            if dil == 4:
                return src[pl.ds(r + 4 * n0, cnt, stride=4), :]
            return tmpf[pl.ds((r % 4) * quarter + r // 4 + 4 * n0, cnt, stride=4), :]

        stage(q_refs[g])
        for r in range(dil):
            for n in range(nb):
                qf = rows_of(q_refs[g], r, n * blk, blk)
                base = (r * nb + n) * 2 * blk
                qb[base:base + blk, :] = jnp.where(lo, qf, 0.0).astype(BF16)
                qb[base + blk:base + 2 * blk, :] = jnp.where(lo, 0.0, qf).astype(BF16)
        for src, dst in ((k_refs[g], kb), (v_refs[g], vb)):
            stage(src)
            for r in range(dil):
                dst[r * sub:(r + 1) * sub, :] = rows_of(src, r, 0, sub).astype(BF16)

        def block(r, n, first, g=g, dil=dil, sub=sub, nb=nb):
            q2 = qb[pl.ds((r * nb + n) * 2 * blk, 2 * blk), :]
            nk = blk if first else 2 * blk
            k0 = r * sub + (n if first else n - 1) * blk
            kt = kb[pl.ds(k0, nk), :]
            vt = vb[pl.ds(k0, nk), :]
            s = lax.dot_general(q2, kt, (((1,), (1,)), ((), ())), preferred_element_type=F32)
            s = jnp.where(valid_mask(nk), s, NEG_INF)
            m = jnp.max(s, axis=-1, keepdims=True)
            p = jnp.exp(s - m)
            l = jnp.sum(p, axis=-1, keepdims=True)
            pv = jnp.dot(p.astype(BF16), vt, preferred_element_type=F32)
            if dil > 1:
                out_rows = pl.ds(r + n * blk * dil, blk, stride=dil)
            else:
                out_rows = pl.ds(n * blk, blk)
            accs[g, out_rows, :] = jnp.where(lo, pv[:blk], pv[blk:])
            ms[g, out_rows, :] = jnp.where(lo, m[:blk], m[blk:])
            ls[g, out_rows, :] = jnp.where(lo, l[:blk], l[blk:])

        for r in range(dil):
            for n in range(nb):
                block(r, n, n == 0)
                _drain(side_tasks, n_blocks_total, done_blocks)
                done_blocks += 1

    def merge(i, carry):
        rows = pl.ds(pl.multiple_of(i * blk, blk), blk)
        m = jnp.maximum(jnp.maximum(ms[0, rows, :], ms[1, rows, :]), ms[2, rows, :])
        num = jnp.zeros((blk, LANES), F32)
        den = jnp.zeros((blk, LANES), F32)
        for g in range(N_GROUPS):
            w = jnp.exp(ms[g, rows, :] - m)
            num = num + w * accs[g, rows, :]
            den = den + w * ls[g, rows, :]
        o_ref[rows, :] = (num / den).astype(o_ref.dtype)
        return carry

    lax.fori_loop(0, seq // blk, merge, 0)


def _prompt_attn(q, kvs, side):
    b, s, _ = q.shape
    gw = HEADS * HEAD_DIM
    n_hp = gw // LANES
    blk_spec = lambda col: pl.BlockSpec((None, s, LANES), lambda i, j, col=col: (i, 0, col + j))
    in_specs = ([blk_spec(g * n_hp) for g in range(N_GROUPS)]
                + [blk_spec(0)] * N_GROUPS
                + [blk_spec(n_hp)] * N_GROUPS)
    args = [q] * N_GROUPS + list(kvs) + list(kvs)
    (o,), kc = _host_call(
        functools.partial(_prompt_attn_kernel, s), side, lambda i, j: i * n_hp + j, len(in_specs), 1,
        grid=(b, n_hp), in_specs=in_specs,
        out_specs=[pl.BlockSpec((None, s, LANES), lambda i, j: (i, 0, j))],
        out_shape=[jax.ShapeDtypeStruct((b, s, gw), BF16)], args=args,
        scratch_shapes=[pltpu.VMEM((2 * s, LANES), BF16),
                        pltpu.VMEM((s, LANES), BF16),
                        pltpu.VMEM((s, LANES), BF16),
                        pltpu.VMEM((s, LANES), F32)]
                       + [pltpu.VMEM((N_GROUPS, s, LANES), F32)] * 3,
        name="prompt_attn")
    return o, kc


def _compact_kernel(side, *refs):
    _, side_in, _, side_out, _ = _split_side(refs, 0, 0, side)
    for task in _side_tasks(side, side_in, side_out):
        task()


def _compact_rest(side, n_units):
    _, kc = _host_call(_compact_kernel, side, lambda i: i, 0, 0, grid=(n_units,), in_specs=[],
                       out_specs=[], out_shape=[], args=[], scratch_shapes=[], name="compact_caches")
    return kc


def _sample_attn_kernel(bb, q_ref, kv0_ref, kv1_ref, kv2_ref, kc_ref, o_ref):
    gw = HEADS * HEAD_DIM
    head_of_row = lax.broadcasted_iota(jnp.int32, (HEADS, gw), 0)
    head_of_col = lax.broadcasted_iota(jnp.int32, (HEADS, gw), 1) // HEAD_DIM
    diag = head_of_row == head_of_col
    nk = kc_ref.shape[-1] // N_GROUPS
    kvn_refs = (kv0_ref, kv1_ref, kv2_ref)

    def body(b, carry):
        row = pl.ds(b, 1)
        q_bd = [jnp.where(diag, q_ref[row, g * gw:(g + 1) * gw], 0.0) for g in range(N_GROUPS)]
        s_all = jnp.dot(jnp.concatenate(q_bd, axis=0).astype(BF16), kc_ref[b, 0],
                        preferred_element_type=F32)
        s = [s_all[g * HEADS:(g + 1) * HEADS, g * nk:(g + 1) * nk] for g in range(N_GROUPS)]
        s_new = [jnp.sum(q_bd[g] * kvn_refs[g][row, 0:gw], axis=-1, keepdims=True)
                 for g in range(N_GROUPS)]
        m = functools.reduce(jnp.maximum, [jnp.max(a, axis=-1, keepdims=True) for a in s] + s_new)
        p = [jnp.exp(a - m) for a in s]
        p_new = [jnp.exp(a - m) for a in s_new]
        l = sum(jnp.sum(a, axis=-1, keepdims=True) for a in p) + sum(p_new)
        acc = lax.dot_general(jnp.concatenate(p, axis=1).astype(BF16), kc_ref[b, 1],
                              (((1,), (1,)), ((), ())), preferred_element_type=F32)
        for g in range(N_GROUPS):
            acc = acc + p_new[g] * kvn_refs[g][row, gw:2 * gw]
        o_ref[row, :] = jnp.sum(jnp.where(diag, acc / l, 0.0), axis=0, keepdims=True)
        return carry

    lax.fori_loop(0, bb, body, 0, unroll=True)


def _sample_attn(q, kvn, kc, bb):
    n = q.shape[0]
    gw = HEADS * HEAD_DIM
    rows = lambda width: pl.BlockSpec((bb, width), lambda i: (i, 0))
    return pl.pallas_call(
        functools.partial(_sample_attn_kernel, bb),
        grid=(n // bb,),
        in_specs=[rows(q.shape[1])] + [rows(2 * gw)] * N_GROUPS
                 + [pl.BlockSpec((bb,) + kc.shape[1:], lambda i: (i, 0, 0, 0))],
        out_specs=rows(gw),
        out_shape=jax.ShapeDtypeStruct((n, gw), F32),
        compiler_params=_cparams(1), name="sample_attn",
    )(q, *kvn, kc)


def _rot_tables(pos):
    half = ROT_DIM // 2
    lane = lax.broadcasted_iota(jnp.int32, (pos.shape[0], LANES), 1) % HEAD_DIM
    inv_freq = ROPE_THETA ** (-(2 * (lane % half)).astype(F32) / ROT_DIM)
    ang = pos.astype(F32)[:, None] * inv_freq
    cos, sin = jnp.cos(ang), jnp.sin(ang)
    cosb = jnp.where(lane < ROT_DIM, cos, 1.0)
    sina = jnp.where(lane < half, -sin, 0.0)
    sinb = jnp.where((lane >= half) & (lane < ROT_DIM), sin, 0.0)
    return cosb, sina, sinb


def kernel(x_prompt, x_sample, state_pool, state_conv, cache_kv_w128, cache_kv_w512, cache_kv_w2048,
           g_pool, w_pool, pool_scale, g_ffn, w_up, conv_w, conv_b, w_down, g_kv, w_kv, k_gain,
           g_attn, w_q, q_gain, w_o):
    b, s, d = x_prompt.shape
    n = x_sample.shape[0]
    depth = w_up.shape[0]
    n_pool = g_pool.shape[0]
    gw = HEADS * HEAD_DIM
    rows3 = lambda a: a.reshape(a.shape[0], 1, a.shape[1])
    lane2 = lambda a: jnp.tile(a, LANES // HEAD_DIM).reshape(1, LANES)
    tm = min(512, s)
    bn = min(64, n)
    bb = min(16, n)

    gf3, cb3 = rows3(g_ffn), rows3(conv_b)
    ffn_mats = {}

    def ffn_w(l):
        if l not in ffn_mats:
            ffn_mats[l] = _cast_layer(w_up, w_down, l)
        wup_l, wdown_l = ffn_mats[l]
        return [(gf3, l), (wup_l, None), (conv_w, l), (cb3, l), (wdown_l, None)]
    pool_w = (rows3(g_pool), w_pool.astype(BF16), rows3(pool_scale))
    wkv_b, wq_b, wo_b = w_kv.astype(BF16), w_q.astype(BF16), w_o.astype(BF16)
    q_scale = HEAD_DIM ** -0.5

    def proj_w(pos):
        cosb, sina, sinb = _rot_tables(pos)
        kg = lane2(k_gain)
        kv_w = ((cosb * kg, sina, sinb), g_kv.reshape(1, d), wkv_b, kg, None)
        q_w = []
        for j in range(depth - n_pool):
            qg = lane2(q_gain[j])
            q_w.append(((cosb * (qg * q_scale), sina * q_scale, sinb * q_scale), rows3(g_attn), wq_b, qg, j))
        return kv_w, q_w

    kv_w_p, q_w_p = proj_w(jnp.arange(s, dtype=jnp.int32))
    kv_w_s, q_w_s = proj_w(jnp.full((n,), PAST_LEN, jnp.int32))

    c_args, sels = _side_plan((cache_kv_w128, cache_kv_w512, cache_kv_w2048))
    units = {"done": 0, "kc": None}

    def side_for(steps, per_step=1, use_mxu=False):
        if units["done"] + steps * per_step > 2 * n:
            return None
        side = _Side(c_args, sels, units["done"], units["kc"], per_step, use_mxu)
        units["done"] += steps * per_step
        return side

    def took(kc):
        if kc is not None:
            units["kc"] = kc

    xp = x_prompt
    pool_p, conv_p = [], []
    n_steps = b * (s // tm)
    can_cast = _cast_rows(d, w_down.shape[1], n_steps) is not None
    kvt_groups = [(g, min(win, s)) for g, (win, _) in enumerate(ATTN_GROUPS) if win >= s or win <= tm]
    kvs_p = kvts_p = None
    for l in range(depth):
        cast = ((w_up, w_down), l) if can_cast and l >= n_pool else None
        if l < n_pool:
            (xp, ptail, ctail), kc = _prompt_layer("pool", xp, pool_w, l, ffn_w(l), tm, side_for(n_steps, 2))
            took(kc)
            pool_p.append(ptail[:, 2 * SUBLANES - POOL_HIST:])
        else:
            j = l - n_pool
            x2d = xp.reshape(b * s, d)
            if j == 0:
                res, kc = _proj(x2d, s // tm, kv_w_p, q_w_p[j], tm, kvt_groups, None, cast)
                kvs_p, q_p = [a.reshape(b, s, 2 * gw) for a in res[:N_GROUPS]], res[N_GROUPS]
                kvts_p = dict(zip((g for g, _ in kvt_groups), res[N_GROUPS + 1:]))
                nxt = res[N_GROUPS + 1 + len(kvt_groups):]
            else:
                (q_p, *nxt), kc = _proj(x2d, s // tm, None, q_w_p[j], tm, (), None, cast)
            if cast:
                ffn_mats[l] = tuple(nxt)
            o_p, kc = _prompt_attn(q_p.reshape(b, s, N_GROUPS * gw), kvs_p, None)
            took(kc)
            (xp, ctail), kc = _prompt_layer("attn", xp, (o_p, wo_b), j, ffn_w(l), tm, side_for(n_steps, 2))
            took(kc)
        conv_p.append(ctail[:, SUBLANES - (CONV_W - 1):])
    if units["done"] < 2 * n:
        took(_compact_rest(_Side(c_args, sels, units["done"], units["kc"], 1, True), 2 * n - units["done"]))
    kc = units["kc"]

    xs = x_sample.reshape(n, d)
    hist_t = jnp.swapaxes(state_conv, 1, 2)
    st_t = jnp.swapaxes(state_pool, 1, 2)
    kvs_s = nst = nhist = None
    for l in range(depth):
        if l < n_pool:
            xs, nst, nhist = _sample_layer("pool", xs, (st_t,) + pool_w, l, ffn_w(l), l, hist_t, bn,
                                           (nst, nhist))
        else:
            j = l - n_pool
            if j == 0:
                (*kvs_s, q_s), _ = _proj(xs, 1, kv_w_s, q_w_s[j], n)
            else:
                (q_s,), _ = _proj(xs, 1, None, q_w_s[j], n)
            o_s = _sample_attn(q_s, kvs_s, kc, bb)
            xs, nhist = _sample_layer("attn", xs, (o_s, wo_b), j, ffn_w(l), l, hist_t, min(2 * bn, n), (nhist,))

    outs = [xp, xs.reshape(n, 1, d), jnp.stack(pool_p), jnp.swapaxes(nst, 1, 2),
            jnp.stack(conv_p), jnp.swapaxes(nhist, 1, 2)]
    for g, (win, _) in enumerate(ATTN_GROUPS):
        if g in kvts_p:
            rows_p = jnp.transpose(kvts_p[g].reshape(b, 2, HEADS, HEAD_DIM, min(win, s)), (0, 4, 1, 2, 3))
        else:
            rows_p = kvs_p[g][:, s - min(win, s):].reshape(b, min(win, s), 2, HEADS, HEAD_DIM)
        outs += [rows_p, kvs_s[g].reshape(n, 1, 2, HEADS, HEAD_DIM)]
    return tuple(outs)
```
